```python
import math
import jax, jax.numpy as jnp
from jax import lax
import numpy as np

D_MODEL = 2048
BATCH = 1
SEQ = 8192
DEPTH = 4

HEAD_DIM = 64
SWA_Q_HEADS = D_MODEL // 128
SWA_KV_HEADS = SWA_Q_HEADS // 4
WINDOW = 128
SB_HEADS = D_MODEL // 128
DIFF_HEADS = D_MODEL // 256
DIFF_DIM = HEAD_DIM
BLOCK_Q = 128
BRANCH_WIDTH = SWA_Q_HEADS * HEAD_DIM
A_KV = SWA_KV_HEADS * HEAD_DIM
N_BRANCHES = 3
IN_SPLITS = (BRANCH_WIDTH, A_KV, A_KV, BRANCH_WIDTH, BRANCH_WIDTH, BRANCH_WIDTH,
             BRANCH_WIDTH, BRANCH_WIDTH, BRANCH_WIDTH)
D_IN = 7 * BRANCH_WIDTH + 2 * A_KV
N_GROUPS = 8
EXPERTS_PER_GROUP = 8
N_EXPERTS = N_GROUPS * EXPERTS_PER_GROUP
TOP_K_EXPERT = 2
D_EXPERT = 3 * D_MODEL // 16
DISPATCH_BLOCK = 128
ALPHA = (2.0 * DEPTH) ** 0.25
BETA = (8.0 * DEPTH) ** -0.25
LN_EPS = 1e-5
RMS_EPS = 1e-5

kernel_name = "hybrid_gated_swa_stickbreak_diffattn_hmoe"


def layer_norm(x, g, b):
    xf = x.astype(jnp.float32)
    mu = jnp.mean(xf, axis=-1, keepdims=True)
    var = jnp.mean(jnp.square(xf - mu), axis=-1, keepdims=True)
    y = (xf - mu) * lax.rsqrt(var + LN_EPS)
    return (y * g.astype(jnp.float32) + b.astype(jnp.float32)).astype(x.dtype)


def rms_norm(x, g):
    xf = x.astype(jnp.float32)
    y = xf * lax.rsqrt(jnp.mean(jnp.square(xf), axis=-1, keepdims=True) + RMS_EPS)
    return (y * g.astype(jnp.float32)).astype(x.dtype)


def alibi_slopes(n_heads):
    return jnp.exp2(-8.0 * jnp.arange(1, n_heads + 1, dtype=jnp.float32) / n_heads)


def swa_sink_attention(q, k, v, sinks):
    B_, S_ = q.shape[0], q.shape[1]
    nb = S_ // BLOCK_Q
    G = SWA_Q_HEADS // SWA_KV_HEADS
    qb = q.reshape(B_, nb, BLOCK_Q, SWA_KV_HEADS, G, HEAD_DIM)
    pad = ((0, 0), (BLOCK_Q, 0), (0, 0), (0, 0))
    kp = jnp.pad(k, pad).reshape(B_, nb + 1, BLOCK_Q, SWA_KV_HEADS, HEAD_DIM)
    vp = jnp.pad(v, pad).reshape(B_, nb + 1, BLOCK_Q, SWA_KV_HEADS, HEAD_DIM)
    kb = jnp.concatenate([kp[:, :-1], kp[:, 1:]], axis=2)
    vb = jnp.concatenate([vp[:, :-1], vp[:, 1:]], axis=2)
    s = jnp.einsum('bnqhgd,bnkhd->bnhgqk', qb, kb).astype(jnp.float32) / math.sqrt(HEAD_DIM)
    qi = jnp.arange(BLOCK_Q)[:, None]
    ki = jnp.arange(2 * BLOCK_Q)[None, :]
    dist = qi + BLOCK_Q - ki
    key_pos = jnp.arange(nb)[:, None] * BLOCK_Q - BLOCK_Q + jnp.arange(2 * BLOCK_Q)[None, :]
    valid = ((dist >= 0) & (dist < WINDOW))[None] & (key_pos >= 0)[:, None, :]
    slopes = alibi_slopes(SWA_Q_HEADS).reshape(SWA_KV_HEADS, G)
    s = s - slopes[:, :, None, None] * dist.astype(jnp.float32)
    s = jnp.where(valid[None, :, None, None], s, -jnp.inf)
    sink = jnp.broadcast_to(sinks.astype(jnp.float32).reshape(SWA_KV_HEADS, G)[None, None, :, :, None, None],
                            s.shape[:-1] + (1,))
    p = jax.nn.softmax(jnp.concatenate([s, sink], axis=-1), axis=-1)[..., :-1]
    o = jnp.einsum('bnhgqk,bnkhd->bnqhgd', p.astype(v.dtype), vb)
    return o.reshape(B_, S_, SWA_Q_HEADS * HEAD_DIM)


def stick_breaking_attention(q, k, v):
    B_, S_, H, Dh = q.shape
    nb = S_ // BLOCK_Q
    qb = jnp.moveaxis(q.reshape(B_, nb, BLOCK_Q, H, Dh), 1, 0)
    key_pos = jnp.arange(S_)
    scale = 1.0 / math.sqrt(Dh)

    def one_block(args):
        q_blk, n = args
        z = jnp.einsum('bqhd,bkhd->bhqk', q_blk, k).astype(jnp.float32) * scale
        q_pos = n * BLOCK_Q + jnp.arange(BLOCK_Q)
        causal = key_pos[None, :] < q_pos[:, None]
        log_rem = jnp.where(causal, jax.nn.log_sigmoid(-z), 0.0)
        after = lax.cumsum(log_rem, axis=3, reverse=True) - log_rem
        w = jnp.where(causal, jnp.exp(jax.nn.log_sigmoid(z) + after), 0.0)
        return jnp.einsum('bhqk,bkhd->bqhd', w.astype(v.dtype), v)

    o = lax.map(one_block, (qb, jnp.arange(nb)))
    return jnp.moveaxis(o, 0, 1).reshape(B_, S_, H * Dh)


def diff_attention(q, k, v, lam, lambda_init, subln_g):
    B_, S_, H = q.shape[0], q.shape[1], q.shape[2]
    nb = S_ // BLOCK_Q
    qb = jnp.moveaxis(q.reshape(B_, nb, BLOCK_Q, H, 2, DIFF_DIM), 1, 0)
    key_pos = jnp.arange(S_)
    slopes = alibi_slopes(H)
    scale = 1.0 / math.sqrt(DIFF_DIM)

    def one_block(args):
        q_blk, n = args
        s = jnp.einsum('bqhmd,bkhmd->bhmqk', q_blk, k).astype(jnp.float32) * scale
        q_pos = n * BLOCK_Q + jnp.arange(BLOCK_Q)
        dist = (q_pos[:, None] - key_pos[None, :]).astype(jnp.float32)
        s = s - slopes[:, None, None, None] * dist
        s = jnp.where(dist >= 0, s, -jnp.inf)
        p = jax.nn.softmax(s, axis=-1)
        a = p[:, :, 0] - lam * p[:, :, 1]
        return jnp.einsum('bhqk,bkhd->bqhd', a.astype(v.dtype), v)

    o = jnp.moveaxis(lax.map(one_block, (qb, jnp.arange(nb))), 0, 1).reshape(B_, S_, H, 2 * DIFF_DIM)
    o = rms_norm(o, subln_g) * (1.0 - lambda_init)
    return o.reshape(B_, S_, H * 2 * DIFF_DIM)


def mixer_sublayer(h, w_in, w_gate, b_gate, sinks, lq1, lk1, lq2, lk2, subln_g, w_branch, w_out,
                   lambda_init):
    B_, S_, _ = h.shape
    proj = h @ w_in
    offs = []
    acc = 0
    for n in IN_SPLITS[:-1]:
        acc += n
        offs.append(acc)
    qa, ka, va, qb, kb, vb, qc, kc, vc = jnp.split(proj, offs, axis=-1)
    o_a = swa_sink_attention(qa.reshape(B_, S_, SWA_Q_HEADS, HEAD_DIM),
                             ka.reshape(B_, S_, SWA_KV_HEADS, HEAD_DIM),
                             va.reshape(B_, S_, SWA_KV_HEADS, HEAD_DIM), sinks)
    o_b = stick_breaking_attention(qb.reshape(B_, S_, SB_HEADS, HEAD_DIM),
                                   kb.reshape(B_, S_, SB_HEADS, HEAD_DIM),
                                   vb.reshape(B_, S_, SB_HEADS, HEAD_DIM))
    f32 = jnp.float32
    lam = (jnp.exp(jnp.sum(lq1.astype(f32) * lk1.astype(f32)))
           - jnp.exp(jnp.sum(lq2.astype(f32) * lk2.astype(f32))) + lambda_init)
    o_c = diff_attention(qc.reshape(B_, S_, DIFF_HEADS, 2, DIFF_DIM),
                         kc.reshape(B_, S_, DIFF_HEADS, 2, DIFF_DIM),
                         vc.reshape(B_, S_, DIFF_HEADS, 2 * DIFF_DIM), lam, lambda_init, subln_g)
    branches = jnp.stack([o_a, o_b, o_c], axis=2)
    y = jnp.einsum('bsnf,nfd->bsnd', branches, w_branch)
    gates = jax.nn.sigmoid(h @ w_gate + b_gate).reshape(B_, S_, N_BRANCHES, D_MODEL)
    return jnp.einsum('bsnd,de->bse', gates * y, w_out)


def hierarchical_moe(h, w_rg, b_rg, w_re, b_re, w_g, w_u, w_d):
    B_, S_, D = h.shape
    T = B_ * S_
    xf = h.reshape(T, D)
    group_logits = (xf @ w_rg + b_rg).astype(jnp.float32)
    group_prob = jax.nn.softmax(group_logits, axis=-1)
    g_top = jnp.argmax(group_logits, axis=-1)
    p_group = jnp.take_along_axis(group_prob, g_top[:, None], axis=-1)
    exp_logits = (xf @ w_re + b_re).astype(jnp.float32).reshape(T, N_GROUPS, EXPERTS_PER_GROUP)
    in_group = jnp.take_along_axis(exp_logits, g_top[:, None, None], axis=1)[:, 0]
    top_val, top_idx = lax.top_k(in_group, TOP_K_EXPERT)
    w_top = jax.nn.softmax(top_val, axis=-1) * p_group
    expert_id = g_top[:, None].astype(jnp.int32) * EXPERTS_PER_GROUP + top_idx.astype(jnp.int32)

    A = T * TOP_K_EXPERT
    DB = DISPATCH_BLOCK
    flat_e = expert_id.reshape(A)
    flat_tok = jnp.repeat(jnp.arange(T, dtype=jnp.int32), TOP_K_EXPERT)
    flat_w = w_top.reshape(A)
    counts = jnp.zeros((N_EXPERTS,), jnp.int32).at[flat_e].add(1)
    padded = (counts + DB - 1) // DB * DB
    start = jnp.cumsum(counts) - counts
    pend = jnp.cumsum(padded)
    pstart = pend - padded
    order = jnp.argsort(flat_e, stable=True)
    se = flat_e[order]
    dest = pstart[se] + (jnp.arange(A, dtype=jnp.int32) - start[se])
    P = ((A + DB - 1) // DB) * DB + N_EXPERTS * DB
    nblk = P // DB
    buf_tok = jnp.full((P,), T, jnp.int32).at[dest].set(flat_tok[order])
    buf_w = jnp.zeros((P,), jnp.float32).at[dest].set(flat_w[order])
    block_e = jnp.minimum(jnp.searchsorted(pend, jnp.arange(nblk, dtype=jnp.int32) * DB, side='right'),
                          N_EXPERTS - 1)
    x_pad = jnp.concatenate([xf, jnp.zeros((1, D), xf.dtype)], axis=0)
    xb = x_pad[buf_tok].reshape(nblk, DB, D)

    def expert_block(args):
        xblk, e = args
        return (jax.nn.silu(xblk @ w_g[e]) * (xblk @ w_u[e])) @ w_d[e]

    yb = lax.map(expert_block, (xb, block_e)).reshape(P, D)
    y = jnp.zeros((T + 1, D), yb.dtype).at[buf_tok].add(yb * buf_w[:, None].astype(yb.dtype))
    return y[:T].reshape(B_, S_, D)


def setup_inputs(seed: int = 0) -> dict:
    key = jax.random.key(seed)
    ks = jax.random.split(key, 26)
    f32 = jnp.float32

    def nrm(k, shape, scale):
        return jax.random.normal(k, shape, f32) * scale

    D, L = D_MODEL, DEPTH
    return {
        "x": nrm(ks[0], (BATCH, SEQ, D), 1.0),
        "c": nrm(ks[1], (BATCH, D), 1.0),
        "w_ada": nrm(ks[2], (L, D, 6 * D), 0.5 * D ** -0.5),
        "b_ada": nrm(ks[3], (L, 6 * D), 0.01),
        "w_in": nrm(ks[4], (L, D, D_IN), D ** -0.5),
        "w_branch_gate": nrm(ks[5], (L, D, N_BRANCHES * D), D ** -0.5),
        "b_branch_gate": nrm(ks[6], (L, N_BRANCHES * D), 0.01),
        "attn_sinks": nrm(ks[7], (L, SWA_Q_HEADS), 0.5),
        "lambda_q1": nrm(ks[8], (L, DIFF_DIM), 0.1),
        "lambda_k1": nrm(ks[9], (L, DIFF_DIM), 0.1),
        "lambda_q2": nrm(ks[10], (L, DIFF_DIM), 0.1),
        "lambda_k2": nrm(ks[11], (L, DIFF_DIM), 0.1),
        "subln_g": 1.0 + nrm(ks[12], (L, 2 * DIFF_DIM), 0.02),
        "w_branch": nrm(ks[13], (L, N_BRANCHES, BRANCH_WIDTH, D), BRANCH_WIDTH ** -0.5 * BETA),
        "w_out": nrm(ks[14], (L, D, D), D ** -0.5 * BETA),
        "ln1_g": 1.0 + nrm(ks[15], (L, D), 0.02),
        "ln1_b": nrm(ks[16], (L, D), 0.02),
        "w_router_group": nrm(ks[17], (L, D, N_GROUPS), D ** -0.5),
        "b_router_group": nrm(ks[18], (L, N_GROUPS), 0.01),
        "w_router_expert": nrm(ks[19], (L, D, N_EXPERTS), D ** -0.5),
        "b_router_expert": nrm(ks[20], (L, N_EXPERTS), 0.01),
        "w_exp_gate": nrm(ks[21], (L, N_EXPERTS, D, D_EXPERT), D ** -0.5),
        "w_exp_up": nrm(ks[22], (L, N_EXPERTS, D, D_EXPERT), D ** -0.5),
        "w_exp_down": nrm(ks[23], (L, N_EXPERTS, D_EXPERT, D), D_EXPERT ** -0.5 * BETA),
        "ln2_g": 1.0 + nrm(ks[24], (L, D), 0.02),
        "ln2_b": nrm(ks[25], (L, D), 0.02),
    }


def reference(x, c, w_ada, b_ada, w_in, w_branch_gate, b_branch_gate, attn_sinks,
              lambda_q1, lambda_k1, lambda_q2, lambda_k2, subln_g, w_branch, w_out,
              ln1_g, ln1_b, w_router_group, b_router_group, w_router_expert, b_router_expert,
              w_exp_gate, w_exp_up, w_exp_down, ln2_g, ln2_b):
    for l in range(DEPTH):
        lambda_init = 0.8 - 0.6 * math.exp(-0.3 * l)
        mod = (c @ w_ada[l] + b_ada[l])[:, None, :]
        sh1, sc1, g1, sh2, sc2, g2 = jnp.split(mod, 6, axis=-1)
        h = x * (1.0 + sc1) + sh1
        y = mixer_sublayer(h, w_in[l], w_branch_gate[l], b_branch_gate[l], attn_sinks[l],
                           lambda_q1[l], lambda_k1[l], lambda_q2[l], lambda_k2[l], subln_g[l],
                           w_branch[l], w_out[l], lambda_init)
        x = layer_norm(ALPHA * x + g1 * y, ln1_g[l], ln1_b[l])
        h = x * (1.0 + sc2) + sh2
        y = hierarchical_moe(h, w_router_group[l], b_router_group[l], w_router_expert[l],
                             b_router_expert[l], w_exp_gate[l], w_exp_up[l], w_exp_down[l])
        x = layer_norm(ALPHA * x + g2 * y, ln2_g[l], ln2_b[l])
    return x
```

```python
import functools
import math

import jax
import jax.numpy as jnp
from jax import lax
from jax.experimental import pallas as pl
from jax.experimental.pallas import tpu as pltpu

F32 = jnp.float32
BF16 = jnp.bfloat16

D_MODEL = 2048
HEAD_DIM = 64
SWA_Q_HEADS = 16
SWA_KV_HEADS = 4
WINDOW = 128
SB_HEADS = 16
DIFF_HEADS = 8
BLOCK_Q = 128
BRANCH_WIDTH = 1024
A_KV = 256
N_BRANCHES = 3
N_GROUPS = 8
EXPERTS_PER_GROUP = 8
N_EXPERTS = 64
D_EXPERT = 384
DISPATCH_BLOCK = 128
DEPTH = 4
ALPHA = (2.0 * DEPTH) ** 0.25
LN_EPS = 1e-5
RMS_EPS = 1e-5
QK_SCALE = 1.0 / math.sqrt(HEAD_DIM)

COL_QA, COL_KA, COL_VA = 0, 1024, 1280
COL_QB, COL_KB, COL_VB = 1536, 2560, 3584
COL_QC, COL_KC, COL_VC = 4608, 5632, 6656

LANES = 128
VMEM_LIMIT = 56 * 1024 * 1024
NEG = -1e30
SB_EXIT = 88.0


def _params(sem):
    return pltpu.CompilerParams(dimension_semantics=sem, vmem_limit_bytes=VMEM_LIMIT)


def _ada_body(c_ref, w_ref, b_ref, o_ref):
    o_ref[0] = jnp.sum(c_ref[...] * w_ref[0], axis=0, keepdims=True) + b_ref[0]


def _ada_mod(c, w_ada, b_ada):
    L, D, N = w_ada.shape
    tn = 512
    return pl.pallas_call(
        _ada_body,
        grid=(L, N // tn),
        in_specs=[pl.BlockSpec((D, 1), lambda l, j: (0, 0)),
                  pl.BlockSpec((1, D, tn), lambda l, j: (l, 0, j)),
                  pl.BlockSpec((1, 1, tn), lambda l, j: (l, 0, j))],
        out_specs=pl.BlockSpec((1, 1, tn), lambda l, j: (l, 0, j)),
        out_shape=jax.ShapeDtypeStruct((L, 1, N), F32),
        compiler_params=_params(("arbitrary", "arbitrary")),
        name="ada_mod",
    )(c.reshape(D, 1), w_ada, b_ada.reshape(L, 1, N))


def _modmm_body(x_ref, sc_ref, sh_ref, w_ref, b_ref, o_ref, h_ref, *, sigmoid):
    @pl.when(pl.program_id(1) == 0)
    def _():
        h_ref[...] = (x_ref[...] * (1.0 + sc_ref[...]) + sh_ref[...]).astype(BF16)

    acc = jnp.dot(h_ref[...], w_ref[...], preferred_element_type=F32) + b_ref[...]
    if sigmoid:
        acc = 1.0 / (1.0 + jnp.exp(-acc))
    o_ref[...] = acc.astype(o_ref.dtype)


def _mod_matmul(x, sc, sh, w, b, *, sigmoid, name):
    M, K = x.shape
    N = w.shape[1]
    tm = min(1024, M)
    tn = 512
    return pl.pallas_call(
        functools.partial(_modmm_body, sigmoid=sigmoid),
        grid=(M // tm, N // tn),
        in_specs=[pl.BlockSpec((tm, K), lambda i, j: (i, 0)),
                  pl.BlockSpec((1, K), lambda i, j: (0, 0)),
                  pl.BlockSpec((1, K), lambda i, j: (0, 0)),
                  pl.BlockSpec((K, tn), lambda i, j: (0, j)),
                  pl.BlockSpec((1, tn), lambda i, j: (0, j))],
        out_specs=pl.BlockSpec((tm, tn), lambda i, j: (i, j)),
        out_shape=jax.ShapeDtypeStruct((M, N), BF16),
        scratch_shapes=[pltpu.VMEM((tm, K), BF16)],
        compiler_params=_params(("arbitrary", "arbitrary")),
        name=name,
    )(x, sc, sh, w, b)


def _swa_body(sinks_ref, q_ref, kp_ref, kc_ref, vp_ref, vc_ref, o_ref):
    i = pl.program_id(0)
    k = jnp.concatenate([kp_ref[...], kc_ref[...]], axis=0)
    v = jnp.concatenate([vp_ref[...], vc_ref[...]], axis=0)
    qi = lax.broadcasted_iota(jnp.int32, (BLOCK_Q, 2 * BLOCK_Q), 0)
    ki = lax.broadcasted_iota(jnp.int32, (BLOCK_Q, 2 * BLOCK_Q), 1)
    dist = qi + BLOCK_Q - ki
    valid = (dist >= 0) & (dist < WINDOW) & ((ki >= BLOCK_Q) | (i > 0))
    distf = dist.astype(F32)
    group = SWA_Q_HEADS // SWA_KV_HEADS
    outs = []
    for h in range(SWA_Q_HEADS):
        g = h // group
        slope = 2.0 ** (-8.0 * (h + 1) / SWA_Q_HEADS)
        qh = q_ref[:, h * HEAD_DIM:(h + 1) * HEAD_DIM]
        kg = k[:, g * HEAD_DIM:(g + 1) * HEAD_DIM]
        vg = v[:, g * HEAD_DIM:(g + 1) * HEAD_DIM]
        s = lax.dot_general(qh, kg, (((1,), (1,)), ((), ())), preferred_element_type=F32) * QK_SCALE
        s = jnp.where(valid, s - slope * distf, NEG)
        sink = sinks_ref[h]
        m = jnp.maximum(jnp.max(s, axis=1, keepdims=True), sink)
        p = jnp.where(valid, jnp.exp(s - m), 0.0)
        denom = jnp.sum(p, axis=1, keepdims=True) + jnp.exp(sink - m)
        o = jnp.dot(p.astype(BF16), vg, preferred_element_type=F32) / denom
        outs.append(o)
    o_ref[...] = jnp.concatenate(outs, axis=1).astype(o_ref.dtype)


def _swa_attention(proj, sinks):
    S = proj.shape[0]
    nb = S // BLOCK_Q
    kv_blk = lambda col: pl.BlockSpec((BLOCK_Q, A_KV), lambda i: (i, col // A_KV))
    kv_prev = lambda col: pl.BlockSpec((BLOCK_Q, A_KV), lambda i: (jnp.maximum(i - 1, 0), col // A_KV))
    return pl.pallas_call(
        _swa_body,
        grid=(nb,),
        in_specs=[pl.BlockSpec(memory_space=pltpu.SMEM),
                  pl.BlockSpec((BLOCK_Q, BRANCH_WIDTH), lambda i: (i, COL_QA // BRANCH_WIDTH)),
                  kv_prev(COL_KA), kv_blk(COL_KA), kv_prev(COL_VA), kv_blk(COL_VA)],
        out_specs=pl.BlockSpec((BLOCK_Q, BRANCH_WIDTH), lambda i: (i, 0)),
        out_shape=jax.ShapeDtypeStruct((S, BRANCH_WIDTH), BF16),
        compiler_params=_params(("arbitrary",)),
        name="swa_attention",
    )(sinks, proj, proj, proj, proj, proj)


def _sb_body(q_ref, k_ref, v_ref, o_ref, acc_ref, carry_ref, *, tq):
    i = pl.program_id(1)
    lane = lax.broadcasted_iota(jnp.int32, (1, LANES), 1)
    rows = lax.broadcasted_iota(jnp.int32, (tq, tq), 0)
    cols = lax.broadcasted_iota(jnp.int32, (tq, tq), 1)
    later = (rows > cols).astype(BF16)
    q2 = q_ref[...]
    outs = []
    for half in range(2):
        in_half = (lane >= HEAD_DIM) if half else (lane < HEAD_DIM)
        qm = jnp.where(in_half, q2, jnp.zeros_like(q2))
        acc_ref[...] = jnp.zeros_like(acc_ref)
        carry_ref[...] = jnp.zeros_like(carry_ref)

        def cond(state):
            j, top = state
            return (j >= 0) & (top > -SB_EXIT)

        def body(state):
            j, _ = state
            off = pl.multiple_of(j * tq, tq)
            kb = k_ref[pl.ds(off, tq), :]
            vb = v_ref[pl.ds(off, tq), :]
            z = lax.dot_general(qm, kb, (((1,), (1,)), ((), ())), preferred_element_type=F32) * QK_SCALE
            soft = jnp.log1p(jnp.exp(-jnp.abs(z)))
            causal = (cols + j * tq) < (rows + i * tq)
            log_rem = jnp.where(causal, -(jnp.maximum(z, 0.0) + soft), 0.0)
            hi = log_rem.astype(BF16)
            lo = (log_rem - hi.astype(F32)).astype(BF16)
            after = (jnp.dot(hi, later, preferred_element_type=F32)
                     + jnp.dot(lo, later, preferred_element_type=F32))
            carry = carry_ref[...]
            log_w = (jnp.minimum(z, 0.0) - soft) + (after + carry)
            w = jnp.where(causal, jnp.exp(log_w), 0.0)
            acc_ref[...] += jnp.dot(w.astype(BF16), vb, preferred_element_type=F32)
            new_carry = carry + jnp.sum(log_rem, axis=1, keepdims=True)
            carry_ref[...] = new_carry
            return j - 1, jnp.max(new_carry)

        lax.while_loop(cond, body, (i, jnp.float32(0.0)))
        outs.append(acc_ref[...])
    o_ref[...] = jnp.where(lane >= HEAD_DIM, outs[1], outs[0]).astype(o_ref.dtype)


def _sb_attention(proj):
    S = proj.shape[0]
    tq = min(256, S)
    npair = BRANCH_WIDTH // LANES
    return pl.pallas_call(
        functools.partial(_sb_body, tq=tq),
        grid=(npair, S // tq),
        in_specs=[pl.BlockSpec((tq, LANES), lambda p, i: (i, COL_QB // LANES + p)),
                  pl.BlockSpec((S, LANES), lambda p, i: (0, COL_KB // LANES + p)),
                  pl.BlockSpec((S, LANES), lambda p, i: (0, COL_VB // LANES + p))],
        out_specs=pl.BlockSpec((tq, LANES), lambda p, i: (i, p)),
        out_shape=jax.ShapeDtypeStruct((S, BRANCH_WIDTH), BF16),
        scratch_shapes=[pltpu.VMEM((tq, LANES), F32), pltpu.VMEM((tq, 1), F32)],
        compiler_params=_params(("arbitrary", "arbitrary")),
        name="stickbreak_attention",
    )(proj, proj, proj)


def _diff_body(slopes_ref, q_ref, k_ref, v_ref, lq1_ref, lk1_ref, lq2_ref, lk2_ref, g_ref, o_ref,
               m_ref, l_ref, acc_ref, *, tq, lambda_init):
    h = pl.program_id(0)
    i = pl.program_id(1)
    slope = slopes_ref[h]
    lane = lax.broadcasted_iota(jnp.int32, (1, LANES), 1)
    rows = lax.broadcasted_iota(jnp.int32, (tq, tq), 0)
    cols = lax.broadcasted_iota(jnp.int32, (tq, tq), 1)
    q12 = q_ref[...]
    res = []
    for half in range(2):
        in_half = (lane >= HEAD_DIM) if half else (lane < HEAD_DIM)
        qm = jnp.where(in_half, q12, jnp.zeros_like(q12))
        m_ref[...] = jnp.full_like(m_ref, NEG)
        l_ref[...] = jnp.zeros_like(l_ref)
        acc_ref[...] = jnp.zeros_like(acc_ref)

        def body(j, carry):
            off = pl.multiple_of(j * tq, tq)
            kb = k_ref[pl.ds(off, tq), :]
            vb = v_ref[pl.ds(off, tq), :]
            s = lax.dot_general(qm, kb, (((1,), (1,)), ((), ())), preferred_element_type=F32) * QK_SCALE
            dist = (rows + i * tq) - (cols + j * tq)
            s = jnp.where(dist >= 0, s - slope * dist.astype(F32), NEG)
            m_old = m_ref[...]
            m_new = jnp.maximum(m_old, jnp.max(s, axis=1, keepdims=True))
            alpha = jnp.exp(m_old - m_new)
            p = jnp.exp(s - m_new)
            l_ref[...] = alpha * l_ref[...] + jnp.sum(p, axis=1, keepdims=True)
            acc_ref[...] = alpha * acc_ref[...] + jnp.dot(p.astype(BF16), vb, preferred_element_type=F32)
            m_ref[...] = m_new
            return carry

        lax.fori_loop(0, i + 1, body, 0)
        res.append(acc_ref[...] / l_ref[...])
    lam = (jnp.exp(jnp.sum(lq1_ref[...] * lk1_ref[...], axis=1, keepdims=True))
           - jnp.exp(jnp.sum(lq2_ref[...] * lk2_ref[...], axis=1, keepdims=True)) + lambda_init)
    o = res[0] - lam * res[1]
    y = o * lax.rsqrt(jnp.mean(o * o, axis=1, keepdims=True) + RMS_EPS) * g_ref[...]
    o_ref[...] = (y * (1.0 - lambda_init)).astype(o_ref.dtype)


def _diff_attention(proj, slopes, lq1, lk1, lq2, lk2, subln_g, lambda_init):
    S = proj.shape[0]
    tq = min(256, S)
    vec = lambda n: pl.BlockSpec((1, n), lambda h, i: (0, 0))
    return pl.pallas_call(
        functools.partial(_diff_body, tq=tq, lambda_init=lambda_init),
        grid=(DIFF_HEADS, S // tq),
        in_specs=[pl.BlockSpec(memory_space=pltpu.SMEM),
                  pl.BlockSpec((tq, LANES), lambda h, i: (i, COL_QC // LANES + h)),
                  pl.BlockSpec((S, LANES), lambda h, i: (0, COL_KC // LANES + h)),
                  pl.BlockSpec((S, LANES), lambda h, i: (0, COL_VC // LANES + h)),
                  vec(HEAD_DIM), vec(HEAD_DIM), vec(HEAD_DIM), vec(HEAD_DIM), vec(2 * HEAD_DIM)],
        out_specs=pl.BlockSpec((tq, LANES), lambda h, i: (i, h)),
        out_shape=jax.ShapeDtypeStruct((S, BRANCH_WIDTH), BF16),
        scratch_shapes=[pltpu.VMEM((tq, 1), F32), pltpu.VMEM((tq, 1), F32), pltpu.VMEM((tq, LANES), F32)],
        compiler_params=_params(("arbitrary", "arbitrary")),
        name="diff_attention",
    )(slopes, proj, proj, proj, lq1.reshape(1, -1), lk1.reshape(1, -1), lq2.reshape(1, -1),
      lk2.reshape(1, -1), subln_g.reshape(1, -1))


def _branch_body(oa_ref, ob_ref, oc_ref, w_ref, ga_ref, gb_ref, gc_ref, z_ref):
    z = None
    for n, (o_ref, g_ref) in enumerate(((oa_ref, ga_ref), (ob_ref, gb_ref), (oc_ref, gc_ref))):
        y = jnp.dot(o_ref[...], w_ref[n], preferred_element_type=F32)
        t = g_ref[...].astype(F32) * y
        z = t if z is None else z + t
    z_ref[...] = z.astype(z_ref.dtype)


def _branch_merge(o_a, o_b, o_c, w_branch, gates):
    S = o_a.shape[0]
    D = w_branch.shape[2]
    tm = min(1024, S)
    tn = 512
    nj = D // tn
    o_spec = pl.BlockSpec((tm, BRANCH_WIDTH), lambda i, j: (i, 0))
    gate_spec = lambda n: pl.BlockSpec((tm, tn), lambda i, j: (i, n * nj + j))
    return pl.pallas_call(
        _branch_body,
        grid=(S // tm, nj),
        in_specs=[o_spec, o_spec, o_spec,
                  pl.BlockSpec((N_BRANCHES, BRANCH_WIDTH, tn), lambda i, j: (0, 0, j)),
                  gate_spec(0), gate_spec(1), gate_spec(2)],
        out_specs=pl.BlockSpec((tm, tn), lambda i, j: (i, j)),
        out_shape=jax.ShapeDtypeStruct((S, D), BF16),
        compiler_params=_params(("arbitrary", "arbitrary")),
        name="branch_merge",
    )(o_a, o_b, o_c, w_branch, gates, gates, gates)


def _layer_norm(r, g, b):
    mu = jnp.mean(r, axis=1, keepdims=True)
    d = r - mu
    var = jnp.mean(d * d, axis=1, keepdims=True)
    return d * lax.rsqrt(var + LN_EPS) * g + b


def _out_body(z_ref, w_ref, x_ref, g1_ref, lng_ref, lnb_ref, sc2_ref, sh2_ref, wr_ref, br_ref,
              x1_ref, logit_ref, *, alpha):
    y = jnp.dot(z_ref[...], w_ref[...], preferred_element_type=F32)
    x1 = _layer_norm(alpha * x_ref[...] + g1_ref[...] * y, lng_ref[...], lnb_ref[...])
    x1_ref[...] = x1
    h2 = x1 * (1.0 + sc2_ref[...]) + sh2_ref[...]
    logit_ref[...] = jnp.dot(h2, wr_ref[...], preferred_element_type=F32,
                             precision=lax.Precision.HIGHEST) + br_ref[...]


def _mixer_out(z, w_out, x, g1, ln_g, ln_b, sc2, sh2, w_router, b_router, alpha):
    S, D = x.shape
    tm = min(256, S)
    row = pl.BlockSpec((tm, D), lambda i: (i, 0))
    vec = pl.BlockSpec((1, D), lambda i: (0, 0))
    return pl.pallas_call(
        functools.partial(_out_body, alpha=alpha),
        grid=(S // tm,),
        in_specs=[row, pl.BlockSpec((D, D), lambda i: (0, 0)), row, vec, vec, vec, vec, vec,
                  pl.BlockSpec((D, LANES), lambda i: (0, 0)), pl.BlockSpec((1, LANES), lambda i: (0, 0))],
        out_specs=[row, pl.BlockSpec((tm, LANES), lambda i: (i, 0))],
        out_shape=[jax.ShapeDtypeStruct((S, D), F32), jax.ShapeDtypeStruct((S, LANES), F32)],
        compiler_params=_params(("arbitrary",)),
        name="mixer_out_ln",
    )(z, w_out, x, g1, ln_g, ln_b, sc2, sh2, w_router, b_router)


def _dispatch_body(dest_ref, x_hbm, xb_in_hbm, xb_hbm, sem, *, rows_per_step):
    del xb_in_hbm
    i = pl.program_id(0)

    def row_copy(t, d):
        return pltpu.make_async_copy(x_hbm.at[pl.ds(t, 1)], xb_hbm.at[pl.ds(d, 1)], sem)

    def issue(r, carry):
        t = i * rows_per_step + r
        for k in range(2):
            row_copy(t, dest_ref[2 * t + k]).start()
        return carry

    def wait(r, carry):
        for k in range(2):
            row_copy(0, 0).wait()
        return carry

    lax.fori_loop(0, rows_per_step, issue, 0)
    lax.fori_loop(0, rows_per_step, wait, 0)


def _dispatch(x1, dest, padded_rows):
    T, D = x1.shape
    rows_per_step = min(512, T)
    xb0 = jnp.zeros((padded_rows, D), F32)
    return pl.pallas_call(
        functools.partial(_dispatch_body, rows_per_step=rows_per_step),
        grid_spec=pltpu.PrefetchScalarGridSpec(
            num_scalar_prefetch=1,
            grid=(T // rows_per_step,),
            in_specs=[pl.BlockSpec(memory_space=pl.ANY), pl.BlockSpec(memory_space=pl.ANY)],
            out_specs=pl.BlockSpec(memory_space=pl.ANY),
            scratch_shapes=[pltpu.SemaphoreType.DMA(())]),
        out_shape=jax.ShapeDtypeStruct((padded_rows, D), F32),
        input_output_aliases={2: 0},
        compiler_params=_params(("arbitrary",)),
        name="moe_dispatch",
    )(dest, x1, xb0)


def _expert_body(be_ref, nu_ref, xb_ref, sc_ref, sh_ref, wg_ref, wu_ref, wd_ref, yb_ref,
                 wg_s, wu_s, wd_s):
    b = pl.program_id(0)
    nu = nu_ref[0]

    @pl.when(b < nu)
    def _():
        e = be_ref[b]
        prev = be_ref[jnp.maximum(b - 1, 0)]

        @pl.when((b == 0) | (e != prev))
        def _():
            wg_s[...] = wg_ref[0].astype(BF16)
            wu_s[...] = wu_ref[0].astype(BF16)
            wd_s[...] = wd_ref[0].astype(BF16)

        h = (xb_ref[...] * (1.0 + sc_ref[...]) + sh_ref[...]).astype(BF16)
        a = jnp.dot(h, wg_s[...], preferred_element_type=F32)
        u = jnp.dot(h, wu_s[...], preferred_element_type=F32)
        act = (a / (1.0 + jnp.exp(-a))) * u
        yb_ref[...] = jnp.dot(act.astype(BF16), wd_s[...], preferred_element_type=F32)

    @pl.when(b >= nu)
    def _():
        yb_ref[...] = jnp.zeros_like(yb_ref)


def _experts(xb, sc2, sh2, w_g, w_u, w_d, block_e, n_used, layer):
    P, D = xb.shape
    DB = DISPATCH_BLOCK
    nblk = P // DB
    blk = lambda b, be, nu: jnp.minimum(b, nu[0] - 1)
    wsel = lambda b, be, nu: (layer * N_EXPERTS + be[blk(b, be, nu)], 0, 0)
    return pl.pallas_call(
        _expert_body,
        grid_spec=pltpu.PrefetchScalarGridSpec(
            num_scalar_prefetch=2,
            grid=(nblk,),
            in_specs=[pl.BlockSpec((DB, D), lambda b, be, nu: (blk(b, be, nu), 0)),
                      pl.BlockSpec((1, D), lambda b, be, nu: (0, 0)),
                      pl.BlockSpec((1, D), lambda b, be, nu: (0, 0)),
                      pl.BlockSpec((1, D, D_EXPERT), wsel),
                      pl.BlockSpec((1, D, D_EXPERT), wsel),
                      pl.BlockSpec((1, D_EXPERT, D), wsel)],
            out_specs=pl.BlockSpec((DB, D), lambda b, be, nu: (b, 0)),
            scratch_shapes=[pltpu.VMEM((D, D_EXPERT), BF16), pltpu.VMEM((D, D_EXPERT), BF16),
                            pltpu.VMEM((D_EXPERT, D), BF16)]),
        out_shape=jax.ShapeDtypeStruct((P, D), F32),
        compiler_params=_params(("arbitrary",)),
        name="moe_experts",
    )(block_e, n_used, xb, sc2, sh2, w_g, w_u, w_d)


def _combine_body(dest_ref, x1_ref, wt_ref, g2_ref, lng_ref, lnb_ref, yb_hbm, o_ref,
                  buf0, buf1, sem, *, tm, alpha):
    i = pl.program_id(0)
    bufs = (buf0, buf1)

    def row_copy(d, k, r):
        return pltpu.make_async_copy(yb_hbm.at[pl.ds(d, 1)], bufs[k].at[pl.ds(r, 1)], sem)

    def issue(r, carry):
        t = i * tm + r
        for k in range(2):
            row_copy(dest_ref[2 * t + k], k, r).start()
        return carry

    def wait(r, carry):
        for k in range(2):
            row_copy(0, k, 0).wait()
        return carry

    lax.fori_loop(0, tm, issue, 0)
    lax.fori_loop(0, tm, wait, 0)
    wt = wt_ref[...]
    y = buf0[...] * wt[:, 0:1] + buf1[...] * wt[:, 1:2]
    o_ref[...] = _layer_norm(alpha * x1_ref[...] + g2_ref[...] * y, lng_ref[...], lnb_ref[...])


def _combine_ln(x1, yb, dest, w_top, g2, ln_g, ln_b, alpha):
    T, D = x1.shape
    tm = min(256, T)
    row = pl.BlockSpec((tm, D), lambda i, d: (i, 0))
    vec = pl.BlockSpec((1, D), lambda i, d: (0, 0))
    return pl.pallas_call(
        functools.partial(_combine_body, tm=tm, alpha=alpha),
        grid_spec=pltpu.PrefetchScalarGridSpec(
            num_scalar_prefetch=1,
            grid=(T // tm,),
            in_specs=[row, pl.BlockSpec((tm, 2), lambda i, d: (i, 0)), vec, vec, vec,
                      pl.BlockSpec(memory_space=pl.ANY)],
            out_specs=row,
            scratch_shapes=[pltpu.VMEM((tm, D), F32), pltpu.VMEM((tm, D), F32),
                            pltpu.SemaphoreType.DMA(())]),
        out_shape=jax.ShapeDtypeStruct((T, D), F32),
        compiler_params=_params(("arbitrary",)),
        name="moe_combine_ln",
    )(dest, x1, w_top, g2, ln_g, ln_b, yb)


def _route(logits):
    T = logits.shape[0]
    DB = DISPATCH_BLOCK
    group_logits = logits[:, :N_GROUPS]
    exp_logits = logits[:, N_GROUPS:N_GROUPS + N_EXPERTS].reshape(T, N_GROUPS, EXPERTS_PER_GROUP)
    group_prob = jax.nn.softmax(group_logits, axis=-1)
    g_top = jnp.argmax(group_logits, axis=-1)
    p_group = jnp.take_along_axis(group_prob, g_top[:, None], axis=-1)
    in_group = jnp.take_along_axis(exp_logits, g_top[:, None, None], axis=1)[:, 0]
    top_val, top_idx = lax.top_k(in_group, 2)
    w_top = jax.nn.softmax(top_val, axis=-1) * p_group
    expert_id = g_top[:, None].astype(jnp.int32) * EXPERTS_PER_GROUP + top_idx.astype(jnp.int32)

    member = (expert_id[:, :, None] == jnp.arange(N_EXPERTS, dtype=jnp.int32)).any(axis=1).astype(jnp.int32)
    incl = jnp.cumsum(member, axis=0)
    counts = incl[-1]
    rank = jnp.take_along_axis(incl - member, expert_id, axis=1)
    padded = (counts + DB - 1) // DB * DB
    pend = jnp.cumsum(padded)
    pstart = pend - padded
    dest = (pstart[expert_id] + rank).astype(jnp.int32)
    P = ((2 * T + DB - 1) // DB) * DB + N_EXPERTS * DB
    nblk = P // DB
    block_e = jnp.minimum(jnp.searchsorted(pend, jnp.arange(nblk, dtype=jnp.int32) * DB, side='right'),
                          N_EXPERTS - 1).astype(jnp.int32)
    n_used = (pend[-1:] // DB).astype(jnp.int32)
    return dest.reshape(-1), w_top, block_e, n_used, P


def kernel(x, c, w_ada, b_ada, w_in, w_branch_gate, b_branch_gate, attn_sinks, lambda_q1, lambda_k1,
           lambda_q2, lambda_k2, subln_g, w_branch, w_out, ln1_g, ln1_b, w_router_group, b_router_group,
           w_router_expert, b_router_expert, w_exp_gate, w_exp_up, w_exp_down, ln2_g, ln2_b):
    B, S, D = x.shape
    assert B == 1 and D == D_MODEL
    depth = w_in.shape[0]
    alpha = ALPHA
    xs = x.reshape(S, D)
    mod = _ada_mod(c, w_ada, b_ada)
    diff_slopes = jnp.exp2(-8.0 * jnp.arange(1, DIFF_HEADS + 1, dtype=F32) / DIFF_HEADS)
    zero_bias = jnp.zeros((1, w_in.shape[2]), F32)
    pad = LANES - N_GROUPS - N_EXPERTS
    w_eg = w_exp_gate.reshape(depth * N_EXPERTS, D, D_EXPERT)
    w_eu = w_exp_up.reshape(depth * N_EXPERTS, D, D_EXPERT)
    w_ed = w_exp_down.reshape(depth * N_EXPERTS, D_EXPERT, D)
    for l in range(depth):
        lambda_init = 0.8 - 0.6 * math.exp(-0.3 * l)
        sh1, sc1, g1, sh2, sc2, g2 = [mod[l, :, n * D:(n + 1) * D] for n in range(6)]
        proj = _mod_matmul(xs, sc1, sh1, w_in[l].astype(BF16), zero_bias, sigmoid=False, name="in_proj")
        gates = _mod_matmul(xs, sc1, sh1, w_branch_gate[l].astype(BF16), b_branch_gate[l].reshape(1, -1),
                            sigmoid=True, name="branch_gates")
        o_a = _swa_attention(proj, attn_sinks[l])
        o_b = _sb_attention(proj)
        o_c = _diff_attention(proj, diff_slopes, lambda_q1[l], lambda_k1[l], lambda_q2[l], lambda_k2[l],
                              subln_g[l], lambda_init)
        z = _branch_merge(o_a, o_b, o_c, w_branch[l].astype(BF16), gates)
        w_router = jnp.pad(jnp.concatenate([w_router_group[l], w_router_expert[l]], axis=1), ((0, 0), (0, pad)))
        b_router = jnp.pad(jnp.concatenate([b_router_group[l], b_router_expert[l]]), (0, pad)).reshape(1, LANES)
        x1, logits = _mixer_out(z, w_out[l].astype(BF16), xs, g1, ln1_g[l].reshape(1, D), ln1_b[l].reshape(1, D),
                                sc2, sh2, w_router, b_router, alpha)
        dest, w_top, block_e, n_used, P = _route(logits)
        xb = _dispatch(x1, dest, P)
        yb = _experts(xb, sc2, sh2, w_eg, w_eu, w_ed, block_e, n_used, l)
        xs = _combine_ln(x1, yb, dest, w_top, g2, ln2_g[l].reshape(1, D), ln2_b[l].reshape(1, D), alpha)
    return xs.reshape(B, S, D)
```

```python
import functools
import math

import jax
import jax.numpy as jnp
from jax import lax
from jax.experimental import pallas as pl
from jax.experimental.pallas import tpu as pltpu

F32 = jnp.float32
BF16 = jnp.bfloat16

D_MODEL = 2048
HEAD_DIM = 64
SWA_Q_HEADS = 16
SWA_KV_HEADS = 4
WINDOW = 128
SB_HEADS = 16
DIFF_HEADS = 8
BLOCK_Q = 128
BRANCH_WIDTH = 1024
A_KV = 256
N_BRANCHES = 3
N_GROUPS = 8
EXPERTS_PER_GROUP = 8
N_EXPERTS = 64
D_EXPERT = 384
DISPATCH_BLOCK = 128
DEPTH = 4
ALPHA = (2.0 * DEPTH) ** 0.25
LN_EPS = 1e-5
RMS_EPS = 1e-5
QK_SCALE = 1.0 / math.sqrt(HEAD_DIM)

COL_QA, COL_KA, COL_VA = 0, 1024, 1280
COL_QB, COL_KB, COL_VB = 1536, 2560, 3584
COL_QC, COL_KC, COL_VC = 4608, 5632, 6656

LANES = 128
VMEM_LIMIT = 56 * 1024 * 1024
NEG = -1e30
SB_EXIT = 88.0
POS_SPLIT = 64
ROW_UNROLL = 8


def _params(sem):
    return pltpu.CompilerParams(dimension_semantics=sem, vmem_limit_bytes=VMEM_LIMIT)


def _ada_body(c_ref, w_ref, b_ref, o_ref):
    o_ref[0] = jnp.sum(c_ref[...] * w_ref[0], axis=0, keepdims=True) + b_ref[0]


def _ada_mod(c, w_ada, b_ada):
    L, D, N = w_ada.shape
    tn = 512
    return pl.pallas_call(
        _ada_body,
        grid=(L, N // tn),
        in_specs=[pl.BlockSpec((D, 1), lambda l, j: (0, 0)),
                  pl.BlockSpec((1, D, tn), lambda l, j: (l, 0, j)),
                  pl.BlockSpec((1, 1, tn), lambda l, j: (l, 0, j))],
        out_specs=pl.BlockSpec((1, 1, tn), lambda l, j: (l, 0, j)),
        out_shape=jax.ShapeDtypeStruct((L, 1, N), F32),
        compiler_params=_params(("arbitrary", "arbitrary")),
        name="ada_mod",
    )(c.reshape(D, 1), w_ada, b_ada.reshape(L, 1, N))


def _modmm_body(x_ref, sc_ref, sh_ref, w_ref, b_ref, o_ref, h_ref, *, sigmoid):
    @pl.when(pl.program_id(1) == 0)
    def _():
        h_ref[...] = (x_ref[...] * (1.0 + sc_ref[...]) + sh_ref[...]).astype(BF16)

    acc = jnp.dot(h_ref[...], w_ref[...], preferred_element_type=F32) + b_ref[...]
    if sigmoid:
        acc = 1.0 / (1.0 + jnp.exp(-acc))
    o_ref[...] = acc.astype(o_ref.dtype)


def _mod_matmul(x, sc, sh, w, b, *, sigmoid, name):
    M, K = x.shape
    N = w.shape[1]
    tm = min(1024, M)
    tn = 512
    return pl.pallas_call(
        functools.partial(_modmm_body, sigmoid=sigmoid),
        grid=(M // tm, N // tn),
        in_specs=[pl.BlockSpec((tm, K), lambda i, j: (i, 0)),
                  pl.BlockSpec((1, K), lambda i, j: (0, 0)),
                  pl.BlockSpec((1, K), lambda i, j: (0, 0)),
                  pl.BlockSpec((K, tn), lambda i, j: (0, j)),
                  pl.BlockSpec((1, tn), lambda i, j: (0, j))],
        out_specs=pl.BlockSpec((tm, tn), lambda i, j: (i, j)),
        out_shape=jax.ShapeDtypeStruct((M, N), BF16),
        scratch_shapes=[pltpu.VMEM((tm, K), BF16)],
        compiler_params=_params(("arbitrary", "arbitrary")),
        name=name,
    )(x, sc, sh, w, b)


def _swa_body(sinks_ref, q_ref, kp_ref, kc_ref, vp_ref, vc_ref, o_ref):
    i = pl.program_id(0)
    k = jnp.concatenate([kp_ref[...], kc_ref[...]], axis=0)
    v = jnp.concatenate([vp_ref[...], vc_ref[...]], axis=0)
    qi = lax.broadcasted_iota(jnp.int32, (BLOCK_Q, 2 * BLOCK_Q), 0)
    ki = lax.broadcasted_iota(jnp.int32, (BLOCK_Q, 2 * BLOCK_Q), 1)
    dist = qi + BLOCK_Q - ki
    valid = (dist >= 0) & (dist < WINDOW) & ((ki >= BLOCK_Q) | (i > 0))
    distf = dist.astype(F32)
    group = SWA_Q_HEADS // SWA_KV_HEADS
    outs = []
    for h in range(SWA_Q_HEADS):
        g = h // group
        slope = 2.0 ** (-8.0 * (h + 1) / SWA_Q_HEADS)
        qh = q_ref[:, h * HEAD_DIM:(h + 1) * HEAD_DIM]
        kg = k[:, g * HEAD_DIM:(g + 1) * HEAD_DIM]
        vg = v[:, g * HEAD_DIM:(g + 1) * HEAD_DIM]
        s = lax.dot_general(qh, kg, (((1,), (1,)), ((), ())), preferred_element_type=F32) * QK_SCALE
        s = jnp.where(valid, s - slope * distf, NEG)
        sink = sinks_ref[h]
        m = jnp.maximum(jnp.max(s, axis=1, keepdims=True), sink)
        p = jnp.where(valid, jnp.exp(s - m), 0.0)
        denom = jnp.sum(p, axis=1, keepdims=True) + jnp.exp(sink - m)
        o = jnp.dot(p.astype(BF16), vg, preferred_element_type=F32) / denom
        outs.append(o)
    o_ref[...] = jnp.concatenate(outs, axis=1).astype(o_ref.dtype)


def _swa_attention(proj, sinks):
    S = proj.shape[0]
    nb = S // BLOCK_Q
    kv_blk = lambda col: pl.BlockSpec((BLOCK_Q, A_KV), lambda i: (i, col // A_KV))
    kv_prev = lambda col: pl.BlockSpec((BLOCK_Q, A_KV), lambda i: (jnp.maximum(i - 1, 0), col // A_KV))
    return pl.pallas_call(
        _swa_body,
        grid=(nb,),
        in_specs=[pl.BlockSpec(memory_space=pltpu.SMEM),
                  pl.BlockSpec((BLOCK_Q, BRANCH_WIDTH), lambda i: (i, COL_QA // BRANCH_WIDTH)),
                  kv_prev(COL_KA), kv_blk(COL_KA), kv_prev(COL_VA), kv_blk(COL_VA)],
        out_specs=pl.BlockSpec((BLOCK_Q, BRANCH_WIDTH), lambda i: (i, 0)),
        out_shape=jax.ShapeDtypeStruct((S, BRANCH_WIDTH), BF16),
        compiler_params=_params(("arbitrary",)),
        name="swa_attention",
    )(sinks, proj, proj, proj, proj, proj)


def _sb_body(q_ref, k_ref, v_ref, o_ref, vt_ref, acc_ref, *, tq):
    i = pl.program_id(1)
    nkb = vt_ref.shape[0]
    lane = lax.broadcasted_iota(jnp.int32, (1, LANES), 1)

    @pl.when(i == 0)
    def _():
        def setup(c, carry):
            off = pl.multiple_of(c * tq, tq)
            vt_ref[c] = v_ref[pl.ds(off, tq), :].astype(F32).T.astype(BF16)
            return carry

        lax.fori_loop(0, nkb, setup, 0)

    k_row = lax.broadcasted_iota(jnp.int32, (tq, tq), 0)
    q_col = lax.broadcasted_iota(jnp.int32, (tq, tq), 1)
    later = (q_col > k_row).astype(BF16)
    qs = q_ref[...] * QK_SCALE
    qm = [jnp.where((lane >= HEAD_DIM) if half else (lane < HEAD_DIM), qs, jnp.zeros_like(qs))
          for half in range(2)]
    acc_ref[...] = jnp.zeros_like(acc_ref)

    def process(jb, diagonal, carries):
        off = pl.multiple_of(jb * tq, tq)
        kb = k_ref[pl.ds(off, tq), :]
        vtb = vt_ref[jb]
        zs = [lax.dot_general(kb, qm[half], (((1,), (1,)), ((), ())), preferred_element_type=F32)
              for half in range(2)]
        log_rems, splits = [], []
        for half in range(2):
            z = zs[half]
            soft = jnp.log1p(jnp.exp(-jnp.abs(z)))
            log_rem = -(jnp.maximum(z, 0.0) + soft)
            if diagonal:
                log_rem = jnp.where(k_row < q_col, log_rem, 0.0)
            hi = log_rem.astype(BF16)
            lo = (log_rem - hi.astype(F32)).astype(BF16)
            log_rems.append(log_rem)
            splits.append((hi, lo))
        afters = [jnp.dot(later, hi, preferred_element_type=F32) + jnp.dot(later, lo, preferred_element_type=F32)
                  for hi, lo in splits]
        new_carries = []
        for half in range(2):
            log_w = (log_rems[half] + zs[half]) + (afters[half] + carries[half])
            w = jnp.exp(log_w)
            if diagonal:
                w = jnp.where(k_row < q_col, w, 0.0)
            rows = slice(half * HEAD_DIM, (half + 1) * HEAD_DIM)
            acc_ref[half] += jnp.dot(vtb[rows, :], w.astype(BF16), preferred_element_type=F32)
            new_carries.append(carries[half] + afters[half][0:1, :] + log_rems[half][0:1, :])
        return tuple(new_carries)

    zero = jnp.zeros((1, tq), F32)
    carries = process(i, True, (zero, zero))

    def top(carries):
        return jnp.maximum(jnp.max(carries[0]), jnp.max(carries[1]))

    def cond(state):
        j, _, best = state
        return (j >= 0) & (best > -SB_EXIT)

    def body(state):
        j, carries, _ = state
        carries = process(j, False, carries)
        return j - 1, carries, top(carries)

    lax.while_loop(cond, body, (i - 1, carries, top(carries)))
    o_ref[...] = jnp.concatenate([acc_ref[0], acc_ref[1]], axis=0).T.astype(o_ref.dtype)


def _sb_attention(proj):
    S = proj.shape[0]
    tq = min(256, S)
    npair = BRANCH_WIDTH // LANES
    return pl.pallas_call(
        functools.partial(_sb_body, tq=tq),
        grid=(npair, S // tq),
        in_specs=[pl.BlockSpec((tq, LANES), lambda p, i: (i, COL_QB // LANES + p)),
                  pl.BlockSpec((S, LANES), lambda p, i: (0, COL_KB // LANES + p)),
                  pl.BlockSpec((S, LANES), lambda p, i: (0, COL_VB // LANES + p))],
        out_specs=pl.BlockSpec((tq, LANES), lambda p, i: (i, p)),
        out_shape=jax.ShapeDtypeStruct((S, BRANCH_WIDTH), BF16),
        scratch_shapes=[pltpu.VMEM((S // tq, LANES, tq), BF16), pltpu.VMEM((2, HEAD_DIM, tq), F32)],
        compiler_params=_params(("arbitrary", "arbitrary")),
        name="stickbreak_attention",
    )(proj, proj, proj)


def _diff_body(slopes_ref, q_ref, k_ref, v_ref, lq1_ref, lk1_ref, lq2_ref, lk2_ref, g_ref, o_ref,
               kaug_ref, vt_ref, acc_ref, *, tq, tk, lambda_init):
    h = pl.program_id(0)
    i = pl.program_id(1)
    slope = slopes_ref[h]
    nkb = kaug_ref.shape[0] // tk
    lane = lax.broadcasted_iota(jnp.int32, (1, LANES), 1)

    @pl.when(i == 0)
    def _():
        def setup(c, carry):
            off = pl.multiple_of(c * tk, tk)
            pos = off + lax.broadcasted_iota(jnp.int32, (tk, LANES), 0)
            lanes = lax.broadcasted_iota(jnp.int32, (tk, LANES), 1)
            coarse = (pos // POS_SPLIT) * POS_SPLIT
            posm = jnp.where(lanes == 0, coarse, jnp.where(lanes == 1, pos - coarse, 0))
            kaug_ref[pl.ds(off, tk), 0:LANES] = k_ref[pl.ds(off, tk), :]
            kaug_ref[pl.ds(off, tk), LANES:2 * LANES] = posm.astype(F32).astype(BF16)
            vt_ref[c] = v_ref[pl.ds(off, tk), :].astype(F32).T.astype(BF16)
            return carry

        lax.fori_loop(0, nkb, setup, 0)

    qs = q_ref[...] * QK_SCALE
    bias_cols = jnp.broadcast_to(jnp.where(lane < 2, slope, 0.0).astype(BF16), (tq, LANES))
    qa = []
    for half in range(2):
        in_half = (lane >= HEAD_DIM) if half else (lane < HEAD_DIM)
        qa.append(jnp.concatenate([jnp.where(in_half, qs, jnp.zeros_like(qs)), bias_cols], axis=1))
    acc_ref[...] = jnp.zeros_like(acc_ref)
    q_pos = i * tq + lax.broadcasted_iota(jnp.int32, (tk, tq), 1)
    k_row = lax.broadcasted_iota(jnp.int32, (tk, tq), 0)

    def process(jb, masked, state):
        m, l = state
        off = pl.multiple_of(jb * tk, tk)
        kb = kaug_ref[pl.ds(off, tk), :]
        vtb = vt_ref[jb]
        sts = [lax.dot_general(kb, qa[half], (((1,), (1,)), ((), ())), preferred_element_type=F32)
               for half in range(2)]
        new_m, new_l, alphas, ps = [], [], [], []
        for half in range(2):
            st = sts[half]
            if masked:
                st = jnp.where(k_row + jb * tk <= q_pos, st, NEG)
            m_new = jnp.maximum(m[half], jnp.max(st, axis=0, keepdims=True))
            alpha = jnp.exp(m[half] - m_new)
            p = jnp.exp(st - m_new)
            new_l.append(alpha * l[half] + jnp.sum(p, axis=0, keepdims=True))
            new_m.append(m_new)
            alphas.append(alpha)
            ps.append(p.astype(BF16))
        for half in range(2):
            acc_ref[half] = alphas[half] * acc_ref[half] + jnp.dot(vtb, ps[half], preferred_element_type=F32)
        return tuple(new_m), tuple(new_l)

    per_tile = tq // tk
    row = lambda v: jnp.full((1, tq), v, F32)
    state = ((row(NEG), row(NEG)), (row(0.0), row(0.0)))
    state = lax.fori_loop(0, i * per_tile, lambda jb, st: process(jb, False, st), state)
    for d in range(per_tile):
        state = process(i * per_tile + d, True, state)
    _, l = state
    lam = (jnp.exp(jnp.sum(lq1_ref[...] * lk1_ref[...], axis=1, keepdims=True))
           - jnp.exp(jnp.sum(lq2_ref[...] * lk2_ref[...], axis=1, keepdims=True)) + lambda_init)
    o = acc_ref[0] / l[0] - lam * (acc_ref[1] / l[1])
    y = o * lax.rsqrt(jnp.mean(o * o, axis=0, keepdims=True) + RMS_EPS) * g_ref[...]
    o_ref[...] = (y * (1.0 - lambda_init)).T.astype(o_ref.dtype)


def _diff_attention(proj, slopes, lq1, lk1, lq2, lk2, subln_g, lambda_init):
    S = proj.shape[0]
    tq = min(512, S)
    tk = min(256, S)
    vec = lambda n: pl.BlockSpec((1, n), lambda h, i: (0, 0))
    return pl.pallas_call(
        functools.partial(_diff_body, tq=tq, tk=tk, lambda_init=lambda_init),
        grid=(DIFF_HEADS, S // tq),
        in_specs=[pl.BlockSpec(memory_space=pltpu.SMEM),
                  pl.BlockSpec((tq, LANES), lambda h, i: (i, COL_QC // LANES + h)),
                  pl.BlockSpec((S, LANES), lambda h, i: (0, COL_KC // LANES + h)),
                  pl.BlockSpec((S, LANES), lambda h, i: (0, COL_VC // LANES + h)),
                  vec(HEAD_DIM), vec(HEAD_DIM), vec(HEAD_DIM), vec(HEAD_DIM),
                  pl.BlockSpec((2 * HEAD_DIM, 1), lambda h, i: (0, 0))],
        out_specs=pl.BlockSpec((tq, LANES), lambda h, i: (i, h)),
        out_shape=jax.ShapeDtypeStruct((S, BRANCH_WIDTH), BF16),
        scratch_shapes=[pltpu.VMEM((S, 2 * LANES), BF16), pltpu.VMEM((S // tk, LANES, tk), BF16),
                        pltpu.VMEM((2, LANES, tq), F32)],
        compiler_params=_params(("arbitrary", "arbitrary")),
        name="diff_attention",
    )(slopes, proj, proj, proj, lq1.reshape(1, -1), lk1.reshape(1, -1), lq2.reshape(1, -1),
      lk2.reshape(1, -1), subln_g.reshape(-1, 1))


def _branch_body(oa_ref, ob_ref, oc_ref, w_ref, ga_ref, gb_ref, gc_ref, z_ref):
    z = None
    for n, (o_ref, g_ref) in enumerate(((oa_ref, ga_ref), (ob_ref, gb_ref), (oc_ref, gc_ref))):
        y = jnp.dot(o_ref[...], w_ref[n], preferred_element_type=F32)
        t = g_ref[...].astype(F32) * y
        z = t if z is None else z + t
    z_ref[...] = z.astype(z_ref.dtype)


def _branch_merge(o_a, o_b, o_c, w_branch, gates):
    S = o_a.shape[0]
    D = w_branch.shape[2]
    tm = min(1024, S)
    tn = 512
    nj = D // tn
    o_spec = pl.BlockSpec((tm, BRANCH_WIDTH), lambda i, j: (i, 0))
    gate_spec = lambda n: pl.BlockSpec((tm, tn), lambda i, j: (i, n * nj + j))
    return pl.pallas_call(
        _branch_body,
        grid=(S // tm, nj),
        in_specs=[o_spec, o_spec, o_spec,
                  pl.BlockSpec((N_BRANCHES, BRANCH_WIDTH, tn), lambda i, j: (0, 0, j)),
                  gate_spec(0), gate_spec(1), gate_spec(2)],
        out_specs=pl.BlockSpec((tm, tn), lambda i, j: (i, j)),
        out_shape=jax.ShapeDtypeStruct((S, D), BF16),
        compiler_params=_params(("arbitrary", "arbitrary")),
        name="branch_merge",
    )(o_a, o_b, o_c, w_branch, gates, gates, gates)


def _layer_norm(r, g, b):
    mu = jnp.mean(r, axis=1, keepdims=True)
    d = r - mu
    var = jnp.mean(d * d, axis=1, keepdims=True)
    return d * lax.rsqrt(var + LN_EPS) * g + b


def _out_body(z_ref, w_ref, x_ref, g1_ref, lng_ref, lnb_ref, sc2_ref, sh2_ref, wr_ref, br_ref,
              x1_ref, logit_ref, *, alpha):
    y = jnp.dot(z_ref[...], w_ref[...], preferred_element_type=F32)
    x1 = _layer_norm(alpha * x_ref[...] + g1_ref[...] * y, lng_ref[...], lnb_ref[...])
    x1_ref[...] = x1
    h2 = x1 * (1.0 + sc2_ref[...]) + sh2_ref[...]
    logit_ref[...] = jnp.dot(h2, wr_ref[...], preferred_element_type=F32,
                             precision=lax.Precision.HIGHEST) + br_ref[...]


def _mixer_out(z, w_out, x, g1, ln_g, ln_b, sc2, sh2, w_router, b_router, alpha):
    S, D = x.shape
    tm = min(256, S)
    row = pl.BlockSpec((tm, D), lambda i: (i, 0))
    vec = pl.BlockSpec((1, D), lambda i: (0, 0))
    return pl.pallas_call(
        functools.partial(_out_body, alpha=alpha),
        grid=(S // tm,),
        in_specs=[row, pl.BlockSpec((D, D), lambda i: (0, 0)), row, vec, vec, vec, vec, vec,
                  pl.BlockSpec((D, LANES), lambda i: (0, 0)), pl.BlockSpec((1, LANES), lambda i: (0, 0))],
        out_specs=[row, pl.BlockSpec((tm, LANES), lambda i: (i, 0))],
        out_shape=[jax.ShapeDtypeStruct((S, D), F32), jax.ShapeDtypeStruct((S, LANES), F32)],
        compiler_params=_params(("arbitrary",)),
        name="mixer_out_ln",
    )(z, w_out, x, g1, ln_g, ln_b, sc2, sh2, w_router, b_router)


def _dispatch_body(dest_ref, x_ref, xb_in_hbm, xb_hbm, sem, *, rows_per_step):
    del xb_in_hbm
    i = pl.program_id(0)

    def row_copy(r, d):
        return pltpu.make_async_copy(x_ref.at[pl.ds(r, 1)], xb_hbm.at[pl.ds(d, 1)], sem)

    def issue(g, carry):
        for u in range(ROW_UNROLL):
            r = g * ROW_UNROLL + u
            t = i * rows_per_step + r
            for k in range(2):
                row_copy(r, dest_ref[2 * t + k]).start()
        return carry

    def wait(g, carry):
        for _ in range(2 * ROW_UNROLL):
            row_copy(0, 0).wait()
        return carry

    lax.fori_loop(0, rows_per_step // ROW_UNROLL, issue, 0)
    lax.fori_loop(0, rows_per_step // ROW_UNROLL, wait, 0)


def _dispatch(x1, dest, padded_rows):
    T, D = x1.shape
    rows_per_step = min(512, T)
    xb0 = jnp.zeros((padded_rows, D), F32)
    return pl.pallas_call(
        functools.partial(_dispatch_body, rows_per_step=rows_per_step),
        grid_spec=pltpu.PrefetchScalarGridSpec(
            num_scalar_prefetch=1,
            grid=(T // rows_per_step,),
            in_specs=[pl.BlockSpec((rows_per_step, D), lambda i, d: (i, 0)),
                      pl.BlockSpec(memory_space=pl.ANY)],
            out_specs=pl.BlockSpec(memory_space=pl.ANY),
            scratch_shapes=[pltpu.SemaphoreType.DMA(())]),
        out_shape=jax.ShapeDtypeStruct((padded_rows, D), F32),
        input_output_aliases={2: 0},
        compiler_params=_params(("arbitrary",)),
        name="moe_dispatch",
    )(dest, x1, xb0)


def _expert_body(be_ref, nu_ref, xb_ref, sc_ref, sh_ref, wg_ref, wu_ref, wd_ref, yb_ref,
                 wg_s, wu_s, wd_s):
    b = pl.program_id(0)
    nu = nu_ref[0]

    @pl.when(b < nu)
    def _():
        e = be_ref[b]
        prev = be_ref[jnp.maximum(b - 1, 0)]

        @pl.when((b == 0) | (e != prev))
        def _():
            wg_s[...] = wg_ref[0].astype(BF16)
            wu_s[...] = wu_ref[0].astype(BF16)
            wd_s[...] = wd_ref[0].astype(BF16)

        h = (xb_ref[...] * (1.0 + sc_ref[...]) + sh_ref[...]).astype(BF16)
        a = jnp.dot(h, wg_s[...], preferred_element_type=F32)
        u = jnp.dot(h, wu_s[...], preferred_element_type=F32)
        act = (a / (1.0 + jnp.exp(-a))) * u
        yb_ref[...] = jnp.dot(act.astype(BF16), wd_s[...], preferred_element_type=F32)

    @pl.when(b >= nu)
    def _():
        yb_ref[...] = jnp.zeros_like(yb_ref)


def _experts(xb, sc2, sh2, w_g, w_u, w_d, block_e, n_used, layer):
    P, D = xb.shape
    DB = DISPATCH_BLOCK
    nblk = P // DB
    blk = lambda b, be, nu: jnp.minimum(b, nu[0] - 1)
    wsel = lambda b, be, nu: (layer * N_EXPERTS + be[blk(b, be, nu)], 0, 0)
    return pl.pallas_call(
        _expert_body,
        grid_spec=pltpu.PrefetchScalarGridSpec(
            num_scalar_prefetch=2,
            grid=(nblk,),
            in_specs=[pl.BlockSpec((DB, D), lambda b, be, nu: (blk(b, be, nu), 0)),
                      pl.BlockSpec((1, D), lambda b, be, nu: (0, 0)),
                      pl.BlockSpec((1, D), lambda b, be, nu: (0, 0)),
                      pl.BlockSpec((1, D, D_EXPERT), wsel),
                      pl.BlockSpec((1, D, D_EXPERT), wsel),
                      pl.BlockSpec((1, D_EXPERT, D), wsel)],
            out_specs=pl.BlockSpec((DB, D), lambda b, be, nu: (b, 0)),
            scratch_shapes=[pltpu.VMEM((D, D_EXPERT), BF16), pltpu.VMEM((D, D_EXPERT), BF16),
                            pltpu.VMEM((D_EXPERT, D), BF16)]),
        out_shape=jax.ShapeDtypeStruct((P, D), F32),
        compiler_params=_params(("arbitrary",)),
        name="moe_experts",
    )(block_e, n_used, xb, sc2, sh2, w_g, w_u, w_d)


def _combine_body(dest_ref, x1_ref, wt_ref, g2_ref, lng_ref, lnb_ref, yb_hbm, o_ref,
                  buf0, buf1, sem, *, tm, alpha):
    i = pl.program_id(0)
    bufs = (buf0, buf1)

    def row_copy(d, k, r):
        return pltpu.make_async_copy(yb_hbm.at[pl.ds(d, 1)], bufs[k].at[pl.ds(r, 1)], sem)

    def issue(g, carry):
        for u in range(ROW_UNROLL):
            r = g * ROW_UNROLL + u
            t = i * tm + r
            for k in range(2):
                row_copy(dest_ref[2 * t + k], k, r).start()
        return carry

    def wait(g, carry):
        for _ in range(ROW_UNROLL):
            for k in range(2):
                row_copy(0, k, 0).wait()
        return carry

    lax.fori_loop(0, tm // ROW_UNROLL, issue, 0)
    lax.fori_loop(0, tm // ROW_UNROLL, wait, 0)
    wt = wt_ref[...]
    y = buf0[...] * wt[:, 0:1] + buf1[...] * wt[:, 1:2]
    o_ref[...] = _layer_norm(alpha * x1_ref[...] + g2_ref[...] * y, lng_ref[...], lnb_ref[...])


def _combine_ln(x1, yb, dest, w_top, g2, ln_g, ln_b, alpha):
    T, D = x1.shape
    tm = min(256, T)
    row = pl.BlockSpec((tm, D), lambda i, d: (i, 0))
    vec = pl.BlockSpec((1, D), lambda i, d: (0, 0))
    return pl.pallas_call(
        functools.partial(_combine_body, tm=tm, alpha=alpha),
        grid_spec=pltpu.PrefetchScalarGridSpec(
            num_scalar_prefetch=1,
            grid=(T // tm,),
            in_specs=[row, pl.BlockSpec((tm, 2), lambda i, d: (i, 0)), vec, vec, vec,
                      pl.BlockSpec(memory_space=pl.ANY)],
            out_specs=row,
            scratch_shapes=[pltpu.VMEM((tm, D), F32), pltpu.VMEM((tm, D), F32),
                            pltpu.SemaphoreType.DMA(())]),
        out_shape=jax.ShapeDtypeStruct((T, D), F32),
        compiler_params=_params(("arbitrary",)),
        name="moe_combine_ln",
    )(dest, x1, w_top, g2, ln_g, ln_b, yb)


def _route(logits):
    T = logits.shape[0]
    DB = DISPATCH_BLOCK
    group_logits = logits[:, :N_GROUPS]
    exp_logits = logits[:, N_GROUPS:N_GROUPS + N_EXPERTS].reshape(T, N_GROUPS, EXPERTS_PER_GROUP)
    group_prob = jax.nn.softmax(group_logits, axis=-1)
    g_top = jnp.argmax(group_logits, axis=-1)
    p_group = jnp.take_along_axis(group_prob, g_top[:, None], axis=-1)
    in_group = jnp.take_along_axis(exp_logits, g_top[:, None, None], axis=1)[:, 0]
    top_val, top_idx = lax.top_k(in_group, 2)
    w_top = jax.nn.softmax(top_val, axis=-1) * p_group
    expert_id = g_top[:, None].astype(jnp.int32) * EXPERTS_PER_GROUP + top_idx.astype(jnp.int32)

    member = (expert_id[:, :, None] == jnp.arange(N_EXPERTS, dtype=jnp.int32)).any(axis=1).astype(jnp.int32)
    incl = jnp.cumsum(member, axis=0)
    counts = incl[-1]
    rank = jnp.take_along_axis(incl - member, expert_id, axis=1)
    padded = (counts + DB - 1) // DB * DB
    pend = jnp.cumsum(padded)
    pstart = pend - padded
    dest = (pstart[expert_id] + rank).astype(jnp.int32)
    P = ((2 * T + DB - 1) // DB) * DB + N_EXPERTS * DB
    nblk = P // DB
    block_e = jnp.minimum(jnp.searchsorted(pend, jnp.arange(nblk, dtype=jnp.int32) * DB, side='right'),
                          N_EXPERTS - 1).astype(jnp.int32)
    n_used = (pend[-1:] // DB).astype(jnp.int32)
    return dest.reshape(-1), w_top, block_e, n_used, P


def kernel(x, c, w_ada, b_ada, w_in, w_branch_gate, b_branch_gate, attn_sinks, lambda_q1, lambda_k1,
           lambda_q2, lambda_k2, subln_g, w_branch, w_out, ln1_g, ln1_b, w_router_group, b_router_group,
           w_router_expert, b_router_expert, w_exp_gate, w_exp_up, w_exp_down, ln2_g, ln2_b):
    B, S, D = x.shape
    assert B == 1 and D == D_MODEL
    depth = w_in.shape[0]
    alpha = ALPHA
    xs = x.reshape(S, D)
    mod = _ada_mod(c, w_ada, b_ada)
    diff_slopes = jnp.exp2(-8.0 * jnp.arange(1, DIFF_HEADS + 1, dtype=F32) / DIFF_HEADS)
    zero_bias = jnp.zeros((1, w_in.shape[2]), F32)
    pad = LANES - N_GROUPS - N_EXPERTS
    w_eg = w_exp_gate.reshape(depth * N_EXPERTS, D, D_EXPERT)
    w_eu = w_exp_up.reshape(depth * N_EXPERTS, D, D_EXPERT)
    w_ed = w_exp_down.reshape(depth * N_EXPERTS, D_EXPERT, D)
    for l in range(depth):
        lambda_init = 0.8 - 0.6 * math.exp(-0.3 * l)
        sh1, sc1, g1, sh2, sc2, g2 = [mod[l, :, n * D:(n + 1) * D] for n in range(6)]
        proj = _mod_matmul(xs, sc1, sh1, w_in[l].astype(BF16), zero_bias, sigmoid=False, name="in_proj")
        gates = _mod_matmul(xs, sc1, sh1, w_branch_gate[l].astype(BF16), b_branch_gate[l].reshape(1, -1),
                            sigmoid=True, name="branch_gates")
        o_a = _swa_attention(proj, attn_sinks[l])
        o_b = _sb_attention(proj)
        o_c = _diff_attention(proj, diff_slopes, lambda_q1[l], lambda_k1[l], lambda_q2[l], lambda_k2[l],
                              subln_g[l], lambda_init)
        z = _branch_merge(o_a, o_b, o_c, w_branch[l].astype(BF16), gates)
        w_router = jnp.pad(jnp.concatenate([w_router_group[l], w_router_expert[l]], axis=1), ((0, 0), (0, pad)))
        b_router = jnp.pad(jnp.concatenate([b_router_group[l], b_router_expert[l]]), (0, pad)).reshape(1, LANES)
        x1, logits = _mixer_out(z, w_out[l].astype(BF16), xs, g1, ln1_g[l].reshape(1, D), ln1_b[l].reshape(1, D),
                                sc2, sh2, w_router, b_router, alpha)
        dest, w_top, block_e, n_used, P = _route(logits)
        xb = _dispatch(x1, dest, P)
        yb = _experts(xb, sc2, sh2, w_eg, w_eu, w_ed, block_e, n_used, l)
        xs = _combine_ln(x1, yb, dest, w_top, g2, ln2_g[l].reshape(1, D), ln2_b[l].reshape(1, D), alpha)
    return xs.reshape(B, S, D)
```

```python
import functools
import math

import jax
import jax.numpy as jnp
from jax import lax
from jax.experimental import pallas as pl
from jax.experimental.pallas import tpu as pltpu

F32 = jnp.float32
BF16 = jnp.bfloat16

D_MODEL = 2048
HEAD_DIM = 64
SWA_Q_HEADS = 16
SWA_KV_HEADS = 4
WINDOW = 128
SB_HEADS = 16
DIFF_HEADS = 8
BLOCK_Q = 128
BRANCH_WIDTH = 1024
A_KV = 256
N_BRANCHES = 3
N_GROUPS = 8
EXPERTS_PER_GROUP = 8
N_EXPERTS = 64
D_EXPERT = 384
DISPATCH_BLOCK = 128
DEPTH = 4
ALPHA = (2.0 * DEPTH) ** 0.25
LN_EPS = 1e-5
RMS_EPS = 1e-5
QK_SCALE = 1.0 / math.sqrt(HEAD_DIM)

COL_QA, COL_KA, COL_VA = 0, 1024, 1280
COL_QB, COL_KB, COL_VB = 1536, 2560, 3584
COL_QC, COL_KC, COL_VC = 4608, 5632, 6656

LANES = 128
VMEM_LIMIT = 56 * 1024 * 1024
NEG = -1e30
SB_EXIT = 88.0
POS_SPLIT = 64
ROW_UNROLL = 8


def _params(sem):
    return pltpu.CompilerParams(dimension_semantics=sem, vmem_limit_bytes=VMEM_LIMIT)


def _ada_body(c_ref, w_ref, b_ref, o_ref):
    o_ref[0] = jnp.sum(c_ref[...] * w_ref[0], axis=0, keepdims=True) + b_ref[0]


def _ada_mod(c, w_ada, b_ada):
    L, D, N = w_ada.shape
    tn = 512
    return pl.pallas_call(
        _ada_body,
        grid=(L, N // tn),
        in_specs=[pl.BlockSpec((D, 1), lambda l, j: (0, 0)),
                  pl.BlockSpec((1, D, tn), lambda l, j: (l, 0, j)),
                  pl.BlockSpec((1, 1, tn), lambda l, j: (l, 0, j))],
        out_specs=pl.BlockSpec((1, 1, tn), lambda l, j: (l, 0, j)),
        out_shape=jax.ShapeDtypeStruct((L, 1, N), F32),
        compiler_params=_params(("arbitrary", "arbitrary")),
        name="ada_mod",
    )(c.reshape(D, 1), w_ada, b_ada.reshape(L, 1, N))


def _modmm_body(x_ref, sc_ref, sh_ref, w_ref, b_ref, o_ref, h_ref, *, sigmoid):
    @pl.when(pl.program_id(1) == 0)
    def _():
        h_ref[...] = (x_ref[...] * (1.0 + sc_ref[...]) + sh_ref[...]).astype(BF16)

    acc = jnp.dot(h_ref[...], w_ref[0].astype(BF16), preferred_element_type=F32) + b_ref[...]
    if sigmoid:
        acc = 1.0 / (1.0 + jnp.exp(-acc))
    o_ref[...] = acc.astype(o_ref.dtype)


def _mod_matmul(x, sc, sh, w, layer, b, *, sigmoid, name):
    M, K = x.shape
    N = w.shape[2]
    tm = min(1024, M)
    tn = 512
    return pl.pallas_call(
        functools.partial(_modmm_body, sigmoid=sigmoid),
        grid=(M // tm, N // tn),
        in_specs=[pl.BlockSpec((tm, K), lambda i, j: (i, 0)),
                  pl.BlockSpec((1, K), lambda i, j: (0, 0)),
                  pl.BlockSpec((1, K), lambda i, j: (0, 0)),
                  pl.BlockSpec((1, K, tn), lambda i, j: (layer, 0, j)),
                  pl.BlockSpec((1, tn), lambda i, j: (0, j))],
        out_specs=pl.BlockSpec((tm, tn), lambda i, j: (i, j)),
        out_shape=jax.ShapeDtypeStruct((M, N), BF16),
        scratch_shapes=[pltpu.VMEM((tm, K), BF16)],
        compiler_params=_params(("arbitrary", "arbitrary")),
        name=name,
    )(x, sc, sh, w, b)


def _swa_body(sinks_ref, q_ref, kp_ref, kc_ref, vp_ref, vc_ref, o_ref):
    i = pl.program_id(0)
    k = jnp.concatenate([kp_ref[...], kc_ref[...]], axis=0)
    v = jnp.concatenate([vp_ref[...], vc_ref[...]], axis=0)
    qi = lax.broadcasted_iota(jnp.int32, (BLOCK_Q, 2 * BLOCK_Q), 0)
    ki = lax.broadcasted_iota(jnp.int32, (BLOCK_Q, 2 * BLOCK_Q), 1)
    dist = qi + BLOCK_Q - ki
    valid = (dist >= 0) & (dist < WINDOW) & ((ki >= BLOCK_Q) | (i > 0))
    distf = dist.astype(F32)
    group = SWA_Q_HEADS // SWA_KV_HEADS
    outs = []
    for h in range(SWA_Q_HEADS):
        g = h // group
        slope = 2.0 ** (-8.0 * (h + 1) / SWA_Q_HEADS)
        qh = q_ref[:, h * HEAD_DIM:(h + 1) * HEAD_DIM]
        kg = k[:, g * HEAD_DIM:(g + 1) * HEAD_DIM]
        vg = v[:, g * HEAD_DIM:(g + 1) * HEAD_DIM]
        s = lax.dot_general(qh, kg, (((1,), (1,)), ((), ())), preferred_element_type=F32) * QK_SCALE
        s = jnp.where(valid, s - slope * distf, NEG)
        sink = sinks_ref[h]
        m = jnp.maximum(jnp.max(s, axis=1, keepdims=True), sink)
        p = jnp.where(valid, jnp.exp(s - m), 0.0)
        denom = jnp.sum(p, axis=1, keepdims=True) + jnp.exp(sink - m)
        o = jnp.dot(p.astype(BF16), vg, preferred_element_type=F32) / denom
        outs.append(o)
    o_ref[...] = jnp.concatenate(outs, axis=1).astype(o_ref.dtype)


def _swa_attention(proj, sinks):
    S = proj.shape[0]
    nb = S // BLOCK_Q
    kv_blk = lambda col: pl.BlockSpec((BLOCK_Q, A_KV), lambda i: (i, col // A_KV))
    kv_prev = lambda col: pl.BlockSpec((BLOCK_Q, A_KV), lambda i: (jnp.maximum(i - 1, 0), col // A_KV))
    return pl.pallas_call(
        _swa_body,
        grid=(nb,),
        in_specs=[pl.BlockSpec(memory_space=pltpu.SMEM),
                  pl.BlockSpec((BLOCK_Q, BRANCH_WIDTH), lambda i: (i, COL_QA // BRANCH_WIDTH)),
                  kv_prev(COL_KA), kv_blk(COL_KA), kv_prev(COL_VA), kv_blk(COL_VA)],
        out_specs=pl.BlockSpec((BLOCK_Q, BRANCH_WIDTH), lambda i: (i, 0)),
        out_shape=jax.ShapeDtypeStruct((S, BRANCH_WIDTH), BF16),
        compiler_params=_params(("arbitrary",)),
        name="swa_attention",
    )(sinks, proj, proj, proj, proj, proj)


def _sb_body(q_ref, k_ref, v_ref, o_ref, vt_ref, acc_ref, *, tq):
    i = pl.program_id(1)
    nkb = vt_ref.shape[0]
    lane = lax.broadcasted_iota(jnp.int32, (1, LANES), 1)

    @pl.when(i == 0)
    def _():
        def setup(c, carry):
            off = pl.multiple_of(c * tq, tq)
            vt_ref[c] = v_ref[pl.ds(off, tq), :].astype(F32).T.astype(BF16)
            return carry

        lax.fori_loop(0, nkb, setup, 0)

    k_row = lax.broadcasted_iota(jnp.int32, (tq, tq), 0)
    q_col = lax.broadcasted_iota(jnp.int32, (tq, tq), 1)
    later = (q_col > k_row).astype(BF16)
    qs = q_ref[...] * QK_SCALE
    qm = [jnp.where((lane >= HEAD_DIM) if half else (lane < HEAD_DIM), qs, jnp.zeros_like(qs))
          for half in range(2)]
    acc_ref[...] = jnp.zeros_like(acc_ref)

    def process(jb, diagonal, carries):
        off = pl.multiple_of(jb * tq, tq)
        kb = k_ref[pl.ds(off, tq), :]
        vtb = vt_ref[jb]
        zs = [lax.dot_general(kb, qm[half], (((1,), (1,)), ((), ())), preferred_element_type=F32)
              for half in range(2)]
        log_rems, splits = [], []
        for half in range(2):
            z = zs[half]
            soft = jnp.log(1.0 + jnp.exp(-jnp.abs(z)))
            log_rem = -(jnp.maximum(z, 0.0) + soft)
            if diagonal:
                log_rem = jnp.where(k_row < q_col, log_rem, 0.0)
            hi = log_rem.astype(BF16)
            lo = (log_rem - hi.astype(F32)).astype(BF16)
            log_rems.append(log_rem)
            splits.append((hi, lo))
        afters = [jnp.dot(later, hi, preferred_element_type=F32) + jnp.dot(later, lo, preferred_element_type=F32)
                  for hi, lo in splits]
        new_carries = []
        for half in range(2):
            log_w = (log_rems[half] + zs[half]) + (afters[half] + carries[half])
            w = jnp.exp(log_w)
            if diagonal:
                w = jnp.where(k_row < q_col, w, 0.0)
            rows = slice(half * HEAD_DIM, (half + 1) * HEAD_DIM)
            acc_ref[half] += jnp.dot(vtb[rows, :], w.astype(BF16), preferred_element_type=F32)
            new_carries.append(carries[half] + afters[half][0:1, :] + log_rems[half][0:1, :])
        return tuple(new_carries)

    zero = jnp.zeros((1, tq), F32)
    carries = process(i, True, (zero, zero))

    def top(carries):
        return jnp.maximum(jnp.max(carries[0]), jnp.max(carries[1]))

    def cond(state):
        j, _, best = state
        return (j >= 0) & (best > -SB_EXIT)

    def body(state):
        j, carries, _ = state
        carries = process(j, False, carries)
        return j - 1, carries, top(carries)

    lax.while_loop(cond, body, (i - 1, carries, top(carries)))
    o_ref[...] = jnp.concatenate([acc_ref[0], acc_ref[1]], axis=0).T.astype(o_ref.dtype)


def _sb_attention(proj):
    S = proj.shape[0]
    tq = min(256, S)
    npair = BRANCH_WIDTH // LANES
    return pl.pallas_call(
        functools.partial(_sb_body, tq=tq),
        grid=(npair, S // tq),
        in_specs=[pl.BlockSpec((tq, LANES), lambda p, i: (i, COL_QB // LANES + p)),
                  pl.BlockSpec((S, LANES), lambda p, i: (0, COL_KB // LANES + p)),
                  pl.BlockSpec((S, LANES), lambda p, i: (0, COL_VB // LANES + p))],
        out_specs=pl.BlockSpec((tq, LANES), lambda p, i: (i, p)),
        out_shape=jax.ShapeDtypeStruct((S, BRANCH_WIDTH), BF16),
        scratch_shapes=[pltpu.VMEM((S // tq, LANES, tq), BF16), pltpu.VMEM((2, HEAD_DIM, tq), F32)],
        compiler_params=_params(("arbitrary", "arbitrary")),
        name="stickbreak_attention",
    )(proj, proj, proj)


def _diff_body(slopes_ref, q_ref, k_ref, v_ref, lq1_ref, lk1_ref, lq2_ref, lk2_ref, g_ref, o_ref,
               kaug_ref, vt_ref, qa_ref, sta_ref, stb_ref, m_ref, l_ref, acc_ref, *, tq, lambda_init):
    h = pl.program_id(0)
    i = pl.program_id(1)
    slope = slopes_ref[h]
    nkb = kaug_ref.shape[0] // tq
    lane = lax.broadcasted_iota(jnp.int32, (1, LANES), 1)

    @pl.when(i == 0)
    def _():
        def setup(c, carry):
            off = pl.multiple_of(c * tq, tq)
            pos = off + lax.broadcasted_iota(jnp.int32, (tq, LANES), 0)
            lanes = lax.broadcasted_iota(jnp.int32, (tq, LANES), 1)
            coarse = (pos // POS_SPLIT) * POS_SPLIT
            posm = jnp.where(lanes == 0, coarse, jnp.where(lanes == 1, pos - coarse, 0))
            kaug_ref[pl.ds(off, tq), 0:LANES] = k_ref[pl.ds(off, tq), :]
            kaug_ref[pl.ds(off, tq), LANES:2 * LANES] = posm.astype(F32).astype(BF16)
            vt_ref[c] = v_ref[pl.ds(off, tq), :].astype(F32).T.astype(BF16)
            return carry

        lax.fori_loop(0, nkb, setup, 0)

    qs = q_ref[...] * QK_SCALE
    bias_cols = jnp.broadcast_to(jnp.where(lane < 2, slope, 0.0).astype(BF16), (tq, LANES))
    for half in range(2):
        in_half = (lane >= HEAD_DIM) if half else (lane < HEAD_DIM)
        qa_ref[half] = jnp.concatenate([jnp.where(in_half, qs, jnp.zeros_like(qs)), bias_cols], axis=1)
    acc_ref[...] = jnp.zeros_like(acc_ref)
    m_ref[...] = jnp.full_like(m_ref, NEG)
    l_ref[...] = jnp.zeros_like(l_ref)
    on_or_below_diagonal = (lax.broadcasted_iota(jnp.int32, (tq, tq), 0)
                            <= lax.broadcasted_iota(jnp.int32, (tq, tq), 1))

    def scores(jb, st_ref):
        off = pl.multiple_of(jb * tq, tq)
        kb = kaug_ref[pl.ds(off, tq), :]
        for half in range(2):
            st_ref[half] = lax.dot_general(kb, qa_ref[half], (((1,), (1,)), ((), ())),
                                           preferred_element_type=F32)

    def softmax_pv(jb, st_ref, diagonal):
        vtb = vt_ref[jb]
        alphas, ps = [], []
        for half in range(2):
            st = st_ref[half]
            if diagonal:
                st = jnp.where(on_or_below_diagonal, st, NEG)
            m_old = m_ref[half]
            m_new = jnp.maximum(m_old, jnp.max(st, axis=0, keepdims=True))
            alpha = jnp.exp(m_old - m_new)
            p = jnp.exp(st - m_new)
            l_ref[half] = alpha * l_ref[half] + jnp.sum(p, axis=0, keepdims=True)
            m_ref[half] = m_new
            alphas.append(alpha)
            ps.append(p.astype(BF16))
        for half in range(2):
            acc_ref[half] = alphas[half] * acc_ref[half] + jnp.dot(vtb, ps[half], preferred_element_type=F32)

    scores(0, sta_ref)

    def pair(p, carry):
        scores(2 * p + 1, stb_ref)
        softmax_pv(2 * p, sta_ref, False)
        scores(2 * p + 2, sta_ref)
        softmax_pv(2 * p + 1, stb_ref, False)
        return carry

    lax.fori_loop(0, i // 2, pair, 0)

    @pl.when(i % 2 == 0)
    def _():
        softmax_pv(i, sta_ref, True)

    @pl.when(i % 2 == 1)
    def _():
        scores(i, stb_ref)
        softmax_pv(i - 1, sta_ref, False)
        softmax_pv(i, stb_ref, True)

    lam = (jnp.exp(jnp.sum(lq1_ref[...] * lk1_ref[...], axis=1, keepdims=True))
           - jnp.exp(jnp.sum(lq2_ref[...] * lk2_ref[...], axis=1, keepdims=True)) + lambda_init)
    o = acc_ref[0] / l_ref[0] - lam * (acc_ref[1] / l_ref[1])
    y = o * lax.rsqrt(jnp.mean(o * o, axis=0, keepdims=True) + RMS_EPS) * g_ref[...]
    o_ref[...] = (y * (1.0 - lambda_init)).T.astype(o_ref.dtype)


def _diff_attention(proj, slopes, lq1, lk1, lq2, lk2, subln_g, lambda_init):
    S = proj.shape[0]
    tq = min(512, S)
    vec = lambda n: pl.BlockSpec((1, n), lambda h, i: (0, 0))
    return pl.pallas_call(
        functools.partial(_diff_body, tq=tq, lambda_init=lambda_init),
        grid=(DIFF_HEADS, S // tq),
        in_specs=[pl.BlockSpec(memory_space=pltpu.SMEM),
                  pl.BlockSpec((tq, LANES), lambda h, i: (i, COL_QC // LANES + h)),
                  pl.BlockSpec((S, LANES), lambda h, i: (0, COL_KC // LANES + h)),
                  pl.BlockSpec((S, LANES), lambda h, i: (0, COL_VC // LANES + h)),
                  vec(HEAD_DIM), vec(HEAD_DIM), vec(HEAD_DIM), vec(HEAD_DIM),
                  pl.BlockSpec((2 * HEAD_DIM, 1), lambda h, i: (0, 0))],
        out_specs=pl.BlockSpec((tq, LANES), lambda h, i: (i, h)),
        out_shape=jax.ShapeDtypeStruct((S, BRANCH_WIDTH), BF16),
        scratch_shapes=[pltpu.VMEM((S, 2 * LANES), BF16),
                        pltpu.VMEM((S // tq, LANES, tq), BF16),
                        pltpu.VMEM((2, tq, 2 * LANES), BF16),
                        pltpu.VMEM((2, tq, tq), F32), pltpu.VMEM((2, tq, tq), F32),
                        pltpu.VMEM((2, 1, tq), F32), pltpu.VMEM((2, 1, tq), F32),
                        pltpu.VMEM((2, LANES, tq), F32)],
        compiler_params=_params(("arbitrary", "arbitrary")),
        name="diff_attention",
    )(slopes, proj, proj, proj, lq1.reshape(1, -1), lk1.reshape(1, -1), lq2.reshape(1, -1),
      lk2.reshape(1, -1), subln_g.reshape(-1, 1))


def _branch_body(oa_ref, ob_ref, oc_ref, w_ref, ga_ref, gb_ref, gc_ref, z_ref):
    z = None
    for n, (o_ref, g_ref) in enumerate(((oa_ref, ga_ref), (ob_ref, gb_ref), (oc_ref, gc_ref))):
        y = jnp.dot(o_ref[...], w_ref[0, n].astype(BF16), preferred_element_type=F32)
        t = g_ref[...].astype(F32) * y
        z = t if z is None else z + t
    z_ref[...] = z.astype(z_ref.dtype)


def _branch_merge(o_a, o_b, o_c, w_branch, layer, gates):
    S = o_a.shape[0]
    D = w_branch.shape[3]
    tm = min(1024, S)
    tn = 512
    nj = D // tn
    o_spec = pl.BlockSpec((tm, BRANCH_WIDTH), lambda i, j: (i, 0))
    gate_spec = lambda n: pl.BlockSpec((tm, tn), lambda i, j: (i, n * nj + j))
    return pl.pallas_call(
        _branch_body,
        grid=(S // tm, nj),
        in_specs=[o_spec, o_spec, o_spec,
                  pl.BlockSpec((1, N_BRANCHES, BRANCH_WIDTH, tn), lambda i, j: (layer, 0, 0, j)),
                  gate_spec(0), gate_spec(1), gate_spec(2)],
        out_specs=pl.BlockSpec((tm, tn), lambda i, j: (i, j)),
        out_shape=jax.ShapeDtypeStruct((S, D), BF16),
        compiler_params=_params(("arbitrary", "arbitrary")),
        name="branch_merge",
    )(o_a, o_b, o_c, w_branch, gates, gates, gates)


def _layer_norm(r, g, b):
    mu = jnp.mean(r, axis=1, keepdims=True)
    d = r - mu
    var = jnp.mean(d * d, axis=1, keepdims=True)
    return d * lax.rsqrt(var + LN_EPS) * g + b


def _out_body(z_ref, w_ref, x_ref, g1_ref, lng_ref, lnb_ref, sc2_ref, sh2_ref, wr_ref, br_ref,
              x1_ref, logit_ref, *, alpha):
    y = jnp.dot(z_ref[...], w_ref[...], preferred_element_type=F32)
    x1 = _layer_norm(alpha * x_ref[...] + g1_ref[...] * y, lng_ref[...], lnb_ref[...])
    x1_ref[...] = x1
    h2 = x1 * (1.0 + sc2_ref[...]) + sh2_ref[...]
    logit_ref[...] = jnp.dot(h2, wr_ref[...], preferred_element_type=F32,
                             precision=lax.Precision.HIGHEST) + br_ref[...]


def _mixer_out(z, w_out, x, g1, ln_g, ln_b, sc2, sh2, w_router, b_router, alpha):
    S, D = x.shape
    tm = min(256, S)
    row = pl.BlockSpec((tm, D), lambda i: (i, 0))
    vec = pl.BlockSpec((1, D), lambda i: (0, 0))
    return pl.pallas_call(
        functools.partial(_out_body, alpha=alpha),
        grid=(S // tm,),
        in_specs=[row, pl.BlockSpec((D, D), lambda i: (0, 0)), row, vec, vec, vec, vec, vec,
                  pl.BlockSpec((D, LANES), lambda i: (0, 0)), pl.BlockSpec((1, LANES), lambda i: (0, 0))],
        out_specs=[row, pl.BlockSpec((tm, LANES), lambda i: (i, 0))],
        out_shape=[jax.ShapeDtypeStruct((S, D), F32), jax.ShapeDtypeStruct((S, LANES), F32)],
        compiler_params=_params(("arbitrary",)),
        name="mixer_out_ln",
    )(z, w_out, x, g1, ln_g, ln_b, sc2, sh2, w_router, b_router)


def _dispatch_body(dest_ref, x_ref, xb_in_hbm, xb_hbm, sem, *, rows_per_step):
    del xb_in_hbm
    i = pl.program_id(0)

    def row_copy(r, d):
        return pltpu.make_async_copy(x_ref.at[pl.ds(r, 1)], xb_hbm.at[pl.ds(d, 1)], sem)

    def issue(g, carry):
        for u in range(ROW_UNROLL):
            r = g * ROW_UNROLL + u
            t = i * rows_per_step + r
            for k in range(2):
                row_copy(r, dest_ref[2 * t + k]).start()
        return carry

    def wait(g, carry):
        for _ in range(2 * ROW_UNROLL):
            row_copy(0, 0).wait()
        return carry

    lax.fori_loop(0, rows_per_step // ROW_UNROLL, issue, 0)
    lax.fori_loop(0, rows_per_step // ROW_UNROLL, wait, 0)


def _dispatch(x1, dest, padded_rows):
    T, D = x1.shape
    rows_per_step = min(512, T)
    xb0 = jnp.zeros((padded_rows, D), F32)
    return pl.pallas_call(
        functools.partial(_dispatch_body, rows_per_step=rows_per_step),
        grid_spec=pltpu.PrefetchScalarGridSpec(
            num_scalar_prefetch=1,
            grid=(T // rows_per_step,),
            in_specs=[pl.BlockSpec((rows_per_step, D), lambda i, d: (i, 0)),
                      pl.BlockSpec(memory_space=pl.ANY)],
            out_specs=pl.BlockSpec(memory_space=pl.ANY),
            scratch_shapes=[pltpu.SemaphoreType.DMA(())]),
        out_shape=jax.ShapeDtypeStruct((padded_rows, D), F32),
        input_output_aliases={2: 0},
        compiler_params=_params(("arbitrary",)),
        name="moe_dispatch",
    )(dest, x1, xb0)


def _expert_body(be_ref, nu_ref, xb_ref, sc_ref, sh_ref, wg_ref, wu_ref, wd_ref, yb_ref,
                 wg_s, wu_s, wd_s):
    b = pl.program_id(0)
    nu = nu_ref[0]

    @pl.when(b < nu)
    def _():
        e = be_ref[b]
        prev = be_ref[jnp.maximum(b - 1, 0)]

        @pl.when((b == 0) | (e != prev))
        def _():
            wg_s[...] = wg_ref[0].astype(BF16)
            wu_s[...] = wu_ref[0].astype(BF16)
            wd_s[...] = wd_ref[0].astype(BF16)

        h = (xb_ref[...] * (1.0 + sc_ref[...]) + sh_ref[...]).astype(BF16)
        a = jnp.dot(h, wg_s[...], preferred_element_type=F32)
        u = jnp.dot(h, wu_s[...], preferred_element_type=F32)
        act = (a / (1.0 + jnp.exp(-a))) * u
        yb_ref[...] = jnp.dot(act.astype(BF16), wd_s[...], preferred_element_type=F32)

    @pl.when(b >= nu)
    def _():
        yb_ref[...] = jnp.zeros_like(yb_ref)


def _experts(xb, sc2, sh2, w_g, w_u, w_d, block_e, n_used, layer):
    P, D = xb.shape
    DB = DISPATCH_BLOCK
    nblk = P // DB
    blk = lambda b, be, nu: jnp.minimum(b, nu[0] - 1)
    wsel = lambda b, be, nu: (layer * N_EXPERTS + be[blk(b, be, nu)], 0, 0)
    return pl.pallas_call(
        _expert_body,
        grid_spec=pltpu.PrefetchScalarGridSpec(
            num_scalar_prefetch=2,
            grid=(nblk,),
            in_specs=[pl.BlockSpec((DB, D), lambda b, be, nu: (blk(b, be, nu), 0)),
                      pl.BlockSpec((1, D), lambda b, be, nu: (0, 0)),
                      pl.BlockSpec((1, D), lambda b, be, nu: (0, 0)),
                      pl.BlockSpec((1, D, D_EXPERT), wsel),
                      pl.BlockSpec((1, D, D_EXPERT), wsel),
                      pl.BlockSpec((1, D_EXPERT, D), wsel)],
            out_specs=pl.BlockSpec((DB, D), lambda b, be, nu: (b, 0)),
            scratch_shapes=[pltpu.VMEM((D, D_EXPERT), BF16), pltpu.VMEM((D, D_EXPERT), BF16),
                            pltpu.VMEM((D_EXPERT, D), BF16)]),
        out_shape=jax.ShapeDtypeStruct((P, D), F32),
        compiler_params=_params(("arbitrary",)),
        name="moe_experts",
    )(block_e, n_used, xb, sc2, sh2, w_g, w_u, w_d)


def _combine_body(dest_ref, x1_ref, wt_ref, g2_ref, lng_ref, lnb_ref, yb_hbm, o_ref,
                  buf0, buf1, sem, *, tm, alpha):
    i = pl.program_id(0)
    bufs = (buf0, buf1)

    def row_copy(d, k, r):
        return pltpu.make_async_copy(yb_hbm.at[pl.ds(d, 1)], bufs[k].at[pl.ds(r, 1)], sem)

    def issue(g, carry):
        for u in range(ROW_UNROLL):
            r = g * ROW_UNROLL + u
            t = i * tm + r
            for k in range(2):
                row_copy(dest_ref[2 * t + k], k, r).start()
        return carry

    def wait(g, carry):
        for _ in range(ROW_UNROLL):
            for k in range(2):
                row_copy(0, k, 0).wait()
        return carry

    lax.fori_loop(0, tm // ROW_UNROLL, issue, 0)
    lax.fori_loop(0, tm // ROW_UNROLL, wait, 0)
    wt = wt_ref[...]
    y = buf0[...] * wt[:, 0:1] + buf1[...] * wt[:, 1:2]
    o_ref[...] = _layer_norm(alpha * x1_ref[...] + g2_ref[...] * y, lng_ref[...], lnb_ref[...])


def _combine_ln(x1, yb, dest, w_top, g2, ln_g, ln_b, alpha):
    T, D = x1.shape
    tm = min(256, T)
    row = pl.BlockSpec((tm, D), lambda i, d: (i, 0))
    vec = pl.BlockSpec((1, D), lambda i, d: (0, 0))
    return pl.pallas_call(
        functools.partial(_combine_body, tm=tm, alpha=alpha),
        grid_spec=pltpu.PrefetchScalarGridSpec(
            num_scalar_prefetch=1,
            grid=(T // tm,),
            in_specs=[row, pl.BlockSpec((tm, 2), lambda i, d: (i, 0)), vec, vec, vec,
                      pl.BlockSpec(memory_space=pl.ANY)],
            out_specs=row,
            scratch_shapes=[pltpu.VMEM((tm, D), F32), pltpu.VMEM((tm, D), F32),
                            pltpu.SemaphoreType.DMA(())]),
        out_shape=jax.ShapeDtypeStruct((T, D), F32),
        compiler_params=_params(("arbitrary",)),
        name="moe_combine_ln",
    )(dest, x1, w_top, g2, ln_g, ln_b, yb)


def _route(logits):
    T = logits.shape[0]
    DB = DISPATCH_BLOCK
    group_logits = logits[:, :N_GROUPS]
    exp_logits = logits[:, N_GROUPS:N_GROUPS + N_EXPERTS].reshape(T, N_GROUPS, EXPERTS_PER_GROUP)
    group_prob = jax.nn.softmax(group_logits, axis=-1)
    g_top = jnp.argmax(group_logits, axis=-1)
    p_group = jnp.take_along_axis(group_prob, g_top[:, None], axis=-1)
    in_group = jnp.take_along_axis(exp_logits, g_top[:, None, None], axis=1)[:, 0]
    top_val, top_idx = lax.top_k(in_group, 2)
    w_top = jax.nn.softmax(top_val, axis=-1) * p_group
    expert_id = g_top[:, None].astype(jnp.int32) * EXPERTS_PER_GROUP + top_idx.astype(jnp.int32)

    member = (expert_id[:, :, None] == jnp.arange(N_EXPERTS, dtype=jnp.int32)).any(axis=1).astype(jnp.int32)
    incl = jnp.cumsum(member, axis=0)
    counts = incl[-1]
    rank = jnp.take_along_axis(incl - member, expert_id, axis=1)
    padded = (counts + DB - 1) // DB * DB
    pend = jnp.cumsum(padded)
    pstart = pend - padded
    dest = (pstart[expert_id] + rank).astype(jnp.int32)
    P = ((2 * T + DB - 1) // DB) * DB + N_EXPERTS * DB
    nblk = P // DB
    block_e = jnp.minimum(jnp.searchsorted(pend, jnp.arange(nblk, dtype=jnp.int32) * DB, side='right'),
                          N_EXPERTS - 1).astype(jnp.int32)
    n_used = (pend[-1:] // DB).astype(jnp.int32)
    return dest.reshape(-1), w_top, block_e, n_used, P


def kernel(x, c, w_ada, b_ada, w_in, w_branch_gate, b_branch_gate, attn_sinks, lambda_q1, lambda_k1,
           lambda_q2, lambda_k2, subln_g, w_branch, w_out, ln1_g, ln1_b, w_router_group, b_router_group,
           w_router_expert, b_router_expert, w_exp_gate, w_exp_up, w_exp_down, ln2_g, ln2_b):
    B, S, D = x.shape
    assert B == 1 and D == D_MODEL
    depth = w_in.shape[0]
    alpha = ALPHA
    xs = x.reshape(S, D)
    mod = _ada_mod(c, w_ada, b_ada)
    diff_slopes = jnp.exp2(-8.0 * jnp.arange(1, DIFF_HEADS + 1, dtype=F32) / DIFF_HEADS)
    zero_bias = jnp.zeros((1, w_in.shape[2]), F32)
    pad = LANES - N_GROUPS - N_EXPERTS
    w_eg = w_exp_gate.reshape(depth * N_EXPERTS, D, D_EXPERT)
    w_eu = w_exp_up.reshape(depth * N_EXPERTS, D, D_EXPERT)
    w_ed = w_exp_down.reshape(depth * N_EXPERTS, D_EXPERT, D)
    for l in range(depth):
        lambda_init = 0.8 - 0.6 * math.exp(-0.3 * l)
        sh1, sc1, g1, sh2, sc2, g2 = [mod[l, :, n * D:(n + 1) * D] for n in range(6)]
        proj = _mod_matmul(xs, sc1, sh1, w_in, l, zero_bias, sigmoid=False, name="in_proj")
        gates = _mod_matmul(xs, sc1, sh1, w_branch_gate, l, b_branch_gate[l].reshape(1, -1),
                            sigmoid=True, name="branch_gates")
        o_a = _swa_attention(proj, attn_sinks[l])
        o_b = _sb_attention(proj)
        o_c = _diff_attention(proj, diff_slopes, lambda_q1[l], lambda_k1[l], lambda_q2[l], lambda_k2[l],
                              subln_g[l], lambda_init)
        z = _branch_merge(o_a, o_b, o_c, w_branch, l, gates)
        w_router = jnp.pad(jnp.concatenate([w_router_group[l], w_router_expert[l]], axis=1), ((0, 0), (0, pad)))
        b_router = jnp.pad(jnp.concatenate([b_router_group[l], b_router_expert[l]]), (0, pad)).reshape(1, LANES)
        x1, logits = _mixer_out(z, w_out[l].astype(BF16), xs, g1, ln1_g[l].reshape(1, D), ln1_b[l].reshape(1, D),
                                sc2, sh2, w_router, b_router, alpha)
        dest, w_top, block_e, n_used, P = _route(logits)
        xb = _dispatch(x1, dest, P)
        yb = _experts(xb, sc2, sh2, w_eg, w_eu, w_ed, block_e, n_used, l)
        xs = _combine_ln(x1, yb, dest, w_top, g2, ln2_g[l].reshape(1, D), ln2_b[l].reshape(1, D), alpha)
    return xs.reshape(B, S, D)
```

```python
import functools
import math

import jax
import jax.numpy as jnp
from jax import lax
from jax.experimental import pallas as pl
from jax.experimental.pallas import tpu as pltpu

F32 = jnp.float32
BF16 = jnp.bfloat16

D_MODEL = 2048
HEAD_DIM = 64
SWA_Q_HEADS = 16
SWA_KV_HEADS = 4
WINDOW = 128
SB_HEADS = 16
DIFF_HEADS = 8
BLOCK_Q = 128
BRANCH_WIDTH = 1024
A_KV = 256
N_BRANCHES = 3
N_GROUPS = 8
EXPERTS_PER_GROUP = 8
N_EXPERTS = 64
D_EXPERT = 384
DISPATCH_BLOCK = 128
DEPTH = 4
ALPHA = (2.0 * DEPTH) ** 0.25
LN_EPS = 1e-5
RMS_EPS = 1e-5
QK_SCALE = 1.0 / math.sqrt(HEAD_DIM)

COL_QA, COL_KA, COL_VA = 0, 1024, 1280
COL_QB, COL_KB, COL_VB = 1536, 2560, 3584
COL_QC, COL_KC, COL_VC = 4608, 5632, 6656

LANES = 128
VMEM_LIMIT = 56 * 1024 * 1024
NEG = -1e30
SB_EXIT = 88.0
ALIBI_CUT = 100.0
POS_SPLIT = 64
ROW_UNROLL = 8


def _params(sem):
    return pltpu.CompilerParams(dimension_semantics=sem, vmem_limit_bytes=VMEM_LIMIT)


def _ada_body(c_ref, w_ref, b_ref, o_ref):
    o_ref[0] = jnp.sum(c_ref[...] * w_ref[0], axis=0, keepdims=True) + b_ref[0]


def _ada_mod(c, w_ada, b_ada):
    L, D, N = w_ada.shape
    tn = 512
    return pl.pallas_call(
        _ada_body,
        grid=(L, N // tn),
        in_specs=[pl.BlockSpec((D, 1), lambda l, j: (0, 0)),
                  pl.BlockSpec((1, D, tn), lambda l, j: (l, 0, j)),
                  pl.BlockSpec((1, 1, tn), lambda l, j: (l, 0, j))],
        out_specs=pl.BlockSpec((1, 1, tn), lambda l, j: (l, 0, j)),
        out_shape=jax.ShapeDtypeStruct((L, 1, N), F32),
        compiler_params=_params(("arbitrary", "arbitrary")),
        name="ada_mod",
    )(c.reshape(D, 1), w_ada, b_ada.reshape(L, 1, N))


def _modmm_body(x_ref, sc_ref, sh_ref, w_ref, b_ref, o_ref, h_ref, *, sigmoid):
    @pl.when(pl.program_id(1) == 0)
    def _():
        h_ref[...] = (x_ref[...] * (1.0 + sc_ref[...]) + sh_ref[...]).astype(BF16)

    acc = jnp.dot(h_ref[...], w_ref[0].astype(BF16), preferred_element_type=F32) + b_ref[...]
    if sigmoid:
        acc = 1.0 / (1.0 + jnp.exp(-acc))
    o_ref[...] = acc.astype(o_ref.dtype)


def _mod_matmul(x, sc, sh, w, layer, b, *, sigmoid, name):
    M, K = x.shape
    N = w.shape[2]
    tm = min(1024, M)
    tn = 512
    return pl.pallas_call(
        functools.partial(_modmm_body, sigmoid=sigmoid),
        grid=(M // tm, N // tn),
        in_specs=[pl.BlockSpec((tm, K), lambda i, j: (i, 0)),
                  pl.BlockSpec((1, K), lambda i, j: (0, 0)),
                  pl.BlockSpec((1, K), lambda i, j: (0, 0)),
                  pl.BlockSpec((1, K, tn), lambda i, j: (layer, 0, j)),
                  pl.BlockSpec((1, tn), lambda i, j: (0, j))],
        out_specs=pl.BlockSpec((tm, tn), lambda i, j: (i, j)),
        out_shape=jax.ShapeDtypeStruct((M, N), BF16),
        scratch_shapes=[pltpu.VMEM((tm, K), BF16)],
        compiler_params=_params(("arbitrary", "arbitrary")),
        name=name,
    )(x, sc, sh, w, b)


def _swa_body(sinks_ref, q_ref, kp_ref, kc_ref, vp_ref, vc_ref, o_ref):
    i = pl.program_id(0)
    k = jnp.concatenate([kp_ref[...], kc_ref[...]], axis=0)
    v = jnp.concatenate([vp_ref[...], vc_ref[...]], axis=0)
    qi = lax.broadcasted_iota(jnp.int32, (BLOCK_Q, 2 * BLOCK_Q), 0)
    ki = lax.broadcasted_iota(jnp.int32, (BLOCK_Q, 2 * BLOCK_Q), 1)
    dist = qi + BLOCK_Q - ki
    valid = (dist >= 0) & (dist < WINDOW) & ((ki >= BLOCK_Q) | (i > 0))
    distf = dist.astype(F32)
    group = SWA_Q_HEADS // SWA_KV_HEADS
    outs = []
    for h in range(SWA_Q_HEADS):
        g = h // group
        slope = 2.0 ** (-8.0 * (h + 1) / SWA_Q_HEADS)
        qh = q_ref[:, h * HEAD_DIM:(h + 1) * HEAD_DIM]
        kg = k[:, g * HEAD_DIM:(g + 1) * HEAD_DIM]
        vg = v[:, g * HEAD_DIM:(g + 1) * HEAD_DIM]
        s = lax.dot_general(qh, kg, (((1,), (1,)), ((), ())), preferred_element_type=F32) * QK_SCALE
        s = jnp.where(valid, s - slope * distf, NEG)
        sink = sinks_ref[h]
        m = jnp.maximum(jnp.max(s, axis=1, keepdims=True), sink)
        p = jnp.where(valid, jnp.exp(s - m), 0.0)
        denom = jnp.sum(p, axis=1, keepdims=True) + jnp.exp(sink - m)
        o = jnp.dot(p.astype(BF16), vg, preferred_element_type=F32) / denom
        outs.append(o)
    o_ref[...] = jnp.concatenate(outs, axis=1).astype(o_ref.dtype)


def _swa_attention(proj, sinks):
    S = proj.shape[0]
    nb = S // BLOCK_Q
    kv_blk = lambda col: pl.BlockSpec((BLOCK_Q, A_KV), lambda i: (i, col // A_KV))
    kv_prev = lambda col: pl.BlockSpec((BLOCK_Q, A_KV), lambda i: (jnp.maximum(i - 1, 0), col // A_KV))
    return pl.pallas_call(
        _swa_body,
        grid=(nb,),
        in_specs=[pl.BlockSpec(memory_space=pltpu.SMEM),
                  pl.BlockSpec((BLOCK_Q, BRANCH_WIDTH), lambda i: (i, COL_QA // BRANCH_WIDTH)),
                  kv_prev(COL_KA), kv_blk(COL_KA), kv_prev(COL_VA), kv_blk(COL_VA)],
        out_specs=pl.BlockSpec((BLOCK_Q, BRANCH_WIDTH), lambda i: (i, 0)),
        out_shape=jax.ShapeDtypeStruct((S, BRANCH_WIDTH), BF16),
        compiler_params=_params(("arbitrary",)),
        name="swa_attention",
    )(sinks, proj, proj, proj, proj, proj)


def _sb_body(q_ref, k_ref, v_ref, o_ref, vt_ref, later_ref, carry_ref, acc_ref, *, tq):
    i = pl.program_id(1)
    nkb = vt_ref.shape[0]
    lane = lax.broadcasted_iota(jnp.int32, (1, LANES), 1)

    @pl.when(i == 0)
    def _():
        def setup(c, carry):
            off = pl.multiple_of(c * tq, tq)
            vt_ref[c] = v_ref[pl.ds(off, tq), :].astype(F32).T.astype(BF16)
            return carry

        lax.fori_loop(0, nkb, setup, 0)
        this_key = lax.broadcasted_iota(jnp.int32, later_ref.shape, 0)
        other_key = lax.broadcasted_iota(jnp.int32, later_ref.shape, 1)
        later_ref[...] = (other_key > this_key).astype(BF16)

    qs = q_ref[...] * QK_SCALE
    qm = [jnp.where((lane >= HEAD_DIM) if half else (lane < HEAD_DIM), qs, jnp.zeros_like(qs))
          for half in range(2)]
    acc_ref[...] = jnp.zeros_like(acc_ref)

    def process(first, nb, masked, carries):
        n = nb * tq
        off = pl.multiple_of(first * tq, tq)
        kb = k_ref[pl.ds(off, n), :]
        zs = [lax.dot_general(kb, qm[half], (((1,), (1,)), ((), ())), preferred_element_type=F32)
              for half in range(2)]
        later = later_ref[0:n, 0:n]
        if masked:
            k_row = lax.broadcasted_iota(jnp.int32, (n, tq), 0)
            q_col = lax.broadcasted_iota(jnp.int32, (n, tq), 1)
            before = k_row < q_col + (nb - 1) * tq
        log_rems, splits = [], []
        for half in range(2):
            z = zs[half]
            soft = jnp.log(1.0 + jnp.exp(-jnp.abs(z)))
            log_rem = -(jnp.maximum(z, 0.0) + soft)
            if masked:
                log_rem = jnp.where(before, log_rem, 0.0)
            hi = log_rem.astype(BF16)
            lo = (log_rem - hi.astype(F32)).astype(BF16)
            log_rems.append(log_rem)
            splits.append((hi, lo))
        afters = [jnp.dot(later, hi, preferred_element_type=F32) + jnp.dot(later, lo, preferred_element_type=F32)
                  for hi, lo in splits]
        new_carries = []
        for half in range(2):
            log_w = (log_rems[half] + zs[half]) + (afters[half] + carries[half])
            w = jnp.exp(log_w)
            if masked:
                w = jnp.where(before, w, 0.0)
            w = w.astype(BF16)
            rows = slice(half * HEAD_DIM, (half + 1) * HEAD_DIM)
            for b in range(nb):
                acc_ref[half] += jnp.dot(vt_ref[first + b][rows, :], w[b * tq:(b + 1) * tq, :],
                                         preferred_element_type=F32)
            new_carries.append(carries[half] + afters[half][0:1, :] + log_rems[half][0:1, :])
        return tuple(new_carries)

    zero = jnp.zeros((1, tq), F32)

    @pl.when(i == 0)
    def _():
        carries = process(0, 1, True, (zero, zero))
        carry_ref[0], carry_ref[1] = carries

    @pl.when(i > 0)
    def _():
        carries = process(i - 1, 2, True, (zero, zero))
        carry_ref[0], carry_ref[1] = carries

    carries = (carry_ref[0], carry_ref[1])

    def top(carries):
        return jnp.maximum(jnp.max(carries[0]), jnp.max(carries[1]))

    def cond(state):
        j, _, best = state
        return (j >= 0) & (best > -SB_EXIT)

    def body(state):
        j, carries, _ = state
        carries = process(j, 1, False, carries)
        return j - 1, carries, top(carries)

    lax.while_loop(cond, body, (i - 2, carries, top(carries)))
    o_ref[...] = jnp.concatenate([acc_ref[0], acc_ref[1]], axis=0).T.astype(o_ref.dtype)


def _sb_attention(proj):
    S = proj.shape[0]
    tq = min(256, S)
    npair = BRANCH_WIDTH // LANES
    return pl.pallas_call(
        functools.partial(_sb_body, tq=tq),
        grid=(npair, S // tq),
        in_specs=[pl.BlockSpec((tq, LANES), lambda p, i: (i, COL_QB // LANES + p)),
                  pl.BlockSpec((S, LANES), lambda p, i: (0, COL_KB // LANES + p)),
                  pl.BlockSpec((S, LANES), lambda p, i: (0, COL_VB // LANES + p))],
        out_specs=pl.BlockSpec((tq, LANES), lambda p, i: (i, p)),
        out_shape=jax.ShapeDtypeStruct((S, BRANCH_WIDTH), BF16),
        scratch_shapes=[pltpu.VMEM((S // tq, LANES, tq), BF16),
                        pltpu.VMEM((2 * tq, 2 * tq), BF16),
                        pltpu.VMEM((2, 1, tq), F32),
                        pltpu.VMEM((2, HEAD_DIM, tq), F32)],
        compiler_params=_params(("arbitrary", "arbitrary")),
        name="stickbreak_attention",
    )(proj, proj, proj)


def _diff_body(slopes_ref, q_ref, k_ref, v_ref, lq1_ref, lk1_ref, lq2_ref, lk2_ref, g_ref, o_ref,
               kaug_ref, vt_ref, knorm_ref, qa_ref, sta_ref, stb_ref, m_ref, l_ref, acc_ref, *, tq, lambda_init):
    h = pl.program_id(0)
    i = pl.program_id(1)
    slope = slopes_ref[0, h]
    inv_slope = slopes_ref[1, h]
    nkb = kaug_ref.shape[0] // tq
    lane = lax.broadcasted_iota(jnp.int32, (1, LANES), 1)

    def max_sq_norms(x):
        sq = x.astype(F32)
        sq = sq * sq
        first = jnp.sum(jnp.where(lane < HEAD_DIM, sq, 0.0), axis=1, keepdims=True)
        second = jnp.sum(jnp.where(lane >= HEAD_DIM, sq, 0.0), axis=1, keepdims=True)
        return jnp.max(first, axis=0, keepdims=True), jnp.max(second, axis=0, keepdims=True)

    @pl.when(i == 0)
    def _():
        def setup(c, carry):
            off = pl.multiple_of(c * tq, tq)
            n1, n2 = max_sq_norms(k_ref[pl.ds(off, tq), :])
            carry = (jnp.maximum(carry[0], n1), jnp.maximum(carry[1], n2))
            pos = off + lax.broadcasted_iota(jnp.int32, (tq, LANES), 0)
            lanes = lax.broadcasted_iota(jnp.int32, (tq, LANES), 1)
            coarse = (pos // POS_SPLIT) * POS_SPLIT
            posm = jnp.where(lanes == 0, coarse, jnp.where(lanes == 1, pos - coarse, 0))
            kaug_ref[pl.ds(off, tq), 0:LANES] = k_ref[pl.ds(off, tq), :]
            kaug_ref[pl.ds(off, tq), LANES:2 * LANES] = posm.astype(F32).astype(BF16)
            vt_ref[c] = v_ref[pl.ds(off, tq), :].astype(F32).T.astype(BF16)
            return carry

        zero = jnp.zeros((1, 1), F32)
        knorm_ref[0], knorm_ref[1] = lax.fori_loop(0, nkb, setup, (zero, zero))

    q1n, q2n = max_sq_norms(q_ref[...])
    qk = jnp.sqrt(jnp.maximum(q1n * knorm_ref[0], q2n * knorm_ref[1]))
    reach = (ALIBI_CUT + 2.0 * QK_SCALE * qk) * inv_slope
    keep = jnp.minimum(jnp.floor((reach - 1.0) * (1.0 / tq)) + 1.0, 1e6).astype(jnp.int32)
    n_off = jnp.clip(jnp.max(keep), 0, i)
    j0 = i - n_off

    qs = q_ref[...] * QK_SCALE
    bias_cols = jnp.broadcast_to(jnp.where(lane < 2, slope, 0.0).astype(BF16), (tq, LANES))
    for half in range(2):
        in_half = (lane >= HEAD_DIM) if half else (lane < HEAD_DIM)
        qa_ref[half] = jnp.concatenate([jnp.where(in_half, qs, jnp.zeros_like(qs)), bias_cols], axis=1)
    acc_ref[...] = jnp.zeros_like(acc_ref)
    m_ref[...] = jnp.full_like(m_ref, NEG)
    l_ref[...] = jnp.zeros_like(l_ref)
    on_or_below_diagonal = (lax.broadcasted_iota(jnp.int32, (tq, tq), 0)
                            <= lax.broadcasted_iota(jnp.int32, (tq, tq), 1))

    def scores(jb, st_ref):
        off = pl.multiple_of(jb * tq, tq)
        kb = kaug_ref[pl.ds(off, tq), :]
        for half in range(2):
            st_ref[half] = lax.dot_general(kb, qa_ref[half], (((1,), (1,)), ((), ())),
                                           preferred_element_type=F32)

    def softmax_pv(jb, st_ref, diagonal):
        vtb = vt_ref[jb]
        alphas, ps = [], []
        for half in range(2):
            st = st_ref[half]
            if diagonal:
                st = jnp.where(on_or_below_diagonal, st, NEG)
            m_old = m_ref[half]
            m_new = jnp.maximum(m_old, jnp.max(st, axis=0, keepdims=True))
            alpha = jnp.exp(m_old - m_new)
            p = jnp.exp(st - m_new)
            l_ref[half] = alpha * l_ref[half] + jnp.sum(p, axis=0, keepdims=True)
            m_ref[half] = m_new
            alphas.append(alpha)
            ps.append(p.astype(BF16))
        for half in range(2):
            acc_ref[half] = alphas[half] * acc_ref[half] + jnp.dot(vtb, ps[half], preferred_element_type=F32)

    scores(j0, sta_ref)

    def pair(p, carry):
        scores(j0 + 2 * p + 1, stb_ref)
        softmax_pv(j0 + 2 * p, sta_ref, False)
        scores(j0 + 2 * p + 2, sta_ref)
        softmax_pv(j0 + 2 * p + 1, stb_ref, False)
        return carry

    lax.fori_loop(0, n_off // 2, pair, 0)

    @pl.when(n_off % 2 == 0)
    def _():
        softmax_pv(i, sta_ref, True)

    @pl.when(n_off % 2 == 1)
    def _():
        scores(i, stb_ref)
        softmax_pv(i - 1, sta_ref, False)
        softmax_pv(i, stb_ref, True)

    lam = (jnp.exp(jnp.sum(lq1_ref[...] * lk1_ref[...], axis=1, keepdims=True))
           - jnp.exp(jnp.sum(lq2_ref[...] * lk2_ref[...], axis=1, keepdims=True)) + lambda_init)
    o = acc_ref[0] / l_ref[0] - lam * (acc_ref[1] / l_ref[1])
    y = o * lax.rsqrt(jnp.mean(o * o, axis=0, keepdims=True) + RMS_EPS) * g_ref[...]
    o_ref[...] = (y * (1.0 - lambda_init)).T.astype(o_ref.dtype)


def _diff_attention(proj, slopes, lq1, lk1, lq2, lk2, subln_g, lambda_init):
    S = proj.shape[0]
    tq = min(512, S)
    vec = lambda n: pl.BlockSpec((1, n), lambda h, i: (0, 0))
    return pl.pallas_call(
        functools.partial(_diff_body, tq=tq, lambda_init=lambda_init),
        grid=(DIFF_HEADS, S // tq),
        in_specs=[pl.BlockSpec(memory_space=pltpu.SMEM),
                  pl.BlockSpec((tq, LANES), lambda h, i: (i, COL_QC // LANES + h)),
                  pl.BlockSpec((S, LANES), lambda h, i: (0, COL_KC // LANES + h)),
                  pl.BlockSpec((S, LANES), lambda h, i: (0, COL_VC // LANES + h)),
                  vec(HEAD_DIM), vec(HEAD_DIM), vec(HEAD_DIM), vec(HEAD_DIM),
                  pl.BlockSpec((2 * HEAD_DIM, 1), lambda h, i: (0, 0))],
        out_specs=pl.BlockSpec((tq, LANES), lambda h, i: (i, h)),
        out_shape=jax.ShapeDtypeStruct((S, BRANCH_WIDTH), BF16),
        scratch_shapes=[pltpu.VMEM((S, 2 * LANES), BF16),
                        pltpu.VMEM((S // tq, LANES, tq), BF16),
                        pltpu.VMEM((2, 1, 1), F32),
                        pltpu.VMEM((2, tq, 2 * LANES), BF16),
                        pltpu.VMEM((2, tq, tq), F32), pltpu.VMEM((2, tq, tq), F32),
                        pltpu.VMEM((2, 1, tq), F32), pltpu.VMEM((2, 1, tq), F32),
                        pltpu.VMEM((2, LANES, tq), F32)],
        compiler_params=_params(("arbitrary", "arbitrary")),
        name="diff_attention",
    )(slopes, proj, proj, proj, lq1.reshape(1, -1), lk1.reshape(1, -1), lq2.reshape(1, -1),
      lk2.reshape(1, -1), subln_g.reshape(-1, 1))


def _branch_body(oa_ref, ob_ref, oc_ref, w_ref, ga_ref, gb_ref, gc_ref, z_ref):
    z = None
    for n, (o_ref, g_ref) in enumerate(((oa_ref, ga_ref), (ob_ref, gb_ref), (oc_ref, gc_ref))):
        y = jnp.dot(o_ref[...], w_ref[0, n].astype(BF16), preferred_element_type=F32)
        t = g_ref[...].astype(F32) * y
        z = t if z is None else z + t
    z_ref[...] = z.astype(z_ref.dtype)


def _branch_merge(o_a, o_b, o_c, w_branch, layer, gates):
    S = o_a.shape[0]
    D = w_branch.shape[3]
    tm = min(1024, S)
    tn = 512
    nj = D // tn
    o_spec = pl.BlockSpec((tm, BRANCH_WIDTH), lambda i, j: (i, 0))
    gate_spec = lambda n: pl.BlockSpec((tm, tn), lambda i, j: (i, n * nj + j))
    return pl.pallas_call(
        _branch_body,
        grid=(S // tm, nj),
        in_specs=[o_spec, o_spec, o_spec,
                  pl.BlockSpec((1, N_BRANCHES, BRANCH_WIDTH, tn), lambda i, j: (layer, 0, 0, j)),
                  gate_spec(0), gate_spec(1), gate_spec(2)],
        out_specs=pl.BlockSpec((tm, tn), lambda i, j: (i, j)),
        out_shape=jax.ShapeDtypeStruct((S, D), BF16),
        compiler_params=_params(("arbitrary", "arbitrary")),
        name="branch_merge",
    )(o_a, o_b, o_c, w_branch, gates, gates, gates)


def _layer_norm(r, g, b):
    mu = jnp.mean(r, axis=1, keepdims=True)
    d = r - mu
    var = jnp.mean(d * d, axis=1, keepdims=True)
    return d * lax.rsqrt(var + LN_EPS) * g + b


def _split_bf16(a):
    hi = a.astype(BF16)
    return hi, (a - hi.astype(F32)).astype(BF16)


def _out_body(z_ref, w_ref, x_ref, g1_ref, lng_ref, lnb_ref, sc2_ref, sh2_ref, wr_hi_ref, wr_lo_ref, br_ref,
              x1_ref, logit_ref, *, alpha):
    y = jnp.dot(z_ref[...], w_ref[...], preferred_element_type=F32)
    x1 = _layer_norm(alpha * x_ref[...] + g1_ref[...] * y, lng_ref[...], lnb_ref[...])
    x1_ref[...] = x1
    h_hi, h_lo = _split_bf16(x1 * (1.0 + sc2_ref[...]) + sh2_ref[...])
    w_hi = wr_hi_ref[...]
    logit_ref[...] = (jnp.dot(h_hi, w_hi, preferred_element_type=F32)
                      + (jnp.dot(h_hi, wr_lo_ref[...], preferred_element_type=F32)
                         + jnp.dot(h_lo, w_hi, preferred_element_type=F32))) + br_ref[...]


def _mixer_out(z, w_out, x, g1, ln_g, ln_b, sc2, sh2, w_router, b_router, alpha):
    S, D = x.shape
    tm = min(256, S)
    row = pl.BlockSpec((tm, D), lambda i: (i, 0))
    vec = pl.BlockSpec((1, D), lambda i: (0, 0))
    router = pl.BlockSpec((D, LANES), lambda i: (0, 0))
    wr_hi, wr_lo = _split_bf16(w_router)
    return pl.pallas_call(
        functools.partial(_out_body, alpha=alpha),
        grid=(S // tm,),
        in_specs=[row, pl.BlockSpec((D, D), lambda i: (0, 0)), row, vec, vec, vec, vec, vec,
                  router, router, pl.BlockSpec((1, LANES), lambda i: (0, 0))],
        out_specs=[row, pl.BlockSpec((tm, LANES), lambda i: (i, 0))],
        out_shape=[jax.ShapeDtypeStruct((S, D), F32), jax.ShapeDtypeStruct((S, LANES), F32)],
        compiler_params=_params(("arbitrary",)),
        name="mixer_out_ln",
    )(z, w_out, x, g1, ln_g, ln_b, sc2, sh2, wr_hi, wr_lo, b_router)


def _dispatch_body(dest_ref, x_ref, xb_in_hbm, xb_hbm, sem, *, rows_per_step):
    del xb_in_hbm
    i = pl.program_id(0)

    def row_copy(r, d):
        return pltpu.make_async_copy(x_ref.at[pl.ds(r, 1)], xb_hbm.at[pl.ds(d, 1)], sem)

    def issue(g, carry):
        for u in range(ROW_UNROLL):
            r = g * ROW_UNROLL + u
            t = i * rows_per_step + r
            for k in range(2):
                row_copy(r, dest_ref[2 * t + k]).start()
        return carry

    def wait(g, carry):
        for _ in range(2 * ROW_UNROLL):
            row_copy(0, 0).wait()
        return carry

    lax.fori_loop(0, rows_per_step // ROW_UNROLL, issue, 0)
    lax.fori_loop(0, rows_per_step // ROW_UNROLL, wait, 0)


def _dispatch(x1, dest, padded_rows):
    T, D = x1.shape
    rows_per_step = min(512, T)
    xb0 = jnp.zeros((padded_rows, D), F32)
    return pl.pallas_call(
        functools.partial(_dispatch_body, rows_per_step=rows_per_step),
        grid_spec=pltpu.PrefetchScalarGridSpec(
            num_scalar_prefetch=1,
            grid=(T // rows_per_step,),
            in_specs=[pl.BlockSpec((rows_per_step, D), lambda i, d: (i, 0)),
                      pl.BlockSpec(memory_space=pl.ANY)],
            out_specs=pl.BlockSpec(memory_space=pl.ANY),
            scratch_shapes=[pltpu.SemaphoreType.DMA(())]),
        out_shape=jax.ShapeDtypeStruct((padded_rows, D), F32),
        input_output_aliases={2: 0},
        compiler_params=_params(("arbitrary",)),
        name="moe_dispatch",
    )(dest, x1, xb0)


def _expert_body(be_ref, nu_ref, xb_ref, sc_ref, sh_ref, wg_ref, wu_ref, wd_ref, yb_ref,
                 wg_s, wu_s, wd_s):
    b = pl.program_id(0)
    nu = nu_ref[0]

    @pl.when(b < nu)
    def _():
        e = be_ref[b]
        prev = be_ref[jnp.maximum(b - 1, 0)]

        @pl.when((b == 0) | (e != prev))
        def _():
            wg_s[...] = wg_ref[0].astype(BF16)
            wu_s[...] = wu_ref[0].astype(BF16)
            wd_s[...] = wd_ref[0].astype(BF16)

        h = (xb_ref[...] * (1.0 + sc_ref[...]) + sh_ref[...]).astype(BF16)
        a = jnp.dot(h, wg_s[...], preferred_element_type=F32)
        u = jnp.dot(h, wu_s[...], preferred_element_type=F32)
        act = (a / (1.0 + jnp.exp(-a))) * u
        yb_ref[...] = jnp.dot(act.astype(BF16), wd_s[...], preferred_element_type=F32)

    @pl.when(b >= nu)
    def _():
        yb_ref[...] = jnp.zeros_like(yb_ref)


def _experts(xb, sc2, sh2, w_g, w_u, w_d, block_e, n_used, layer):
    P, D = xb.shape
    DB = DISPATCH_BLOCK
    nblk = P // DB
    blk = lambda b, be, nu: jnp.minimum(b, nu[0] - 1)
    wsel = lambda b, be, nu: (layer * N_EXPERTS + be[blk(b, be, nu)], 0, 0)
    return pl.pallas_call(
        _expert_body,
        grid_spec=pltpu.PrefetchScalarGridSpec(
            num_scalar_prefetch=2,
            grid=(nblk,),
            in_specs=[pl.BlockSpec((DB, D), lambda b, be, nu: (blk(b, be, nu), 0)),
                      pl.BlockSpec((1, D), lambda b, be, nu: (0, 0)),
                      pl.BlockSpec((1, D), lambda b, be, nu: (0, 0)),
                      pl.BlockSpec((1, D, D_EXPERT), wsel),
                      pl.BlockSpec((1, D, D_EXPERT), wsel),
                      pl.BlockSpec((1, D_EXPERT, D), wsel)],
            out_specs=pl.BlockSpec((DB, D), lambda b, be, nu: (b, 0)),
            scratch_shapes=[pltpu.VMEM((D, D_EXPERT), BF16), pltpu.VMEM((D, D_EXPERT), BF16),
                            pltpu.VMEM((D_EXPERT, D), BF16)]),
        out_shape=jax.ShapeDtypeStruct((P, D), F32),
        compiler_params=_params(("arbitrary",)),
        name="moe_experts",
    )(block_e, n_used, xb, sc2, sh2, w_g, w_u, w_d)


def _combine_body(dest_ref, x1_ref, wt_ref, g2_ref, lng_ref, lnb_ref, yb_hbm, o_ref,
                  buf0, buf1, sem, *, tm, alpha):
    i = pl.program_id(0)
    bufs = (buf0, buf1)

    def row_copy(d, k, r):
        return pltpu.make_async_copy(yb_hbm.at[pl.ds(d, 1)], bufs[k].at[pl.ds(r, 1)], sem)

    def issue(g, carry):
        for u in range(ROW_UNROLL):
            r = g * ROW_UNROLL + u
            t = i * tm + r
            for k in range(2):
                row_copy(dest_ref[2 * t + k], k, r).start()
        return carry

    def wait(g, carry):
        for _ in range(ROW_UNROLL):
            for k in range(2):
                row_copy(0, k, 0).wait()
        return carry

    lax.fori_loop(0, tm // ROW_UNROLL, issue, 0)
    lax.fori_loop(0, tm // ROW_UNROLL, wait, 0)
    wt = wt_ref[...]
    y = buf0[...] * wt[:, 0:1] + buf1[...] * wt[:, 1:2]
    o_ref[...] = _layer_norm(alpha * x1_ref[...] + g2_ref[...] * y, lng_ref[...], lnb_ref[...])


def _combine_ln(x1, yb, dest, w_top, g2, ln_g, ln_b, alpha):
    T, D = x1.shape
    tm = min(256, T)
    row = pl.BlockSpec((tm, D), lambda i, d: (i, 0))
    vec = pl.BlockSpec((1, D), lambda i, d: (0, 0))
    return pl.pallas_call(
        functools.partial(_combine_body, tm=tm, alpha=alpha),
        grid_spec=pltpu.PrefetchScalarGridSpec(
            num_scalar_prefetch=1,
            grid=(T // tm,),
            in_specs=[row, pl.BlockSpec((tm, 2), lambda i, d: (i, 0)), vec, vec, vec,
                      pl.BlockSpec(memory_space=pl.ANY)],
            out_specs=row,
            scratch_shapes=[pltpu.VMEM((tm, D), F32), pltpu.VMEM((tm, D), F32),
                            pltpu.SemaphoreType.DMA(())]),
        out_shape=jax.ShapeDtypeStruct((T, D), F32),
        compiler_params=_params(("arbitrary",)),
        name="moe_combine_ln",
    )(dest, x1, w_top, g2, ln_g, ln_b, yb)


def _route(logits):
    T = logits.shape[0]
    DB = DISPATCH_BLOCK
    group_logits = logits[:, :N_GROUPS]
    exp_logits = logits[:, N_GROUPS:N_GROUPS + N_EXPERTS].reshape(T, N_GROUPS, EXPERTS_PER_GROUP)
    group_prob = jax.nn.softmax(group_logits, axis=-1)
    g_top = jnp.argmax(group_logits, axis=-1)
    p_group = jnp.take_along_axis(group_prob, g_top[:, None], axis=-1)
    in_group = jnp.take_along_axis(exp_logits, g_top[:, None, None], axis=1)[:, 0]
    top_val, top_idx = lax.top_k(in_group, 2)
    w_top = jax.nn.softmax(top_val, axis=-1) * p_group
    expert_id = g_top[:, None].astype(jnp.int32) * EXPERTS_PER_GROUP + top_idx.astype(jnp.int32)

    member = (expert_id[:, :, None] == jnp.arange(N_EXPERTS, dtype=jnp.int32)).any(axis=1).astype(jnp.int32)
    incl = jnp.cumsum(member, axis=0)
    counts = incl[-1]
    rank = jnp.take_along_axis(incl - member, expert_id, axis=1)
    padded = (counts + DB - 1) // DB * DB
    pend = jnp.cumsum(padded)
    pstart = pend - padded
    dest = (pstart[expert_id] + rank).astype(jnp.int32)
    P = ((2 * T + DB - 1) // DB) * DB + N_EXPERTS * DB
    nblk = P // DB
    block_e = jnp.minimum(jnp.searchsorted(pend, jnp.arange(nblk, dtype=jnp.int32) * DB, side='right'),
                          N_EXPERTS - 1).astype(jnp.int32)
    n_used = (pend[-1:] // DB).astype(jnp.int32)
    return dest.reshape(-1), w_top, block_e, n_used, P


def kernel(x, c, w_ada, b_ada, w_in, w_branch_gate, b_branch_gate, attn_sinks, lambda_q1, lambda_k1,
           lambda_q2, lambda_k2, subln_g, w_branch, w_out, ln1_g, ln1_b, w_router_group, b_router_group,
           w_router_expert, b_router_expert, w_exp_gate, w_exp_up, w_exp_down, ln2_g, ln2_b):
    B, S, D = x.shape
    assert B == 1 and D == D_MODEL
    depth = w_in.shape[0]
    alpha = ALPHA
    xs = x.reshape(S, D)
    mod = _ada_mod(c, w_ada, b_ada)
    diff_slopes = jnp.exp2(-8.0 * jnp.arange(1, DIFF_HEADS + 1, dtype=F32) / DIFF_HEADS)
    diff_slopes = jnp.stack([diff_slopes, 1.0 / diff_slopes])
    zero_bias = jnp.zeros((1, w_in.shape[2]), F32)
    pad = LANES - N_GROUPS - N_EXPERTS
    w_eg = w_exp_gate.reshape(depth * N_EXPERTS, D, D_EXPERT)
    w_eu = w_exp_up.reshape(depth * N_EXPERTS, D, D_EXPERT)
    w_ed = w_exp_down.reshape(depth * N_EXPERTS, D_EXPERT, D)
    for l in range(depth):
        lambda_init = 0.8 - 0.6 * math.exp(-0.3 * l)
        sh1, sc1, g1, sh2, sc2, g2 = [mod[l, :, n * D:(n + 1) * D] for n in range(6)]
        proj = _mod_matmul(xs, sc1, sh1, w_in, l, zero_bias, sigmoid=False, name="in_proj")
        gates = _mod_matmul(xs, sc1, sh1, w_branch_gate, l, b_branch_gate[l].reshape(1, -1),
                            sigmoid=True, name="branch_gates")
        o_a = _swa_attention(proj, attn_sinks[l])
        o_b = _sb_attention(proj)
        o_c = _diff_attention(proj, diff_slopes, lambda_q1[l], lambda_k1[l], lambda_q2[l], lambda_k2[l],
                              subln_g[l], lambda_init)
        z = _branch_merge(o_a, o_b, o_c, w_branch, l, gates)
        w_router = jnp.pad(jnp.concatenate([w_router_group[l], w_router_expert[l]], axis=1), ((0, 0), (0, pad)))
        b_router = jnp.pad(jnp.concatenate([b_router_group[l], b_router_expert[l]]), (0, pad)).reshape(1, LANES)
        x1, logits = _mixer_out(z, w_out[l].astype(BF16), xs, g1, ln1_g[l].reshape(1, D), ln1_b[l].reshape(1, D),
                                sc2, sh2, w_router, b_router, alpha)
        dest, w_top, block_e, n_used, P = _route(logits)
        xb = _dispatch(x1, dest, P)
        yb = _experts(xb, sc2, sh2, w_eg, w_eu, w_ed, block_e, n_used, l)
        xs = _combine_ln(x1, yb, dest, w_top, g2, ln2_g[l].reshape(1, D), ln2_b[l].reshape(1, D), alpha)
    return xs.reshape(B, S, D)
```

```python
import functools
import math

import jax
import jax.numpy as jnp
from jax import lax
from jax.experimental import pallas as pl
from jax.experimental.pallas import tpu as pltpu

F32 = jnp.float32
BF16 = jnp.bfloat16

D_MODEL = 2048
HEAD_DIM = 64
SWA_Q_HEADS = 16
SWA_KV_HEADS = 4
WINDOW = 128
SB_HEADS = 16
DIFF_HEADS = 8
BLOCK_Q = 128
BRANCH_WIDTH = 1024
A_KV = 256
N_BRANCHES = 3
N_GROUPS = 8
EXPERTS_PER_GROUP = 8
N_EXPERTS = 64
D_EXPERT = 384
DISPATCH_BLOCK = 128
DEPTH = 4
ALPHA = (2.0 * DEPTH) ** 0.25
LN_EPS = 1e-5
RMS_EPS = 1e-5
QK_SCALE = 1.0 / math.sqrt(HEAD_DIM)

COL_QA, COL_KA, COL_VA = 0, 1024, 1280
COL_QB, COL_KB, COL_VB = 1536, 2560, 3584
COL_QC, COL_KC, COL_VC = 4608, 5632, 6656

LANES = 128
VMEM_LIMIT = 56 * 1024 * 1024
NEG = -1e30
SB_EXIT = 88.0
ALIBI_CUT = 100.0
POS_SPLIT = 64
SUM_ROWS = 16
ROW_UNROLL = 8


def _params(sem):
    return pltpu.CompilerParams(dimension_semantics=sem, vmem_limit_bytes=VMEM_LIMIT)


def _ada_body(c_ref, w_ref, b_ref, o_ref):
    o_ref[0] = jnp.sum(c_ref[...] * w_ref[0], axis=0, keepdims=True) + b_ref[0]


def _ada_mod(c, w_ada, b_ada):
    L, D, N = w_ada.shape
    tn = 512
    return pl.pallas_call(
        _ada_body,
        grid=(L, N // tn),
        in_specs=[pl.BlockSpec((D, 1), lambda l, j: (0, 0)),
                  pl.BlockSpec((1, D, tn), lambda l, j: (l, 0, j)),
                  pl.BlockSpec((1, 1, tn), lambda l, j: (l, 0, j))],
        out_specs=pl.BlockSpec((1, 1, tn), lambda l, j: (l, 0, j)),
        out_shape=jax.ShapeDtypeStruct((L, 1, N), F32),
        compiler_params=_params(("arbitrary", "arbitrary")),
        name="ada_mod",
    )(c.reshape(D, 1), w_ada, b_ada.reshape(L, 1, N))


def _modmm_body(x_ref, sc_ref, sh_ref, w_ref, b_ref, o_ref, h_ref, *, sigmoid):
    @pl.when(pl.program_id(1) == 0)
    def _():
        h_ref[...] = (x_ref[...] * (1.0 + sc_ref[...]) + sh_ref[...]).astype(BF16)

    acc = jnp.dot(h_ref[...], w_ref[0].astype(BF16), preferred_element_type=F32) + b_ref[...]
    if sigmoid:
        acc = 1.0 / (1.0 + jnp.exp(-acc))
    o_ref[...] = acc.astype(o_ref.dtype)


def _mod_matmul(x, sc, sh, w, layer, b, *, sigmoid, name):
    M, K = x.shape
    N = w.shape[2]
    tm = min(1024, M)
    tn = 512
    return pl.pallas_call(
        functools.partial(_modmm_body, sigmoid=sigmoid),
        grid=(M // tm, N // tn),
        in_specs=[pl.BlockSpec((tm, K), lambda i, j: (i, 0)),
                  pl.BlockSpec((1, K), lambda i, j: (0, 0)),
                  pl.BlockSpec((1, K), lambda i, j: (0, 0)),
                  pl.BlockSpec((1, K, tn), lambda i, j: (layer, 0, j)),
                  pl.BlockSpec((1, tn), lambda i, j: (0, j))],
        out_specs=pl.BlockSpec((tm, tn), lambda i, j: (i, j)),
        out_shape=jax.ShapeDtypeStruct((M, N), BF16),
        scratch_shapes=[pltpu.VMEM((tm, K), BF16)],
        compiler_params=_params(("arbitrary", "arbitrary")),
        name=name,
    )(x, sc, sh, w, b)


def _swa_body(sinks_ref, q_ref, kp_ref, kc_ref, vp_ref, vc_ref, o_ref):
    i = pl.program_id(0)
    k = jnp.concatenate([kp_ref[...], kc_ref[...]], axis=0)
    v = jnp.concatenate([vp_ref[...], vc_ref[...]], axis=0)
    qi = lax.broadcasted_iota(jnp.int32, (BLOCK_Q, 2 * BLOCK_Q), 0)
    ki = lax.broadcasted_iota(jnp.int32, (BLOCK_Q, 2 * BLOCK_Q), 1)
    dist = qi + BLOCK_Q - ki
    valid = (dist >= 0) & (dist < WINDOW) & ((ki >= BLOCK_Q) | (i > 0))
    distf = dist.astype(F32)
    group = SWA_Q_HEADS // SWA_KV_HEADS
    outs = []
    for h in range(SWA_Q_HEADS):
        g = h // group
        slope = 2.0 ** (-8.0 * (h + 1) / SWA_Q_HEADS)
        qh = q_ref[:, h * HEAD_DIM:(h + 1) * HEAD_DIM]
        kg = k[:, g * HEAD_DIM:(g + 1) * HEAD_DIM]
        vg = v[:, g * HEAD_DIM:(g + 1) * HEAD_DIM]
        s = lax.dot_general(qh, kg, (((1,), (1,)), ((), ())), preferred_element_type=F32) * QK_SCALE
        s = jnp.where(valid, s - slope * distf, NEG)
        sink = sinks_ref[h]
        m = jnp.maximum(jnp.max(s, axis=1, keepdims=True), sink)
        p = jnp.where(valid, jnp.exp(s - m), 0.0)
        denom = jnp.sum(p, axis=1, keepdims=True) + jnp.exp(sink - m)
        o = jnp.dot(p.astype(BF16), vg, preferred_element_type=F32) / denom
        outs.append(o)
    o_ref[...] = jnp.concatenate(outs, axis=1).astype(o_ref.dtype)


def _swa_attention(proj, sinks):
    S = proj.shape[0]
    nb = S // BLOCK_Q
    kv_blk = lambda col: pl.BlockSpec((BLOCK_Q, A_KV), lambda i: (i, col // A_KV))
    kv_prev = lambda col: pl.BlockSpec((BLOCK_Q, A_KV), lambda i: (jnp.maximum(i - 1, 0), col // A_KV))
    return pl.pallas_call(
        _swa_body,
        grid=(nb,),
        in_specs=[pl.BlockSpec(memory_space=pltpu.SMEM),
                  pl.BlockSpec((BLOCK_Q, BRANCH_WIDTH), lambda i: (i, COL_QA // BRANCH_WIDTH)),
                  kv_prev(COL_KA), kv_blk(COL_KA), kv_prev(COL_VA), kv_blk(COL_VA)],
        out_specs=pl.BlockSpec((BLOCK_Q, BRANCH_WIDTH), lambda i: (i, 0)),
        out_shape=jax.ShapeDtypeStruct((S, BRANCH_WIDTH), BF16),
        compiler_params=_params(("arbitrary",)),
        name="swa_attention",
    )(sinks, proj, proj, proj, proj, proj)


def _sb_body(q_ref, k_ref, v_ref, o_ref, vt_ref, later_ref, carry_ref, acc_ref, *, tq):
    i = pl.program_id(1)
    nkb = vt_ref.shape[0]
    lane = lax.broadcasted_iota(jnp.int32, (1, LANES), 1)

    @pl.when(i == 0)
    def _():
        def setup(c, carry):
            off = pl.multiple_of(c * tq, tq)
            vt_ref[c] = v_ref[pl.ds(off, tq), :].astype(F32).T.astype(BF16)
            return carry

        lax.fori_loop(0, nkb, setup, 0)
        this_key = lax.broadcasted_iota(jnp.int32, later_ref.shape, 0)
        other_key = lax.broadcasted_iota(jnp.int32, later_ref.shape, 1)
        later_ref[...] = (other_key > this_key).astype(BF16)

    qs = q_ref[...] * QK_SCALE
    qm = [jnp.where((lane >= HEAD_DIM) if half else (lane < HEAD_DIM), qs, jnp.zeros_like(qs))
          for half in range(2)]
    acc_ref[...] = jnp.zeros_like(acc_ref)

    def process(first, nb, masked, carries):
        n = nb * tq
        off = pl.multiple_of(first * tq, tq)
        kb = k_ref[pl.ds(off, n), :]
        zs = [lax.dot_general(kb, qm[half], (((1,), (1,)), ((), ())), preferred_element_type=F32)
              for half in range(2)]
        later = later_ref[0:n, 0:n]
        if masked:
            k_row = lax.broadcasted_iota(jnp.int32, (n, tq), 0)
            q_col = lax.broadcasted_iota(jnp.int32, (n, tq), 1)
            before = k_row < q_col + (nb - 1) * tq
        log_rems, splits = [], []
        for half in range(2):
            z = zs[half]
            soft = jnp.log(1.0 + jnp.exp(-jnp.abs(z)))
            log_rem = -(jnp.maximum(z, 0.0) + soft)
            if masked:
                log_rem = jnp.where(before, log_rem, 0.0)
            hi = log_rem.astype(BF16)
            lo = (log_rem - hi.astype(F32)).astype(BF16)
            log_rems.append(log_rem)
            splits.append((hi, lo))
        afters = [jnp.dot(later, hi, preferred_element_type=F32) + jnp.dot(later, lo, preferred_element_type=F32)
                  for hi, lo in splits]
        new_carries = []
        for half in range(2):
            log_w = (log_rems[half] + zs[half]) + (afters[half] + carries[half])
            w = jnp.exp(log_w)
            if masked:
                w = jnp.where(before, w, 0.0)
            w = w.astype(BF16)
            rows = slice(half * HEAD_DIM, (half + 1) * HEAD_DIM)
            for b in range(nb):
                acc_ref[half] += jnp.dot(vt_ref[first + b][rows, :], w[b * tq:(b + 1) * tq, :],
                                         preferred_element_type=F32)
            new_carries.append(carries[half] + afters[half][0:1, :] + log_rems[half][0:1, :])
        return tuple(new_carries)

    zero = jnp.zeros((1, tq), F32)

    @pl.when(i == 0)
    def _():
        carries = process(0, 1, True, (zero, zero))
        carry_ref[0], carry_ref[1] = carries

    @pl.when(i > 0)
    def _():
        carries = process(i - 1, 2, True, (zero, zero))
        carry_ref[0], carry_ref[1] = carries

    carries = (carry_ref[0], carry_ref[1])

    def top(carries):
        return jnp.maximum(jnp.max(carries[0]), jnp.max(carries[1]))

    def cond(state):
        j, _, best = state
        return (j >= 0) & (best > -SB_EXIT)

    def body(state):
        j, carries, _ = state
        carries = process(j, 1, False, carries)
        return j - 1, carries, top(carries)

    lax.while_loop(cond, body, (i - 2, carries, top(carries)))
    o_ref[...] = jnp.concatenate([acc_ref[0], acc_ref[1]], axis=0).T.astype(o_ref.dtype)


def _sb_attention(proj):
    S = proj.shape[0]
    tq = min(256, S)
    npair = BRANCH_WIDTH // LANES
    return pl.pallas_call(
        functools.partial(_sb_body, tq=tq),
        grid=(npair, S // tq),
        in_specs=[pl.BlockSpec((tq, LANES), lambda p, i: (i, COL_QB // LANES + p)),
                  pl.BlockSpec((S, LANES), lambda p, i: (0, COL_KB // LANES + p)),
                  pl.BlockSpec((S, LANES), lambda p, i: (0, COL_VB // LANES + p))],
        out_specs=pl.BlockSpec((tq, LANES), lambda p, i: (i, p)),
        out_shape=jax.ShapeDtypeStruct((S, BRANCH_WIDTH), BF16),
        scratch_shapes=[pltpu.VMEM((S // tq, LANES, tq), BF16),
                        pltpu.VMEM((2 * tq, 2 * tq), BF16),
                        pltpu.VMEM((2, 1, tq), F32),
                        pltpu.VMEM((2, HEAD_DIM, tq), F32)],
        compiler_params=_params(("arbitrary", "arbitrary")),
        name="stickbreak_attention",
    )(proj, proj, proj)


def _diff_body(slopes_ref, q_ref, k_ref, v_ref, lq1_ref, lk1_ref, lq2_ref, lk2_ref, g_ref, o_ref,
               kaug_ref, vt_ref, knorm_ref, qa_ref, sta_ref, stb_ref, m_ref, acc_ref, *, tq, lambda_init):
    h = pl.program_id(0)
    i = pl.program_id(1)
    slope = slopes_ref[0, h]
    inv_slope = slopes_ref[1, h]
    nkb = kaug_ref.shape[0] // tq
    lane = lax.broadcasted_iota(jnp.int32, (1, LANES), 1)

    def max_sq_norms(x):
        sq = x.astype(F32)
        sq = sq * sq
        first = jnp.sum(jnp.where(lane < HEAD_DIM, sq, 0.0), axis=1, keepdims=True)
        second = jnp.sum(jnp.where(lane >= HEAD_DIM, sq, 0.0), axis=1, keepdims=True)
        return jnp.max(first, axis=0, keepdims=True), jnp.max(second, axis=0, keepdims=True)

    @pl.when(i == 0)
    def _():
        def setup(c, carry):
            off = pl.multiple_of(c * tq, tq)
            n1, n2 = max_sq_norms(k_ref[pl.ds(off, tq), :])
            carry = (jnp.maximum(carry[0], n1), jnp.maximum(carry[1], n2))
            pos = off + lax.broadcasted_iota(jnp.int32, (tq, LANES), 0)
            lanes = lax.broadcasted_iota(jnp.int32, (tq, LANES), 1)
            coarse = (pos // POS_SPLIT) * POS_SPLIT
            posm = jnp.where(lanes == 0, coarse, jnp.where(lanes == 1, pos - coarse, 0))
            kaug_ref[pl.ds(off, tq), 0:LANES] = k_ref[pl.ds(off, tq), :]
            kaug_ref[pl.ds(off, tq), LANES:2 * LANES] = posm.astype(F32).astype(BF16)
            vt_ref[c, 0:LANES, :] = v_ref[pl.ds(off, tq), :].astype(F32).T.astype(BF16)
            ones_row = lax.broadcasted_iota(jnp.int32, (SUM_ROWS, tq), 0) == 0
            vt_ref[c, LANES:LANES + SUM_ROWS, :] = jnp.where(ones_row, 1.0, 0.0).astype(BF16)
            return carry

        zero = jnp.zeros((1, 1), F32)
        knorm_ref[0], knorm_ref[1] = lax.fori_loop(0, nkb, setup, (zero, zero))

    q1n, q2n = max_sq_norms(q_ref[...])
    qk = jnp.sqrt(jnp.maximum(q1n * knorm_ref[0], q2n * knorm_ref[1]))
    reach = (ALIBI_CUT + 2.0 * QK_SCALE * qk) * inv_slope
    keep = jnp.minimum(jnp.floor((reach - 1.0) * (1.0 / tq)) + 1.0, 1e6).astype(jnp.int32)
    n_off = jnp.clip(jnp.max(keep), 0, i)
    j0 = i - n_off

    qs = q_ref[...] * QK_SCALE
    bias_cols = jnp.broadcast_to(jnp.where(lane < 2, slope, 0.0).astype(BF16), (tq, LANES))
    for half in range(2):
        in_half = (lane >= HEAD_DIM) if half else (lane < HEAD_DIM)
        qa_ref[half] = jnp.concatenate([jnp.where(in_half, qs, jnp.zeros_like(qs)), bias_cols], axis=1)
    acc_ref[...] = jnp.zeros_like(acc_ref)
    m_ref[...] = jnp.full_like(m_ref, NEG)
    on_or_below_diagonal = (lax.broadcasted_iota(jnp.int32, (tq, tq), 0)
                            <= lax.broadcasted_iota(jnp.int32, (tq, tq), 1))

    def scores(jb, st_ref):
        off = pl.multiple_of(jb * tq, tq)
        kb = kaug_ref[pl.ds(off, tq), :]
        for half in range(2):
            st_ref[half] = lax.dot_general(kb, qa_ref[half], (((1,), (1,)), ((), ())),
                                           preferred_element_type=F32)

    def softmax_pv(jb, st_ref, diagonal):
        vtb = vt_ref[jb]
        alphas, ps = [], []
        for half in range(2):
            st = st_ref[half]
            if diagonal:
                st = jnp.where(on_or_below_diagonal, st, NEG)
            m_old = m_ref[half]
            m_new = jnp.maximum(m_old, jnp.max(st, axis=0, keepdims=True))
            alpha = jnp.exp(m_old - m_new)
            p = jnp.exp(st - m_new)
            m_ref[half] = m_new
            alphas.append(alpha)
            ps.append(p.astype(BF16))
        for half in range(2):
            acc_ref[half] = alphas[half] * acc_ref[half] + jnp.dot(vtb, ps[half], preferred_element_type=F32)

    scores(j0, sta_ref)

    def pair(p, carry):
        scores(j0 + 2 * p + 1, stb_ref)
        softmax_pv(j0 + 2 * p, sta_ref, False)
        scores(j0 + 2 * p + 2, sta_ref)
        softmax_pv(j0 + 2 * p + 1, stb_ref, False)
        return carry

    lax.fori_loop(0, n_off // 2, pair, 0)

    @pl.when(n_off % 2 == 0)
    def _():
        softmax_pv(i, sta_ref, True)

    @pl.when(n_off % 2 == 1)
    def _():
        scores(i, stb_ref)
        softmax_pv(i - 1, sta_ref, False)
        softmax_pv(i, stb_ref, True)

    lam = (jnp.exp(jnp.sum(lq1_ref[...] * lk1_ref[...], axis=1, keepdims=True))
           - jnp.exp(jnp.sum(lq2_ref[...] * lk2_ref[...], axis=1, keepdims=True)) + lambda_init)
    o = (acc_ref[0, 0:LANES, :] / acc_ref[0, LANES:LANES + 1, :]
         - lam * (acc_ref[1, 0:LANES, :] / acc_ref[1, LANES:LANES + 1, :]))
    y = o * lax.rsqrt(jnp.mean(o * o, axis=0, keepdims=True) + RMS_EPS) * g_ref[...]
    o_ref[...] = (y * (1.0 - lambda_init)).T.astype(o_ref.dtype)


def _diff_attention(proj, slopes, lq1, lk1, lq2, lk2, subln_g, lambda_init):
    S = proj.shape[0]
    tq = min(512, S)
    vec = lambda n: pl.BlockSpec((1, n), lambda h, i: (0, 0))
    return pl.pallas_call(
        functools.partial(_diff_body, tq=tq, lambda_init=lambda_init),
        grid=(DIFF_HEADS, S // tq),
        in_specs=[pl.BlockSpec(memory_space=pltpu.SMEM),
                  pl.BlockSpec((tq, LANES), lambda h, i: (i, COL_QC // LANES + h)),
                  pl.BlockSpec((S, LANES), lambda h, i: (0, COL_KC // LANES + h)),
                  pl.BlockSpec((S, LANES), lambda h, i: (0, COL_VC // LANES + h)),
                  vec(HEAD_DIM), vec(HEAD_DIM), vec(HEAD_DIM), vec(HEAD_DIM),
                  pl.BlockSpec((2 * HEAD_DIM, 1), lambda h, i: (0, 0))],
        out_specs=pl.BlockSpec((tq, LANES), lambda h, i: (i, h)),
        out_shape=jax.ShapeDtypeStruct((S, BRANCH_WIDTH), BF16),
        scratch_shapes=[pltpu.VMEM((S, 2 * LANES), BF16),
                        pltpu.VMEM((S // tq, LANES + SUM_ROWS, tq), BF16),
                        pltpu.VMEM((2, 1, 1), F32),
                        pltpu.VMEM((2, tq, 2 * LANES), BF16),
                        pltpu.VMEM((2, tq, tq), F32), pltpu.VMEM((2, tq, tq), F32),
                        pltpu.VMEM((2, 1, tq), F32),
                        pltpu.VMEM((2, LANES + SUM_ROWS, tq), F32)],
        compiler_params=_params(("arbitrary", "arbitrary")),
        name="diff_attention",
    )(slopes, proj, proj, proj, lq1.reshape(1, -1), lk1.reshape(1, -1), lq2.reshape(1, -1),
      lk2.reshape(1, -1), subln_g.reshape(-1, 1))


def _branch_body(oa_ref, ob_ref, oc_ref, w_ref, ga_ref, gb_ref, gc_ref, z_ref):
    z = None
    for n, (o_ref, g_ref) in enumerate(((oa_ref, ga_ref), (ob_ref, gb_ref), (oc_ref, gc_ref))):
        y = jnp.dot(o_ref[...], w_ref[0, n].astype(BF16), preferred_element_type=F32)
        t = g_ref[...].astype(F32) * y
        z = t if z is None else z + t
    z_ref[...] = z.astype(z_ref.dtype)


def _branch_merge(o_a, o_b, o_c, w_branch, layer, gates):
    S = o_a.shape[0]
    D = w_branch.shape[3]
    tm = min(1024, S)
    tn = 512
    nj = D // tn
    o_spec = pl.BlockSpec((tm, BRANCH_WIDTH), lambda i, j: (i, 0))
    gate_spec = lambda n: pl.BlockSpec((tm, tn), lambda i, j: (i, n * nj + j))
    return pl.pallas_call(
        _branch_body,
        grid=(S // tm, nj),
        in_specs=[o_spec, o_spec, o_spec,
                  pl.BlockSpec((1, N_BRANCHES, BRANCH_WIDTH, tn), lambda i, j: (layer, 0, 0, j)),
                  gate_spec(0), gate_spec(1), gate_spec(2)],
        out_specs=pl.BlockSpec((tm, tn), lambda i, j: (i, j)),
        out_shape=jax.ShapeDtypeStruct((S, D), BF16),
        compiler_params=_params(("arbitrary", "arbitrary")),
        name="branch_merge",
    )(o_a, o_b, o_c, w_branch, gates, gates, gates)


def _layer_norm(r, g, b):
    mu = jnp.mean(r, axis=1, keepdims=True)
    d = r - mu
    var = jnp.mean(d * d, axis=1, keepdims=True)
    return d * lax.rsqrt(var + LN_EPS) * g + b


def _split_bf16(a):
    hi = a.astype(BF16)
    return hi, (a - hi.astype(F32)).astype(BF16)


def _out_body(z_ref, w_ref, x_ref, g1_ref, lng_ref, lnb_ref, sc2_ref, sh2_ref, wr_hi_ref, wr_lo_ref, br_ref,
              x1_ref, logit_ref, *, alpha):
    y = jnp.dot(z_ref[...], w_ref[...], preferred_element_type=F32)
    x1 = _layer_norm(alpha * x_ref[...] + g1_ref[...] * y, lng_ref[...], lnb_ref[...])
    x1_ref[...] = x1
    h_hi, h_lo = _split_bf16(x1 * (1.0 + sc2_ref[...]) + sh2_ref[...])
    w_hi = wr_hi_ref[...]
    logit_ref[...] = (jnp.dot(h_hi, w_hi, preferred_element_type=F32)
                      + (jnp.dot(h_hi, wr_lo_ref[...], preferred_element_type=F32)
                         + jnp.dot(h_lo, w_hi, preferred_element_type=F32))) + br_ref[...]


def _mixer_out(z, w_out, x, g1, ln_g, ln_b, sc2, sh2, w_router, b_router, alpha):
    S, D = x.shape
    tm = min(256, S)
    row = pl.BlockSpec((tm, D), lambda i: (i, 0))
    vec = pl.BlockSpec((1, D), lambda i: (0, 0))
    router = pl.BlockSpec((D, LANES), lambda i: (0, 0))
    wr_hi, wr_lo = _split_bf16(w_router)
    return pl.pallas_call(
        functools.partial(_out_body, alpha=alpha),
        grid=(S // tm,),
        in_specs=[row, pl.BlockSpec((D, D), lambda i: (0, 0)), row, vec, vec, vec, vec, vec,
                  router, router, pl.BlockSpec((1, LANES), lambda i: (0, 0))],
        out_specs=[row, pl.BlockSpec((tm, LANES), lambda i: (i, 0))],
        out_shape=[jax.ShapeDtypeStruct((S, D), F32), jax.ShapeDtypeStruct((S, LANES), F32)],
        compiler_params=_params(("arbitrary",)),
        name="mixer_out_ln",
    )(z, w_out, x, g1, ln_g, ln_b, sc2, sh2, wr_hi, wr_lo, b_router)


def _dispatch_body(dest_ref, x_ref, xb_in_hbm, xb_hbm, sem, *, rows_per_step):
    del xb_in_hbm
    i = pl.program_id(0)

    def row_copy(r, d):
        return pltpu.make_async_copy(x_ref.at[pl.ds(r, 1)], xb_hbm.at[pl.ds(d, 1)], sem)

    def issue(g, carry):
        for u in range(ROW_UNROLL):
            r = g * ROW_UNROLL + u
            t = i * rows_per_step + r
            for k in range(2):
                row_copy(r, dest_ref[2 * t + k]).start()
        return carry

    def wait(g, carry):
        for _ in range(2 * ROW_UNROLL):
            row_copy(0, 0).wait()
        return carry

    lax.fori_loop(0, rows_per_step // ROW_UNROLL, issue, 0)
    lax.fori_loop(0, rows_per_step // ROW_UNROLL, wait, 0)


def _dispatch(x1, dest, padded_rows):
    T, D = x1.shape
    rows_per_step = min(512, T)
    xb0 = jnp.zeros((padded_rows, D), F32)
    return pl.pallas_call(
        functools.partial(_dispatch_body, rows_per_step=rows_per_step),
        grid_spec=pltpu.PrefetchScalarGridSpec(
            num_scalar_prefetch=1,
            grid=(T // rows_per_step,),
            in_specs=[pl.BlockSpec((rows_per_step, D), lambda i, d: (i, 0)),
                      pl.BlockSpec(memory_space=pl.ANY)],
            out_specs=pl.BlockSpec(memory_space=pl.ANY),
            scratch_shapes=[pltpu.SemaphoreType.DMA(())]),
        out_shape=jax.ShapeDtypeStruct((padded_rows, D), F32),
        input_output_aliases={2: 0},
        compiler_params=_params(("arbitrary",)),
        name="moe_dispatch",
    )(dest, x1, xb0)


def _expert_body(be_ref, nu_ref, first_ref, slot_ref, next_ref, xb_ref, sc_ref, sh_ref,
                 wg_hbm, wu_hbm, wd_hbm, yb_ref, wg_f, wu_f, wd_f, wg_s, wu_s, wd_s, sem, *, layer):
    b = pl.program_id(0)
    nu = nu_ref[0]
    base = layer * N_EXPERTS

    def weight_copies(e, slot):
        return (pltpu.make_async_copy(wg_hbm.at[base + e], wg_f.at[slot], sem.at[slot, 0]),
                pltpu.make_async_copy(wu_hbm.at[base + e], wu_f.at[slot], sem.at[slot, 1]),
                pltpu.make_async_copy(wd_hbm.at[base + e], wd_f.at[slot], sem.at[slot, 2]))

    @pl.when(b == 0)
    def _():
        for copy in weight_copies(be_ref[0], 0):
            copy.start()

    @pl.when((b < nu) & (first_ref[b] == 1))
    def _():
        e = be_ref[b]
        nxt = next_ref[b]
        for slot in range(2):
            @pl.when(slot_ref[b] == slot)
            def _():
                for copy in weight_copies(e, slot):
                    copy.wait()

                @pl.when(nxt >= 0)
                def _():
                    for copy in weight_copies(nxt, 1 - slot):
                        copy.start()

                wg_s[...] = wg_f[slot].astype(BF16)
                wu_s[...] = wu_f[slot].astype(BF16)
                wd_s[...] = wd_f[slot].astype(BF16)

    @pl.when(b < nu)
    def _():
        h = (xb_ref[...] * (1.0 + sc_ref[...]) + sh_ref[...]).astype(BF16)
        a = jnp.dot(h, wg_s[...], preferred_element_type=F32)
        u = jnp.dot(h, wu_s[...], preferred_element_type=F32)
        act = (a / (1.0 + jnp.exp(-a))) * u
        yb_ref[...] = jnp.dot(act.astype(BF16), wd_s[...], preferred_element_type=F32)

    @pl.when(b >= nu)
    def _():
        yb_ref[...] = jnp.zeros_like(yb_ref)


def _experts(xb, sc2, sh2, w_g, w_u, w_d, plan, layer):
    P, D = xb.shape
    DB = DISPATCH_BLOCK
    nblk = P // DB
    blk = lambda b, be, nu, *_: (jnp.minimum(b, nu[0] - 1), 0)
    vec = pl.BlockSpec((1, D), lambda b, *_: (0, 0))
    hbm = pl.BlockSpec(memory_space=pl.ANY)
    return pl.pallas_call(
        functools.partial(_expert_body, layer=layer),
        grid_spec=pltpu.PrefetchScalarGridSpec(
            num_scalar_prefetch=5,
            grid=(nblk,),
            in_specs=[pl.BlockSpec((DB, D), blk), vec, vec, hbm, hbm, hbm],
            out_specs=pl.BlockSpec((DB, D), lambda b, *_: (b, 0)),
            scratch_shapes=[pltpu.VMEM((2, D, D_EXPERT), F32), pltpu.VMEM((2, D, D_EXPERT), F32),
                            pltpu.VMEM((2, D_EXPERT, D), F32),
                            pltpu.VMEM((D, D_EXPERT), BF16), pltpu.VMEM((D, D_EXPERT), BF16),
                            pltpu.VMEM((D_EXPERT, D), BF16),
                            pltpu.SemaphoreType.DMA((2, 3))]),
        out_shape=jax.ShapeDtypeStruct((P, D), F32),
        compiler_params=_params(("arbitrary",)),
        name="moe_experts",
    )(*plan, xb, sc2, sh2, w_g, w_u, w_d)


def _combine_body(dest_ref, x1_ref, wt_ref, g2_ref, lng_ref, lnb_ref, yb_hbm, o_ref,
                  buf0, buf1, sem, *, tm, alpha):
    i = pl.program_id(0)
    bufs = (buf0, buf1)

    def row_copy(d, k, r):
        return pltpu.make_async_copy(yb_hbm.at[pl.ds(d, 1)], bufs[k].at[pl.ds(r, 1)], sem)

    def issue(g, carry):
        for u in range(ROW_UNROLL):
            r = g * ROW_UNROLL + u
            t = i * tm + r
            for k in range(2):
                row_copy(dest_ref[2 * t + k], k, r).start()
        return carry

    def wait(g, carry):
        for _ in range(ROW_UNROLL):
            for k in range(2):
                row_copy(0, k, 0).wait()
        return carry

    lax.fori_loop(0, tm // ROW_UNROLL, issue, 0)
    lax.fori_loop(0, tm // ROW_UNROLL, wait, 0)
    wt = wt_ref[...]
    y = buf0[...] * wt[:, 0:1] + buf1[...] * wt[:, 1:2]
    o_ref[...] = _layer_norm(alpha * x1_ref[...] + g2_ref[...] * y, lng_ref[...], lnb_ref[...])


def _combine_ln(x1, yb, dest, w_top, g2, ln_g, ln_b, alpha):
    T, D = x1.shape
    tm = min(256, T)
    row = pl.BlockSpec((tm, D), lambda i, d: (i, 0))
    vec = pl.BlockSpec((1, D), lambda i, d: (0, 0))
    return pl.pallas_call(
        functools.partial(_combine_body, tm=tm, alpha=alpha),
        grid_spec=pltpu.PrefetchScalarGridSpec(
            num_scalar_prefetch=1,
            grid=(T // tm,),
            in_specs=[row, pl.BlockSpec((tm, 2), lambda i, d: (i, 0)), vec, vec, vec,
                      pl.BlockSpec(memory_space=pl.ANY)],
            out_specs=row,
            scratch_shapes=[pltpu.VMEM((tm, D), F32), pltpu.VMEM((tm, D), F32),
                            pltpu.SemaphoreType.DMA(())]),
        out_shape=jax.ShapeDtypeStruct((T, D), F32),
        compiler_params=_params(("arbitrary",)),
        name="moe_combine_ln",
    )(dest, x1, w_top, g2, ln_g, ln_b, yb)


def _route(logits):
    T = logits.shape[0]
    DB = DISPATCH_BLOCK
    group_logits = logits[:, :N_GROUPS]
    exp_logits = logits[:, N_GROUPS:N_GROUPS + N_EXPERTS].reshape(T, N_GROUPS, EXPERTS_PER_GROUP)
    group_prob = jax.nn.softmax(group_logits, axis=-1)
    g_top = jnp.argmax(group_logits, axis=-1)
    p_group = jnp.take_along_axis(group_prob, g_top[:, None], axis=-1)
    in_group = jnp.take_along_axis(exp_logits, g_top[:, None, None], axis=1)[:, 0]
    top_val, top_idx = lax.top_k(in_group, 2)
    w_top = jax.nn.softmax(top_val, axis=-1) * p_group
    expert_id = g_top[:, None].astype(jnp.int32) * EXPERTS_PER_GROUP + top_idx.astype(jnp.int32)

    member = (expert_id[:, :, None] == jnp.arange(N_EXPERTS, dtype=jnp.int32)).any(axis=1).astype(jnp.int32)
    incl = jnp.cumsum(member, axis=0)
    counts = incl[-1]
    rank = jnp.take_along_axis(incl - member, expert_id, axis=1)
    padded = (counts + DB - 1) // DB * DB
    pend = jnp.cumsum(padded)
    pstart = pend - padded
    dest = (pstart[expert_id] + rank).astype(jnp.int32)
    P = ((2 * T + DB - 1) // DB) * DB + N_EXPERTS * DB
    nblk = P // DB
    block_e = jnp.minimum(jnp.searchsorted(pend, jnp.arange(nblk, dtype=jnp.int32) * DB, side='right'),
                          N_EXPERTS - 1).astype(jnp.int32)
    n_used = (pend[-1:] // DB).astype(jnp.int32)
    b_idx = jnp.arange(nblk, dtype=jnp.int32)
    first = (b_idx < n_used) & ((b_idx == 0) | (block_e != jnp.roll(block_e, 1)))
    slot = (jnp.cumsum(first.astype(jnp.int32)) - 1) % 2
    after = (pend // DB).astype(jnp.int32)[block_e]
    next_e = jnp.where(after < n_used, block_e[jnp.minimum(after, nblk - 1)], -1)
    plan = (block_e, n_used, first.astype(jnp.int32), slot.astype(jnp.int32), next_e.astype(jnp.int32))
    return dest.reshape(-1), w_top, plan, P


def kernel(x, c, w_ada, b_ada, w_in, w_branch_gate, b_branch_gate, attn_sinks, lambda_q1, lambda_k1,
           lambda_q2, lambda_k2, subln_g, w_branch, w_out, ln1_g, ln1_b, w_router_group, b_router_group,
           w_router_expert, b_router_expert, w_exp_gate, w_exp_up, w_exp_down, ln2_g, ln2_b):
    B, S, D = x.shape
    assert B == 1 and D == D_MODEL
    depth = w_in.shape[0]
    alpha = ALPHA
    xs = x.reshape(S, D)
    mod = _ada_mod(c, w_ada, b_ada)
    diff_slopes = jnp.exp2(-8.0 * jnp.arange(1, DIFF_HEADS + 1, dtype=F32) / DIFF_HEADS)
    diff_slopes = jnp.stack([diff_slopes, 1.0 / diff_slopes])
    zero_bias = jnp.zeros((1, w_in.shape[2]), F32)
    pad = LANES - N_GROUPS - N_EXPERTS
    w_eg = w_exp_gate.reshape(depth * N_EXPERTS, D, D_EXPERT)
    w_eu = w_exp_up.reshape(depth * N_EXPERTS, D, D_EXPERT)
    w_ed = w_exp_down.reshape(depth * N_EXPERTS, D_EXPERT, D)
    for l in range(depth):
        lambda_init = 0.8 - 0.6 * math.exp(-0.3 * l)
        sh1, sc1, g1, sh2, sc2, g2 = [mod[l, :, n * D:(n + 1) * D] for n in range(6)]
        proj = _mod_matmul(xs, sc1, sh1, w_in, l, zero_bias, sigmoid=False, name="in_proj")
        gates = _mod_matmul(xs, sc1, sh1, w_branch_gate, l, b_branch_gate[l].reshape(1, -1),
                            sigmoid=True, name="branch_gates")
        o_a = _swa_attention(proj, attn_sinks[l])
        o_b = _sb_attention(proj)
        o_c = _diff_attention(proj, diff_slopes, lambda_q1[l], lambda_k1[l], lambda_q2[l], lambda_k2[l],
                              subln_g[l], lambda_init)
        z = _branch_merge(o_a, o_b, o_c, w_branch, l, gates)
        w_router = jnp.pad(jnp.concatenate([w_router_group[l], w_router_expert[l]], axis=1), ((0, 0), (0, pad)))
        b_router = jnp.pad(jnp.concatenate([b_router_group[l], b_router_expert[l]]), (0, pad)).reshape(1, LANES)
        x1, logits = _mixer_out(z, w_out[l].astype(BF16), xs, g1, ln1_g[l].reshape(1, D), ln1_b[l].reshape(1, D),
                                sc2, sh2, w_router, b_router, alpha)
        dest, w_top, plan, P = _route(logits)
        xb = _dispatch(x1, dest, P)
        yb = _experts(xb, sc2, sh2, w_eg, w_eu, w_ed, plan, l)
        xs = _combine_ln(x1, yb, dest, w_top, g2, ln2_g[l].reshape(1, D), ln2_b[l].reshape(1, D), alpha)
    return xs.reshape(B, S, D)
```

```python
import functools
import math

import jax
import jax.numpy as jnp
from jax import lax
from jax.experimental import pallas as pl
from jax.experimental.pallas import tpu as pltpu

F32 = jnp.float32
BF16 = jnp.bfloat16

D_MODEL = 2048
HEAD_DIM = 64
SWA_Q_HEADS = 16
SWA_KV_HEADS = 4
WINDOW = 128
SB_HEADS = 16
DIFF_HEADS = 8
BLOCK_Q = 128
BRANCH_WIDTH = 1024
A_KV = 256
N_BRANCHES = 3
N_GROUPS = 8
EXPERTS_PER_GROUP = 8
N_EXPERTS = 64
D_EXPERT = 384
DISPATCH_BLOCK = 128
DEPTH = 4
ALPHA = (2.0 * DEPTH) ** 0.25
LN_EPS = 1e-5
RMS_EPS = 1e-5
QK_SCALE = 1.0 / math.sqrt(HEAD_DIM)

COL_QA, COL_KA, COL_VA = 0, 1024, 1280
COL_QB, COL_KB, COL_VB = 1536, 2560, 3584
COL_QC, COL_KC, COL_VC = 4608, 5632, 6656

LANES = 128
VMEM_LIMIT = 56 * 1024 * 1024
NEG = -1e30
SB_EXIT = 88.0
ALIBI_CUT = 100.0
POS_SPLIT = 64
SUM_ROWS = 16
ROW_UNROLL = 8


def _params(sem):
    return pltpu.CompilerParams(dimension_semantics=sem, vmem_limit_bytes=VMEM_LIMIT)


def _ada_body(c_ref, w_ref, b_ref, o_ref):
    o_ref[0] = jnp.sum(c_ref[...] * w_ref[0], axis=0, keepdims=True) + b_ref[0]


def _ada_mod(c, w_ada, b_ada):
    L, D, N = w_ada.shape
    tn = 512
    return pl.pallas_call(
        _ada_body,
        grid=(L, N // tn),
        in_specs=[pl.BlockSpec((D, 1), lambda l, j: (0, 0)),
                  pl.BlockSpec((1, D, tn), lambda l, j: (l, 0, j)),
                  pl.BlockSpec((1, 1, tn), lambda l, j: (l, 0, j))],
        out_specs=pl.BlockSpec((1, 1, tn), lambda l, j: (l, 0, j)),
        out_shape=jax.ShapeDtypeStruct((L, 1, N), F32),
        compiler_params=_params(("arbitrary", "arbitrary")),
        name="ada_mod",
    )(c.reshape(D, 1), w_ada, b_ada.reshape(L, 1, N))


def _modmm_body(x_ref, sc_ref, sh_ref, w_ref, b_ref, o_ref, h_ref, *, sigmoid):
    @pl.when(pl.program_id(1) == 0)
    def _():
        h_ref[...] = (x_ref[...] * (1.0 + sc_ref[...]) + sh_ref[...]).astype(BF16)

    acc = jnp.dot(h_ref[...], w_ref[0].astype(BF16), preferred_element_type=F32) + b_ref[...]
    if sigmoid:
        acc = 1.0 / (1.0 + jnp.exp(-acc))
    o_ref[...] = acc.astype(o_ref.dtype)


def _mod_matmul(x, sc, sh, w, layer, b, *, sigmoid, name):
    M, K = x.shape
    N = w.shape[2]
    tm = min(1024, M)
    tn = 768
    return pl.pallas_call(
        functools.partial(_modmm_body, sigmoid=sigmoid),
        grid=(M // tm, N // tn),
        in_specs=[pl.BlockSpec((tm, K), lambda i, j: (i, 0)),
                  pl.BlockSpec((1, K), lambda i, j: (0, 0)),
                  pl.BlockSpec((1, K), lambda i, j: (0, 0)),
                  pl.BlockSpec((1, K, tn), lambda i, j: (layer, 0, j)),
                  pl.BlockSpec((1, tn), lambda i, j: (0, j))],
        out_specs=pl.BlockSpec((tm, tn), lambda i, j: (i, j)),
        out_shape=jax.ShapeDtypeStruct((M, N), BF16),
        scratch_shapes=[pltpu.VMEM((tm, K), BF16)],
        compiler_params=_params(("arbitrary", "arbitrary")),
        name=name,
    )(x, sc, sh, w, b)


def _swa_body(sinks_ref, q_ref, kp_ref, kc_ref, vp_ref, vc_ref, o_ref):
    i = pl.program_id(0)
    k = jnp.concatenate([kp_ref[...], kc_ref[...]], axis=0)
    v = jnp.concatenate([vp_ref[...], vc_ref[...]], axis=0)
    qi = lax.broadcasted_iota(jnp.int32, (BLOCK_Q, 2 * BLOCK_Q), 0)
    ki = lax.broadcasted_iota(jnp.int32, (BLOCK_Q, 2 * BLOCK_Q), 1)
    dist = qi + BLOCK_Q - ki
    valid = (dist >= 0) & (dist < WINDOW) & ((ki >= BLOCK_Q) | (i > 0))
    distf = dist.astype(F32)
    group = SWA_Q_HEADS // SWA_KV_HEADS
    outs = []
    for h in range(SWA_Q_HEADS):
        g = h // group
        slope = 2.0 ** (-8.0 * (h + 1) / SWA_Q_HEADS)
        qh = q_ref[:, h * HEAD_DIM:(h + 1) * HEAD_DIM]
        kg = k[:, g * HEAD_DIM:(g + 1) * HEAD_DIM]
        vg = v[:, g * HEAD_DIM:(g + 1) * HEAD_DIM]
        s = lax.dot_general(qh, kg, (((1,), (1,)), ((), ())), preferred_element_type=F32) * QK_SCALE
        s = jnp.where(valid, s - slope * distf, NEG)
        sink = sinks_ref[h]
        m = jnp.maximum(jnp.max(s, axis=1, keepdims=True), sink)
        p = jnp.where(valid, jnp.exp(s - m), 0.0)
        denom = jnp.sum(p, axis=1, keepdims=True) + jnp.exp(sink - m)
        o = jnp.dot(p.astype(BF16), vg, preferred_element_type=F32) / denom
        outs.append(o)
    o_ref[...] = jnp.concatenate(outs, axis=1).astype(o_ref.dtype)


def _swa_attention(proj, sinks):
    S = proj.shape[0]
    nb = S // BLOCK_Q
    kv_blk = lambda col: pl.BlockSpec((BLOCK_Q, A_KV), lambda i: (i, col // A_KV))
    kv_prev = lambda col: pl.BlockSpec((BLOCK_Q, A_KV), lambda i: (jnp.maximum(i - 1, 0), col // A_KV))
    return pl.pallas_call(
        _swa_body,
        grid=(nb,),
        in_specs=[pl.BlockSpec(memory_space=pltpu.SMEM),
                  pl.BlockSpec((BLOCK_Q, BRANCH_WIDTH), lambda i: (i, COL_QA // BRANCH_WIDTH)),
                  kv_prev(COL_KA), kv_blk(COL_KA), kv_prev(COL_VA), kv_blk(COL_VA)],
        out_specs=pl.BlockSpec((BLOCK_Q, BRANCH_WIDTH), lambda i: (i, 0)),
        out_shape=jax.ShapeDtypeStruct((S, BRANCH_WIDTH), BF16),
        compiler_params=_params(("arbitrary",)),
        name="swa_attention",
    )(sinks, proj, proj, proj, proj, proj)


def _sb_body(q_ref, k_ref, v_ref, o_ref, vt_ref, later_ref, carry_ref, acc_ref, *, tq):
    g = pl.program_id(1)
    nkb = vt_ref.shape[0]
    lane = lax.broadcasted_iota(jnp.int32, (1, LANES), 1)

    @pl.when(g == 0)
    def _():
        def setup(c, carry):
            off = pl.multiple_of(c * tq, tq)
            vt_ref[c] = v_ref[pl.ds(off, tq), :].astype(F32).T.astype(BF16)
            return carry

        lax.fori_loop(0, nkb, setup, 0)
        this_key = lax.broadcasted_iota(jnp.int32, later_ref.shape, 0)
        other_key = lax.broadcasted_iota(jnp.int32, later_ref.shape, 1)
        later_ref[...] = (other_key > this_key).astype(BF16)

    qs = q_ref[...] * QK_SCALE
    acc_ref[...] = jnp.zeros_like(acc_ref)

    def stream(tile, half, first, carry):
        in_half = (lane >= HEAD_DIM) if half else (lane < HEAD_DIM)
        q_tile = qs[tile * tq:(tile + 1) * tq, :]
        return dict(qm=jnp.where(in_half, q_tile, jnp.zeros_like(q_tile)), tile=tile, half=half,
                    first=first, carry=carry)

    def process(streams, nb, masked):
        n = nb * tq
        later = later_ref[...]
        if masked:
            k_row = lax.broadcasted_iota(jnp.int32, (n, tq), 0)
            q_col = lax.broadcasted_iota(jnp.int32, (n, tq), 1)
            before = k_row < q_col + (nb - 1) * tq
        zs = []
        for st in streams:
            off = pl.multiple_of(st["first"] * tq, tq)
            zs.append(lax.dot_general(k_ref[pl.ds(off, n), :], st["qm"], (((1,), (1,)), ((), ())),
                                      preferred_element_type=F32))
        log_rems, splits = [], []
        for z in zs:
            soft = jnp.log(1.0 + jnp.exp(-jnp.abs(z)))
            log_rem = -(jnp.maximum(z, 0.0) + soft)
            if masked:
                log_rem = jnp.where(before, log_rem, 0.0)
            hi = log_rem.astype(BF16)
            lo = (log_rem - hi.astype(F32)).astype(BF16)
            log_rems.append(log_rem)
            splits.append((hi, lo))
        afters, totals = [], []
        for (hi, lo), log_rem in zip(splits, log_rems):
            parts, total = [None] * nb, jnp.zeros((1, tq), F32)
            for b in reversed(range(nb)):
                blk = slice(b * tq, (b + 1) * tq)
                inside = (jnp.dot(later, hi[blk, :], preferred_element_type=F32)
                          + jnp.dot(later, lo[blk, :], preferred_element_type=F32))
                parts[b] = inside + total
                total = total + inside[0:1, :] + log_rem[b * tq:b * tq + 1, :]
            afters.append(parts[0] if nb == 1 else jnp.concatenate(parts, axis=0))
            totals.append(total)
        new_carries = []
        for st, z, log_rem, after, total in zip(streams, zs, log_rems, afters, totals):
            w = jnp.exp((log_rem + z) + (after + st["carry"]))
            if masked:
                w = jnp.where(before, w, 0.0)
            w = w.astype(BF16)
            rows = slice(st["half"] * HEAD_DIM, (st["half"] + 1) * HEAD_DIM)
            for b in range(nb):
                acc_ref[st["tile"], st["half"]] += jnp.dot(vt_ref[st["first"] + b][rows, :],
                                                           w[b * tq:(b + 1) * tq, :], preferred_element_type=F32)
            new_carries.append(st["carry"] + total)
        return new_carries

    zero = jnp.zeros((1, tq), F32)
    tiles_heads = [(tile, half) for tile in range(2) for half in range(2)]

    def keep(carries, pairs):
        for (tile, half), carry in zip(pairs, carries):
            carry_ref[tile, half] = carry

    @pl.when(g == 0)
    def _():
        keep(process([stream(0, half, 0, zero) for half in range(2)], 1, True), tiles_heads[:2])
        keep(process([stream(1, half, 0, zero) for half in range(2)], 2, True), tiles_heads[2:])

    @pl.when(g > 0)
    def _():
        keep(process([stream(tile, half, 2 * g + tile - 1, zero) for tile, half in tiles_heads], 2, True),
             tiles_heads)

    def top(carries):
        return jnp.maximum(jnp.max(carries[0]), jnp.max(carries[1]))

    for tile in range(2):
        carries = (carry_ref[tile, 0], carry_ref[tile, 1])

        def cond(state):
            j, _, best = state
            return (j >= 0) & (best > -SB_EXIT)

        def body(state, tile=tile):
            j, carries, _ = state
            carries = tuple(process([stream(tile, half, j, carries[half]) for half in range(2)], 1, False))
            return j - 1, carries, top(carries)

        lax.while_loop(cond, body, (2 * g + tile - 2, carries, top(carries)))
        o_ref[tile * tq:(tile + 1) * tq, :] = jnp.concatenate(
            [acc_ref[tile, 0], acc_ref[tile, 1]], axis=0).T.astype(o_ref.dtype)


def _sb_attention(proj):
    S = proj.shape[0]
    tq = min(256, S // 2)
    npair = BRANCH_WIDTH // LANES
    return pl.pallas_call(
        functools.partial(_sb_body, tq=tq),
        grid=(npair, S // (2 * tq)),
        in_specs=[pl.BlockSpec((2 * tq, LANES), lambda p, g: (g, COL_QB // LANES + p)),
                  pl.BlockSpec((S, LANES), lambda p, g: (0, COL_KB // LANES + p)),
                  pl.BlockSpec((S, LANES), lambda p, g: (0, COL_VB // LANES + p))],
        out_specs=pl.BlockSpec((2 * tq, LANES), lambda p, g: (g, p)),
        out_shape=jax.ShapeDtypeStruct((S, BRANCH_WIDTH), BF16),
        scratch_shapes=[pltpu.VMEM((S // tq, LANES, tq), BF16),
                        pltpu.VMEM((tq, tq), BF16),
                        pltpu.VMEM((2, 2, 1, tq), F32),
                        pltpu.VMEM((2, 2, HEAD_DIM, tq), F32)],
        compiler_params=_params(("arbitrary", "arbitrary")),
        name="stickbreak_attention",
    )(proj, proj, proj)


def _diff_body(slopes_ref, q_ref, k_ref, v_ref, lq1_ref, lk1_ref, lq2_ref, lk2_ref, g_ref, o_ref,
               kaug_ref, vt_ref, knorm_ref, qa_ref, sta_ref, stb_ref, m_ref, acc_ref, *, tq, lambda_init):
    h = pl.program_id(0)
    i = pl.program_id(1)
    slope = slopes_ref[0, h]
    inv_slope = slopes_ref[1, h]
    nkb = kaug_ref.shape[0] // tq
    lane = lax.broadcasted_iota(jnp.int32, (1, LANES), 1)

    def max_sq_norms(x):
        sq = x.astype(F32)
        sq = sq * sq
        first = jnp.sum(jnp.where(lane < HEAD_DIM, sq, 0.0), axis=1, keepdims=True)
        second = jnp.sum(jnp.where(lane >= HEAD_DIM, sq, 0.0), axis=1, keepdims=True)
        return jnp.max(first, axis=0, keepdims=True), jnp.max(second, axis=0, keepdims=True)

    @pl.when(i == 0)
    def _():
        def setup(c, carry):
            off = pl.multiple_of(c * tq, tq)
            n1, n2 = max_sq_norms(k_ref[pl.ds(off, tq), :])
            carry = (jnp.maximum(carry[0], n1), jnp.maximum(carry[1], n2))
            pos = off + lax.broadcasted_iota(jnp.int32, (tq, LANES), 0)
            lanes = lax.broadcasted_iota(jnp.int32, (tq, LANES), 1)
            coarse = (pos // POS_SPLIT) * POS_SPLIT
            posm = jnp.where(lanes == 0, coarse, jnp.where(lanes == 1, pos - coarse, 0))
            kaug_ref[pl.ds(off, tq), 0:LANES] = k_ref[pl.ds(off, tq), :]
            kaug_ref[pl.ds(off, tq), LANES:2 * LANES] = posm.astype(F32).astype(BF16)
            vt_ref[c, 0:LANES, :] = v_ref[pl.ds(off, tq), :].astype(F32).T.astype(BF16)
            ones_row = lax.broadcasted_iota(jnp.int32, (SUM_ROWS, tq), 0) == 0
            vt_ref[c, LANES:LANES + SUM_ROWS, :] = jnp.where(ones_row, 1.0, 0.0).astype(BF16)
            return carry

        zero = jnp.zeros((1, 1), F32)
        knorm_ref[0], knorm_ref[1] = lax.fori_loop(0, nkb, setup, (zero, zero))

    q1n, q2n = max_sq_norms(q_ref[...])
    qk = jnp.sqrt(jnp.maximum(q1n * knorm_ref[0], q2n * knorm_ref[1]))
    reach = (ALIBI_CUT + 2.0 * QK_SCALE * qk) * inv_slope
    keep = jnp.minimum(jnp.floor((reach - 1.0) * (1.0 / tq)) + 1.0, 1e6).astype(jnp.int32)
    n_off = jnp.clip(jnp.max(keep), 0, i)
    j0 = i - n_off

    qs = q_ref[...] * QK_SCALE
    bias_cols = jnp.broadcast_to(jnp.where(lane < 2, slope, 0.0).astype(BF16), (tq, LANES))
    for half in range(2):
        in_half = (lane >= HEAD_DIM) if half else (lane < HEAD_DIM)
        qa_ref[half] = jnp.concatenate([jnp.where(in_half, qs, jnp.zeros_like(qs)), bias_cols], axis=1)
    acc_ref[...] = jnp.zeros_like(acc_ref)
    m_ref[...] = jnp.full_like(m_ref, NEG)
    on_or_below_diagonal = (lax.broadcasted_iota(jnp.int32, (tq, tq), 0)
                            <= lax.broadcasted_iota(jnp.int32, (tq, tq), 1))

    def scores(jb, st_ref):
        off = pl.multiple_of(jb * tq, tq)
        kb = kaug_ref[pl.ds(off, tq), :]
        for half in range(2):
            st_ref[half] = lax.dot_general(kb, qa_ref[half], (((1,), (1,)), ((), ())),
                                           preferred_element_type=F32)

    def softmax_pv(jb, st_ref, diagonal):
        vtb = vt_ref[jb]
        alphas, ps = [], []
        for half in range(2):
            st = st_ref[half]
            if diagonal:
                st = jnp.where(on_or_below_diagonal, st, NEG)
            m_old = m_ref[half]
            m_new = jnp.maximum(m_old, jnp.max(st, axis=0, keepdims=True))
            alpha = jnp.exp(m_old - m_new)
            p = jnp.exp(st - m_new)
            m_ref[half] = m_new
            alphas.append(alpha)
            ps.append(p.astype(BF16))
        for half in range(2):
            acc_ref[half] = alphas[half] * acc_ref[half] + jnp.dot(vtb, ps[half], preferred_element_type=F32)

    scores(j0, sta_ref)

    def pair(p, carry):
        scores(j0 + 2 * p + 1, stb_ref)
        softmax_pv(j0 + 2 * p, sta_ref, False)
        scores(j0 + 2 * p + 2, sta_ref)
        softmax_pv(j0 + 2 * p + 1, stb_ref, False)
        return carry

    lax.fori_loop(0, n_off // 2, pair, 0)

    @pl.when(n_off % 2 == 0)
    def _():
        softmax_pv(i, sta_ref, True)

    @pl.when(n_off % 2 == 1)
    def _():
        scores(i, stb_ref)
        softmax_pv(i - 1, sta_ref, False)
        softmax_pv(i, stb_ref, True)

    lam = (jnp.exp(jnp.sum(lq1_ref[...] * lk1_ref[...], axis=1, keepdims=True))
           - jnp.exp(jnp.sum(lq2_ref[...] * lk2_ref[...], axis=1, keepdims=True)) + lambda_init)
    o = (acc_ref[0, 0:LANES, :] / acc_ref[0, LANES:LANES + 1, :]
         - lam * (acc_ref[1, 0:LANES, :] / acc_ref[1, LANES:LANES + 1, :]))
    y = o * lax.rsqrt(jnp.mean(o * o, axis=0, keepdims=True) + RMS_EPS) * g_ref[...]
    o_ref[...] = (y * (1.0 - lambda_init)).T.astype(o_ref.dtype)


def _diff_attention(proj, slopes, lq1, lk1, lq2, lk2, subln_g, lambda_init):
    S = proj.shape[0]
    tq = min(512, S)
    vec = lambda n: pl.BlockSpec((1, n), lambda h, i: (0, 0))
    return pl.pallas_call(
        functools.partial(_diff_body, tq=tq, lambda_init=lambda_init),
        grid=(DIFF_HEADS, S // tq),
        in_specs=[pl.BlockSpec(memory_space=pltpu.SMEM),
                  pl.BlockSpec((tq, LANES), lambda h, i: (i, COL_QC // LANES + h)),
                  pl.BlockSpec((S, LANES), lambda h, i: (0, COL_KC // LANES + h)),
                  pl.BlockSpec((S, LANES), lambda h, i: (0, COL_VC // LANES + h)),
                  vec(HEAD_DIM), vec(HEAD_DIM), vec(HEAD_DIM), vec(HEAD_DIM),
                  pl.BlockSpec((2 * HEAD_DIM, 1), lambda h, i: (0, 0))],
        out_specs=pl.BlockSpec((tq, LANES), lambda h, i: (i, h)),
        out_shape=jax.ShapeDtypeStruct((S, BRANCH_WIDTH), BF16),
        scratch_shapes=[pltpu.VMEM((S, 2 * LANES), BF16),
                        pltpu.VMEM((S // tq, LANES + SUM_ROWS, tq), BF16),
                        pltpu.VMEM((2, 1, 1), F32),
                        pltpu.VMEM((2, tq, 2 * LANES), BF16),
                        pltpu.VMEM((2, tq, tq), F32), pltpu.VMEM((2, tq, tq), F32),
                        pltpu.VMEM((2, 1, tq), F32),
                        pltpu.VMEM((2, LANES + SUM_ROWS, tq), F32)],
        compiler_params=_params(("arbitrary", "arbitrary")),
        name="diff_attention",
    )(slopes, proj, proj, proj, lq1.reshape(1, -1), lk1.reshape(1, -1), lq2.reshape(1, -1),
      lk2.reshape(1, -1), subln_g.reshape(-1, 1))


def _branch_body(oa_ref, ob_ref, oc_ref, w_ref, ga_ref, gb_ref, gc_ref, z_ref):
    z = None
    for n, (o_ref, g_ref) in enumerate(((oa_ref, ga_ref), (ob_ref, gb_ref), (oc_ref, gc_ref))):
        y = jnp.dot(o_ref[...], w_ref[0, n].astype(BF16), preferred_element_type=F32)
        t = g_ref[...].astype(F32) * y
        z = t if z is None else z + t
    z_ref[...] = z.astype(z_ref.dtype)


def _branch_merge(o_a, o_b, o_c, w_branch, layer, gates):
    S = o_a.shape[0]
    D = w_branch.shape[3]
    tm = min(1024, S)
    tn = 512
    nj = D // tn
    o_spec = pl.BlockSpec((tm, BRANCH_WIDTH), lambda i, j: (i, 0))
    gate_spec = lambda n: pl.BlockSpec((tm, tn), lambda i, j: (i, n * nj + j))
    return pl.pallas_call(
        _branch_body,
        grid=(S // tm, nj),
        in_specs=[o_spec, o_spec, o_spec,
                  pl.BlockSpec((1, N_BRANCHES, BRANCH_WIDTH, tn), lambda i, j: (layer, 0, 0, j)),
                  gate_spec(0), gate_spec(1), gate_spec(2)],
        out_specs=pl.BlockSpec((tm, tn), lambda i, j: (i, j)),
        out_shape=jax.ShapeDtypeStruct((S, D), BF16),
        compiler_params=_params(("arbitrary", "arbitrary")),
        name="branch_merge",
    )(o_a, o_b, o_c, w_branch, gates, gates, gates)


def _layer_norm(r, g, b):
    mu = jnp.mean(r, axis=1, keepdims=True)
    d = r - mu
    var = jnp.mean(d * d, axis=1, keepdims=True)
    return d * lax.rsqrt(var + LN_EPS) * g + b


def _split_bf16(a):
    hi = a.astype(BF16)
    return hi, (a - hi.astype(F32)).astype(BF16)


def _out_body(z_ref, w_ref, x_ref, g1_ref, lng_ref, lnb_ref, sc2_ref, sh2_ref, wr_hi_ref, wr_lo_ref, br_ref,
              x1_ref, logit_ref, *, alpha):
    y = jnp.dot(z_ref[...], w_ref[...], preferred_element_type=F32)
    x1 = _layer_norm(alpha * x_ref[...] + g1_ref[...] * y, lng_ref[...], lnb_ref[...])
    x1_ref[...] = x1
    h_hi, h_lo = _split_bf16(x1 * (1.0 + sc2_ref[...]) + sh2_ref[...])
    w_hi = wr_hi_ref[...]
    logit_ref[...] = (jnp.dot(h_hi, w_hi, preferred_element_type=F32)
                      + (jnp.dot(h_hi, wr_lo_ref[...], preferred_element_type=F32)
                         + jnp.dot(h_lo, w_hi, preferred_element_type=F32))) + br_ref[...]


def _mixer_out(z, w_out, x, g1, ln_g, ln_b, sc2, sh2, w_router, b_router, alpha):
    S, D = x.shape
    tm = min(256, S)
    row = pl.BlockSpec((tm, D), lambda i: (i, 0))
    vec = pl.BlockSpec((1, D), lambda i: (0, 0))
    router = pl.BlockSpec((D, LANES), lambda i: (0, 0))
    wr_hi, wr_lo = _split_bf16(w_router)
    return pl.pallas_call(
        functools.partial(_out_body, alpha=alpha),
        grid=(S // tm,),
        in_specs=[row, pl.BlockSpec((D, D), lambda i: (0, 0)), row, vec, vec, vec, vec, vec,
                  router, router, pl.BlockSpec((1, LANES), lambda i: (0, 0))],
        out_specs=[row, pl.BlockSpec((tm, LANES), lambda i: (i, 0))],
        out_shape=[jax.ShapeDtypeStruct((S, D), F32), jax.ShapeDtypeStruct((S, LANES), F32)],
        compiler_params=_params(("arbitrary",)),
        name="mixer_out_ln",
    )(z, w_out, x, g1, ln_g, ln_b, sc2, sh2, wr_hi, wr_lo, b_router)


def _dispatch_body(dest_ref, x_ref, xb_in_hbm, xb_hbm, sem, *, rows_per_step):
    del xb_in_hbm
    i = pl.program_id(0)

    def row_copy(r, d):
        return pltpu.make_async_copy(x_ref.at[pl.ds(r, 1)], xb_hbm.at[pl.ds(d, 1)], sem)

    def issue(g, carry):
        for u in range(ROW_UNROLL):
            r = g * ROW_UNROLL + u
            t = i * rows_per_step + r
            for k in range(2):
                row_copy(r, dest_ref[2 * t + k]).start()
        return carry

    def wait(g, carry):
        for _ in range(2 * ROW_UNROLL):
            row_copy(0, 0).wait()
        return carry

    lax.fori_loop(0, rows_per_step // ROW_UNROLL, issue, 0)
    lax.fori_loop(0, rows_per_step // ROW_UNROLL, wait, 0)


def _dispatch(x1, dest, padded_rows):
    T, D = x1.shape
    rows_per_step = min(512, T)
    xb0 = jnp.zeros((padded_rows, D), F32)
    return pl.pallas_call(
        functools.partial(_dispatch_body, rows_per_step=rows_per_step),
        grid_spec=pltpu.PrefetchScalarGridSpec(
            num_scalar_prefetch=1,
            grid=(T // rows_per_step,),
            in_specs=[pl.BlockSpec((rows_per_step, D), lambda i, d: (i, 0)),
                      pl.BlockSpec(memory_space=pl.ANY)],
            out_specs=pl.BlockSpec(memory_space=pl.ANY),
            scratch_shapes=[pltpu.SemaphoreType.DMA(())]),
        out_shape=jax.ShapeDtypeStruct((padded_rows, D), F32),
        input_output_aliases={2: 0},
        compiler_params=_params(("arbitrary",)),
        name="moe_dispatch",
    )(dest, x1, xb0)


def _expert_body(be_ref, nu_ref, first_ref, slot_ref, next_ref, xb_ref, sc_ref, sh_ref,
                 wg_hbm, wu_hbm, wd_hbm, yb_ref, wg_f, wu_f, wd_f, wg_s, wu_s, wd_s, sem, *, layer):
    b = pl.program_id(0)
    nu = nu_ref[0]
    base = layer * N_EXPERTS

    def weight_copies(e, slot):
        return (pltpu.make_async_copy(wg_hbm.at[base + e], wg_f.at[slot], sem.at[slot, 0]),
                pltpu.make_async_copy(wu_hbm.at[base + e], wu_f.at[slot], sem.at[slot, 1]),
                pltpu.make_async_copy(wd_hbm.at[base + e], wd_f.at[slot], sem.at[slot, 2]))

    @pl.when(b == 0)
    def _():
        for copy in weight_copies(be_ref[0], 0):
            copy.start()

    @pl.when((b < nu) & (first_ref[b] == 1))
    def _():
        e = be_ref[b]
        nxt = next_ref[b]
        for slot in range(2):
            @pl.when(slot_ref[b] == slot)
            def _():
                for copy in weight_copies(e, slot):
                    copy.wait()

                @pl.when(nxt >= 0)
                def _():
                    for copy in weight_copies(nxt, 1 - slot):
                        copy.start()

                wg_s[...] = wg_f[slot].astype(BF16)
                wu_s[...] = wu_f[slot].astype(BF16)
                wd_s[...] = wd_f[slot].astype(BF16)

    @pl.when(b < nu)
    def _():
        h = (xb_ref[...] * (1.0 + sc_ref[...]) + sh_ref[...]).astype(BF16)
        a = jnp.dot(h, wg_s[...], preferred_element_type=F32)
        u = jnp.dot(h, wu_s[...], preferred_element_type=F32)
        act = (a / (1.0 + jnp.exp(-a))) * u
        yb_ref[...] = jnp.dot(act.astype(BF16), wd_s[...], preferred_element_type=F32)

    @pl.when(b >= nu)
    def _():
        yb_ref[...] = jnp.zeros_like(yb_ref)


def _experts(xb, sc2, sh2, w_g, w_u, w_d, plan, layer):
    P, D = xb.shape
    DB = DISPATCH_BLOCK
    nblk = P // DB
    blk = lambda b, be, nu, *_: (jnp.minimum(b, nu[0] - 1), 0)
    vec = pl.BlockSpec((1, D), lambda b, *_: (0, 0))
    hbm = pl.BlockSpec(memory_space=pl.ANY)
    return pl.pallas_call(
        functools.partial(_expert_body, layer=layer),
        grid_spec=pltpu.PrefetchScalarGridSpec(
            num_scalar_prefetch=5,
            grid=(nblk,),
            in_specs=[pl.BlockSpec((DB, D), blk), vec, vec, hbm, hbm, hbm],
            out_specs=pl.BlockSpec((DB, D), lambda b, *_: (b, 0)),
            scratch_shapes=[pltpu.VMEM((2, D, D_EXPERT), F32), pltpu.VMEM((2, D, D_EXPERT), F32),
                            pltpu.VMEM((2, D_EXPERT, D), F32),
                            pltpu.VMEM((D, D_EXPERT), BF16), pltpu.VMEM((D, D_EXPERT), BF16),
                            pltpu.VMEM((D_EXPERT, D), BF16),
                            pltpu.SemaphoreType.DMA((2, 3))]),
        out_shape=jax.ShapeDtypeStruct((P, D), F32),
        compiler_params=_params(("arbitrary",)),
        name="moe_experts",
    )(*plan, xb, sc2, sh2, w_g, w_u, w_d)


def _combine_body(dest_ref, x1_ref, wt_ref, g2_ref, lng_ref, lnb_ref, yb_hbm, o_ref,
                  buf0, buf1, sem, *, tm, alpha):
    i = pl.program_id(0)
    bufs = (buf0, buf1)

    def row_copy(d, k, r):
        return pltpu.make_async_copy(yb_hbm.at[pl.ds(d, 1)], bufs[k].at[pl.ds(r, 1)], sem)

    def issue(g, carry):
        for u in range(ROW_UNROLL):
            r = g * ROW_UNROLL + u
            t = i * tm + r
            for k in range(2):
                row_copy(dest_ref[2 * t + k], k, r).start()
        return carry

    def wait(g, carry):
        for _ in range(ROW_UNROLL):
            for k in range(2):
                row_copy(0, k, 0).wait()
        return carry

    lax.fori_loop(0, tm // ROW_UNROLL, issue, 0)
    lax.fori_loop(0, tm // ROW_UNROLL, wait, 0)
    wt = wt_ref[...]
    y = buf0[...] * wt[:, 0:1] + buf1[...] * wt[:, 1:2]
    o_ref[...] = _layer_norm(alpha * x1_ref[...] + g2_ref[...] * y, lng_ref[...], lnb_ref[...])


def _combine_ln(x1, yb, dest, w_top, g2, ln_g, ln_b, alpha):
    T, D = x1.shape
    tm = min(256, T)
    row = pl.BlockSpec((tm, D), lambda i, d: (i, 0))
    vec = pl.BlockSpec((1, D), lambda i, d: (0, 0))
    return pl.pallas_call(
        functools.partial(_combine_body, tm=tm, alpha=alpha),
        grid_spec=pltpu.PrefetchScalarGridSpec(
            num_scalar_prefetch=1,
            grid=(T // tm,),
            in_specs=[row, pl.BlockSpec((tm, 2), lambda i, d: (i, 0)), vec, vec, vec,
                      pl.BlockSpec(memory_space=pl.ANY)],
            out_specs=row,
            scratch_shapes=[pltpu.VMEM((tm, D), F32), pltpu.VMEM((tm, D), F32),
                            pltpu.SemaphoreType.DMA(())]),
        out_shape=jax.ShapeDtypeStruct((T, D), F32),
        compiler_params=_params(("arbitrary",)),
        name="moe_combine_ln",
    )(dest, x1, w_top, g2, ln_g, ln_b, yb)


def _route(logits):
    T = logits.shape[0]
    DB = DISPATCH_BLOCK
    group_logits = logits[:, :N_GROUPS]
    exp_logits = logits[:, N_GROUPS:N_GROUPS + N_EXPERTS].reshape(T, N_GROUPS, EXPERTS_PER_GROUP)
    group_prob = jax.nn.softmax(group_logits, axis=-1)
    g_top = jnp.argmax(group_logits, axis=-1)
    p_group = jnp.take_along_axis(group_prob, g_top[:, None], axis=-1)
    in_group = jnp.take_along_axis(exp_logits, g_top[:, None, None], axis=1)[:, 0]
    top_val, top_idx = lax.top_k(in_group, 2)
    w_top = jax.nn.softmax(top_val, axis=-1) * p_group
    expert_id = g_top[:, None].astype(jnp.int32) * EXPERTS_PER_GROUP + top_idx.astype(jnp.int32)

    member = (expert_id[:, :, None] == jnp.arange(N_EXPERTS, dtype=jnp.int32)).any(axis=1).astype(jnp.int32)
    incl = jnp.cumsum(member, axis=0)
    counts = incl[-1]
    rank = jnp.take_along_axis(incl - member, expert_id, axis=1)
    padded = (counts + DB - 1) // DB * DB
    pend = jnp.cumsum(padded)
    pstart = pend - padded
    dest = (pstart[expert_id] + rank).astype(jnp.int32)
    P = ((2 * T + DB - 1) // DB) * DB + N_EXPERTS * DB
    nblk = P // DB
    block_e = jnp.minimum(jnp.searchsorted(pend, jnp.arange(nblk, dtype=jnp.int32) * DB, side='right'),
                          N_EXPERTS - 1).astype(jnp.int32)
    n_used = (pend[-1:] // DB).astype(jnp.int32)
    b_idx = jnp.arange(nblk, dtype=jnp.int32)
    first = (b_idx < n_used) & ((b_idx == 0) | (block_e != jnp.roll(block_e, 1)))
    slot = (jnp.cumsum(first.astype(jnp.int32)) - 1) % 2
    after = (pend // DB).astype(jnp.int32)[block_e]
    next_e = jnp.where(after < n_used, block_e[jnp.minimum(after, nblk - 1)], -1)
    plan = (block_e, n_used, first.astype(jnp.int32), slot.astype(jnp.int32), next_e.astype(jnp.int32))
    return dest.reshape(-1), w_top, plan, P


def kernel(x, c, w_ada, b_ada, w_in, w_branch_gate, b_branch_gate, attn_sinks, lambda_q1, lambda_k1,
           lambda_q2, lambda_k2, subln_g, w_branch, w_out, ln1_g, ln1_b, w_router_group, b_router_group,
           w_router_expert, b_router_expert, w_exp_gate, w_exp_up, w_exp_down, ln2_g, ln2_b):
    B, S, D = x.shape
    assert B == 1 and D == D_MODEL
    depth = w_in.shape[0]
    alpha = ALPHA
    xs = x.reshape(S, D)
    mod = _ada_mod(c, w_ada, b_ada)
    diff_slopes = jnp.exp2(-8.0 * jnp.arange(1, DIFF_HEADS + 1, dtype=F32) / DIFF_HEADS)
    diff_slopes = jnp.stack([diff_slopes, 1.0 / diff_slopes])
    zero_bias = jnp.zeros((1, w_in.shape[2]), F32)
    pad = LANES - N_GROUPS - N_EXPERTS
    w_eg = w_exp_gate.reshape(depth * N_EXPERTS, D, D_EXPERT)
    w_eu = w_exp_up.reshape(depth * N_EXPERTS, D, D_EXPERT)
    w_ed = w_exp_down.reshape(depth * N_EXPERTS, D_EXPERT, D)
    for l in range(depth):
        lambda_init = 0.8 - 0.6 * math.exp(-0.3 * l)
        sh1, sc1, g1, sh2, sc2, g2 = [mod[l, :, n * D:(n + 1) * D] for n in range(6)]
        proj = _mod_matmul(xs, sc1, sh1, w_in, l, zero_bias, sigmoid=False, name="in_proj")
        gates = _mod_matmul(xs, sc1, sh1, w_branch_gate, l, b_branch_gate[l].reshape(1, -1),
                            sigmoid=True, name="branch_gates")
        o_a = _swa_attention(proj, attn_sinks[l])
        o_b = _sb_attention(proj)
        o_c = _diff_attention(proj, diff_slopes, lambda_q1[l], lambda_k1[l], lambda_q2[l], lambda_k2[l],
                              subln_g[l], lambda_init)
        z = _branch_merge(o_a, o_b, o_c, w_branch, l, gates)
        w_router = jnp.pad(jnp.concatenate([w_router_group[l], w_router_expert[l]], axis=1), ((0, 0), (0, pad)))
        b_router = jnp.pad(jnp.concatenate([b_router_group[l], b_router_expert[l]]), (0, pad)).reshape(1, LANES)
        x1, logits = _mixer_out(z, w_out[l].astype(BF16), xs, g1, ln1_g[l].reshape(1, D), ln1_b[l].reshape(1, D),
                                sc2, sh2, w_router, b_router, alpha)
        dest, w_top, plan, P = _route(logits)
        xb = _dispatch(x1, dest, P)
        yb = _experts(xb, sc2, sh2, w_eg, w_eu, w_ed, plan, l)
        xs = _combine_ln(x1, yb, dest, w_top, g2, ln2_g[l].reshape(1, D), ln2_b[l].reshape(1, D), alpha)
    return xs.reshape(B, S, D)
```

```python
import functools
import math

import jax
import jax.numpy as jnp
from jax import lax
from jax.experimental import pallas as pl
from jax.experimental.pallas import tpu as pltpu

F32 = jnp.float32
BF16 = jnp.bfloat16

D_MODEL = 2048
HEAD_DIM = 64
SWA_Q_HEADS = 16
SWA_KV_HEADS = 4
WINDOW = 128
SB_HEADS = 16
DIFF_HEADS = 8
BLOCK_Q = 128
BRANCH_WIDTH = 1024
A_KV = 256
N_BRANCHES = 3
N_GROUPS = 8
EXPERTS_PER_GROUP = 8
N_EXPERTS = 64
D_EXPERT = 384
DISPATCH_BLOCK = 128
DEPTH = 4
ALPHA = (2.0 * DEPTH) ** 0.25
LN_EPS = 1e-5
RMS_EPS = 1e-5
QK_SCALE = 1.0 / math.sqrt(HEAD_DIM)

COL_QA, COL_KA, COL_VA = 0, 1024, 1280
COL_QB, COL_KB, COL_VB = 1536, 2560, 3584
COL_QC, COL_KC, COL_VC = 4608, 5632, 6656

LANES = 128
VMEM_LIMIT = 56 * 1024 * 1024
NEG = -1e30
SB_EXIT = 88.0
ALIBI_CUT = 100.0
POS_SPLIT = 64
SUM_ROWS = 16
ROW_UNROLL = 8


def _params(sem):
    return pltpu.CompilerParams(dimension_semantics=sem, vmem_limit_bytes=VMEM_LIMIT)


def _ada_body(c_ref, w_ref, b_ref, o_ref):
    o_ref[0] = jnp.sum(c_ref[...] * w_ref[0], axis=0, keepdims=True) + b_ref[0]


def _ada_mod(c, w_ada, b_ada):
    L, D, N = w_ada.shape
    tn = 512
    return pl.pallas_call(
        _ada_body,
        grid=(L, N // tn),
        in_specs=[pl.BlockSpec((D, 1), lambda l, j: (0, 0)),
                  pl.BlockSpec((1, D, tn), lambda l, j: (l, 0, j)),
                  pl.BlockSpec((1, 1, tn), lambda l, j: (l, 0, j))],
        out_specs=pl.BlockSpec((1, 1, tn), lambda l, j: (l, 0, j)),
        out_shape=jax.ShapeDtypeStruct((L, 1, N), F32),
        compiler_params=_params(("arbitrary", "arbitrary")),
        name="ada_mod",
    )(c.reshape(D, 1), w_ada, b_ada.reshape(L, 1, N))


def _modmm_body(x_ref, sc_ref, sh_ref, w_ref, b_ref, o_ref, h_ref, *, sigmoid):
    @pl.when(pl.program_id(1) == 0)
    def _():
        h_ref[...] = (x_ref[...] * (1.0 + sc_ref[...]) + sh_ref[...]).astype(BF16)

    acc = jnp.dot(h_ref[...], w_ref[0].astype(BF16), preferred_element_type=F32) + b_ref[...]
    if sigmoid:
        acc = 1.0 / (1.0 + jnp.exp(-acc))
    o_ref[...] = acc.astype(o_ref.dtype)


def _mod_matmul(x, sc, sh, w, layer, b, *, sigmoid, name):
    M, K = x.shape
    N = w.shape[2]
    tm = min(1024, M)
    tn = 768
    return pl.pallas_call(
        functools.partial(_modmm_body, sigmoid=sigmoid),
        grid=(M // tm, N // tn),
        in_specs=[pl.BlockSpec((tm, K), lambda i, j: (i, 0)),
                  pl.BlockSpec((1, K), lambda i, j: (0, 0)),
                  pl.BlockSpec((1, K), lambda i, j: (0, 0)),
                  pl.BlockSpec((1, K, tn), lambda i, j: (layer, 0, j)),
                  pl.BlockSpec((1, tn), lambda i, j: (0, j))],
        out_specs=pl.BlockSpec((tm, tn), lambda i, j: (i, j)),
        out_shape=jax.ShapeDtypeStruct((M, N), BF16),
        scratch_shapes=[pltpu.VMEM((tm, K), BF16)],
        compiler_params=_params(("arbitrary", "arbitrary")),
        name=name,
    )(x, sc, sh, w, b)


def _swa_body(sinks_ref, q_ref, kp_ref, kc_ref, vp_ref, vc_ref, o_ref):
    i = pl.program_id(0)
    k = jnp.concatenate([kp_ref[...], kc_ref[...]], axis=0)
    v = jnp.concatenate([vp_ref[...], vc_ref[...]], axis=0)
    qi = lax.broadcasted_iota(jnp.int32, (BLOCK_Q, 2 * BLOCK_Q), 0)
    ki = lax.broadcasted_iota(jnp.int32, (BLOCK_Q, 2 * BLOCK_Q), 1)
    dist = qi + BLOCK_Q - ki
    valid = (dist >= 0) & (dist < WINDOW) & ((ki >= BLOCK_Q) | (i > 0))
    distf = dist.astype(F32)
    group = SWA_Q_HEADS // SWA_KV_HEADS
    outs = []
    for h in range(SWA_Q_HEADS):
        g = h // group
        slope = 2.0 ** (-8.0 * (h + 1) / SWA_Q_HEADS)
        qh = q_ref[:, h * HEAD_DIM:(h + 1) * HEAD_DIM]
        kg = k[:, g * HEAD_DIM:(g + 1) * HEAD_DIM]
        vg = v[:, g * HEAD_DIM:(g + 1) * HEAD_DIM]
        s = lax.dot_general(qh, kg, (((1,), (1,)), ((), ())), preferred_element_type=F32) * QK_SCALE
        s = jnp.where(valid, s - slope * distf, NEG)
        sink = sinks_ref[h]
        m = jnp.maximum(jnp.max(s, axis=1, keepdims=True), sink)
        p = jnp.where(valid, jnp.exp(s - m), 0.0)
        denom = jnp.sum(p, axis=1, keepdims=True) + jnp.exp(sink - m)
        o = jnp.dot(p.astype(BF16), vg, preferred_element_type=F32) / denom
        outs.append(o)
    o_ref[...] = jnp.concatenate(outs, axis=1).astype(o_ref.dtype)


def _swa_attention(proj, sinks):
    S = proj.shape[0]
    nb = S // BLOCK_Q
    kv_blk = lambda col: pl.BlockSpec((BLOCK_Q, A_KV), lambda i: (i, col // A_KV))
    kv_prev = lambda col: pl.BlockSpec((BLOCK_Q, A_KV), lambda i: (jnp.maximum(i - 1, 0), col // A_KV))
    return pl.pallas_call(
        _swa_body,
        grid=(nb,),
        in_specs=[pl.BlockSpec(memory_space=pltpu.SMEM),
                  pl.BlockSpec((BLOCK_Q, BRANCH_WIDTH), lambda i: (i, COL_QA // BRANCH_WIDTH)),
                  kv_prev(COL_KA), kv_blk(COL_KA), kv_prev(COL_VA), kv_blk(COL_VA)],
        out_specs=pl.BlockSpec((BLOCK_Q, BRANCH_WIDTH), lambda i: (i, 0)),
        out_shape=jax.ShapeDtypeStruct((S, BRANCH_WIDTH), BF16),
        compiler_params=_params(("arbitrary",)),
        name="swa_attention",
    )(sinks, proj, proj, proj, proj, proj)


def _sb_body(q_ref, k_ref, v_ref, o_ref, vt_ref, later_ref, carry_ref, acc_ref, *, tq):
    g = pl.program_id(1)
    nkb = vt_ref.shape[0]
    lane = lax.broadcasted_iota(jnp.int32, (1, LANES), 1)

    @pl.when(g == 0)
    def _():
        def setup(c, carry):
            off = pl.multiple_of(c * tq, tq)
            vt_ref[c] = v_ref[pl.ds(off, tq), :].astype(F32).T.astype(BF16)
            return carry

        lax.fori_loop(0, nkb, setup, 0)
        this_key = lax.broadcasted_iota(jnp.int32, later_ref.shape, 0)
        other_key = lax.broadcasted_iota(jnp.int32, later_ref.shape, 1)
        later_ref[...] = (other_key > this_key).astype(BF16)

    qs = q_ref[...] * QK_SCALE
    acc_ref[...] = jnp.zeros_like(acc_ref)

    def stream(tile, half, first, carry):
        in_half = (lane >= HEAD_DIM) if half else (lane < HEAD_DIM)
        q_tile = qs[tile * tq:(tile + 1) * tq, :]
        return dict(qm=jnp.where(in_half, q_tile, jnp.zeros_like(q_tile)), tile=tile, half=half,
                    first=first, carry=carry)

    def process(streams, nb, masked):
        n = nb * tq
        later = later_ref[...]
        if masked:
            k_row = lax.broadcasted_iota(jnp.int32, (n, tq), 0)
            q_col = lax.broadcasted_iota(jnp.int32, (n, tq), 1)
            before = k_row < q_col + (nb - 1) * tq
        zs = []
        for st in streams:
            off = pl.multiple_of(st["first"] * tq, tq)
            zs.append(lax.dot_general(k_ref[pl.ds(off, n), :], st["qm"], (((1,), (1,)), ((), ())),
                                      preferred_element_type=F32))
        log_rems, splits = [], []
        for z in zs:
            soft = jnp.log(1.0 + jnp.exp(-jnp.abs(z)))
            log_rem = -(jnp.maximum(z, 0.0) + soft)
            if masked:
                log_rem = jnp.where(before, log_rem, 0.0)
            hi = log_rem.astype(BF16)
            lo = (log_rem - hi.astype(F32)).astype(BF16)
            log_rems.append(log_rem)
            splits.append((hi, lo))
        afters, totals = [], []
        for (hi, lo), log_rem in zip(splits, log_rems):
            parts, total = [None] * nb, jnp.zeros((1, tq), F32)
            for b in reversed(range(nb)):
                blk = slice(b * tq, (b + 1) * tq)
                inside = (jnp.dot(later, hi[blk, :], preferred_element_type=F32)
                          + jnp.dot(later, lo[blk, :], preferred_element_type=F32))
                parts[b] = inside + total
                total = total + inside[0:1, :] + log_rem[b * tq:b * tq + 1, :]
            afters.append(parts[0] if nb == 1 else jnp.concatenate(parts, axis=0))
            totals.append(total)
        new_carries = []
        for st, z, log_rem, after, total in zip(streams, zs, log_rems, afters, totals):
            w = jnp.exp((log_rem + z) + (after + st["carry"]))
            if masked:
                w = jnp.where(before, w, 0.0)
            w = w.astype(BF16)
            rows = slice(st["half"] * HEAD_DIM, (st["half"] + 1) * HEAD_DIM)
            for b in range(nb):
                acc_ref[st["tile"], st["half"]] += jnp.dot(vt_ref[st["first"] + b][rows, :],
                                                           w[b * tq:(b + 1) * tq, :], preferred_element_type=F32)
            new_carries.append(st["carry"] + total)
        return new_carries

    zero = jnp.zeros((1, tq), F32)
    tiles_heads = [(tile, half) for tile in range(2) for half in range(2)]

    def keep(carries, pairs):
        for (tile, half), carry in zip(pairs, carries):
            carry_ref[tile, half] = carry

    @pl.when(g == 0)
    def _():
        keep(process([stream(0, half, 0, zero) for half in range(2)], 1, True), tiles_heads[:2])
        keep(process([stream(1, half, 0, zero) for half in range(2)], 2, True), tiles_heads[2:])

    @pl.when(g > 0)
    def _():
        keep(process([stream(tile, half, 2 * g + tile - 1, zero) for tile, half in tiles_heads], 2, True),
             tiles_heads)

    def top(carries):
        return jnp.maximum(jnp.max(carries[0]), jnp.max(carries[1]))

    for tile in range(2):
        carries = (carry_ref[tile, 0], carry_ref[tile, 1])

        def cond(state):
            j, _, best = state
            return (j >= 0) & (best > -SB_EXIT)

        def body(state, tile=tile):
            j, carries, _ = state
            carries = tuple(process([stream(tile, half, j, carries[half]) for half in range(2)], 1, False))
            return j - 1, carries, top(carries)

        lax.while_loop(cond, body, (2 * g + tile - 2, carries, top(carries)))
        o_ref[tile * tq:(tile + 1) * tq, :] = jnp.concatenate(
            [acc_ref[tile, 0], acc_ref[tile, 1]], axis=0).T.astype(o_ref.dtype)


def _sb_attention(proj):
    S = proj.shape[0]
    tq = min(256, S // 2)
    npair = BRANCH_WIDTH // LANES
    return pl.pallas_call(
        functools.partial(_sb_body, tq=tq),
        grid=(npair, S // (2 * tq)),
        in_specs=[pl.BlockSpec((2 * tq, LANES), lambda p, g: (g, COL_QB // LANES + p)),
                  pl.BlockSpec((S, LANES), lambda p, g: (0, COL_KB // LANES + p)),
                  pl.BlockSpec((S, LANES), lambda p, g: (0, COL_VB // LANES + p))],
        out_specs=pl.BlockSpec((2 * tq, LANES), lambda p, g: (g, p)),
        out_shape=jax.ShapeDtypeStruct((S, BRANCH_WIDTH), BF16),
        scratch_shapes=[pltpu.VMEM((S // tq, LANES, tq), BF16),
                        pltpu.VMEM((tq, tq), BF16),
                        pltpu.VMEM((2, 2, 1, tq), F32),
                        pltpu.VMEM((2, 2, HEAD_DIM, tq), F32)],
        compiler_params=_params(("arbitrary", "arbitrary")),
        name="stickbreak_attention",
    )(proj, proj, proj)


def _diff_body(slopes_ref, q_ref, k_ref, v_ref, lq1_ref, lk1_ref, lq2_ref, lk2_ref, g_ref, o_ref,
               kaug_ref, vt_ref, knorm_ref, qa_ref, sta_ref, stb_ref, m_ref, acc_ref, *, tq, lambda_init):
    h = pl.program_id(0)
    i = pl.program_id(1)
    slope = slopes_ref[0, h]
    inv_slope = slopes_ref[1, h]
    nkb = kaug_ref.shape[0] // tq
    lane = lax.broadcasted_iota(jnp.int32, (1, LANES), 1)

    def max_sq_norms(x):
        sq = x.astype(F32)
        sq = sq * sq
        first = jnp.sum(jnp.where(lane < HEAD_DIM, sq, 0.0), axis=1, keepdims=True)
        second = jnp.sum(jnp.where(lane >= HEAD_DIM, sq, 0.0), axis=1, keepdims=True)
        return jnp.max(first, axis=0, keepdims=True), jnp.max(second, axis=0, keepdims=True)

    @pl.when(i == 0)
    def _():
        def setup(c, carry):
            off = pl.multiple_of(c * tq, tq)
            n1, n2 = max_sq_norms(k_ref[pl.ds(off, tq), :])
            carry = (jnp.maximum(carry[0], n1), jnp.maximum(carry[1], n2))
            pos = off + lax.broadcasted_iota(jnp.int32, (tq, LANES), 0)
            lanes = lax.broadcasted_iota(jnp.int32, (tq, LANES), 1)
            coarse = (pos // POS_SPLIT) * POS_SPLIT
            posm = jnp.where(lanes == 0, coarse, jnp.where(lanes == 1, pos - coarse, 0))
            kaug_ref[pl.ds(off, tq), 0:LANES] = k_ref[pl.ds(off, tq), :]
            kaug_ref[pl.ds(off, tq), LANES:2 * LANES] = posm.astype(F32).astype(BF16)
            vt_ref[c, 0:LANES, :] = v_ref[pl.ds(off, tq), :].astype(F32).T.astype(BF16)
            ones_row = lax.broadcasted_iota(jnp.int32, (SUM_ROWS, tq), 0) == 0
            vt_ref[c, LANES:LANES + SUM_ROWS, :] = jnp.where(ones_row, 1.0, 0.0).astype(BF16)
            return carry

        zero = jnp.zeros((1, 1), F32)
        knorm_ref[0], knorm_ref[1] = lax.fori_loop(0, nkb, setup, (zero, zero))

    q1n, q2n = max_sq_norms(q_ref[...])
    qk = jnp.sqrt(jnp.maximum(q1n * knorm_ref[0], q2n * knorm_ref[1]))
    reach = (ALIBI_CUT + 2.0 * QK_SCALE * qk) * inv_slope
    keep = jnp.minimum(jnp.floor((reach - 1.0) * (1.0 / tq)) + 1.0, 1e6).astype(jnp.int32)
    n_off = jnp.clip(jnp.max(keep), 0, i)
    j0 = i - n_off

    qs = q_ref[...] * QK_SCALE
    bias_cols = jnp.broadcast_to(jnp.where(lane < 2, slope, 0.0).astype(BF16), (tq, LANES))
    for half in range(2):
        in_half = (lane >= HEAD_DIM) if half else (lane < HEAD_DIM)
        qa_ref[half] = jnp.concatenate([jnp.where(in_half, qs, jnp.zeros_like(qs)), bias_cols], axis=1)
    acc_ref[...] = jnp.zeros_like(acc_ref)
    m_ref[...] = jnp.full_like(m_ref, NEG)
    on_or_below_diagonal = (lax.broadcasted_iota(jnp.int32, (tq, tq), 0)
                            <= lax.broadcasted_iota(jnp.int32, (tq, tq), 1))

    def scores(jb, st_ref):
        off = pl.multiple_of(jb * tq, tq)
        kb = kaug_ref[pl.ds(off, tq), :]
        for half in range(2):
            st_ref[half] = lax.dot_general(kb, qa_ref[half], (((1,), (1,)), ((), ())),
                                           preferred_element_type=F32)

    def softmax_pv(jb, st_ref, diagonal):
        vtb = vt_ref[jb]
        alphas, ps = [], []
        for half in range(2):
            st = st_ref[half]
            if diagonal:
                st = jnp.where(on_or_below_diagonal, st, NEG)
            m_old = m_ref[half]
            m_new = jnp.maximum(m_old, jnp.max(st, axis=0, keepdims=True))
            alpha = jnp.exp(m_old - m_new)
            p = jnp.exp(st - m_new)
            m_ref[half] = m_new
            alphas.append(alpha)
            ps.append(p.astype(BF16))
        for half in range(2):
            acc_ref[half] = alphas[half] * acc_ref[half] + jnp.dot(vtb, ps[half], preferred_element_type=F32)

    scores(j0, sta_ref)

    def pair(p, carry):
        scores(j0 + 2 * p + 1, stb_ref)
        softmax_pv(j0 + 2 * p, sta_ref, False)
        scores(j0 + 2 * p + 2, sta_ref)
        softmax_pv(j0 + 2 * p + 1, stb_ref, False)
        return carry

    lax.fori_loop(0, n_off // 2, pair, 0)

    @pl.when(n_off % 2 == 0)
    def _():
        softmax_pv(i, sta_ref, True)

    @pl.when(n_off % 2 == 1)
    def _():
        scores(i, stb_ref)
        softmax_pv(i - 1, sta_ref, False)
        softmax_pv(i, stb_ref, True)

    lam = (jnp.exp(jnp.sum(lq1_ref[...] * lk1_ref[...], axis=1, keepdims=True))
           - jnp.exp(jnp.sum(lq2_ref[...] * lk2_ref[...], axis=1, keepdims=True)) + lambda_init)
    o = (acc_ref[0, 0:LANES, :] / acc_ref[0, LANES:LANES + 1, :]
         - lam * (acc_ref[1, 0:LANES, :] / acc_ref[1, LANES:LANES + 1, :]))
    y = o * lax.rsqrt(jnp.mean(o * o, axis=0, keepdims=True) + RMS_EPS) * g_ref[...]
    o_ref[...] = (y * (1.0 - lambda_init)).T.astype(o_ref.dtype)


def _diff_attention(proj, slopes, lq1, lk1, lq2, lk2, subln_g, lambda_init):
    S = proj.shape[0]
    tq = min(512, S)
    vec = lambda n: pl.BlockSpec((1, n), lambda h, i: (0, 0))
    return pl.pallas_call(
        functools.partial(_diff_body, tq=tq, lambda_init=lambda_init),
        grid=(DIFF_HEADS, S // tq),
        in_specs=[pl.BlockSpec(memory_space=pltpu.SMEM),
                  pl.BlockSpec((tq, LANES), lambda h, i: (i, COL_QC // LANES + h)),
                  pl.BlockSpec((S, LANES), lambda h, i: (0, COL_KC // LANES + h)),
                  pl.BlockSpec((S, LANES), lambda h, i: (0, COL_VC // LANES + h)),
                  vec(HEAD_DIM), vec(HEAD_DIM), vec(HEAD_DIM), vec(HEAD_DIM),
                  pl.BlockSpec((2 * HEAD_DIM, 1), lambda h, i: (0, 0))],
        out_specs=pl.BlockSpec((tq, LANES), lambda h, i: (i, h)),
        out_shape=jax.ShapeDtypeStruct((S, BRANCH_WIDTH), BF16),
        scratch_shapes=[pltpu.VMEM((S, 2 * LANES), BF16),
                        pltpu.VMEM((S // tq, LANES + SUM_ROWS, tq), BF16),
                        pltpu.VMEM((2, 1, 1), F32),
                        pltpu.VMEM((2, tq, 2 * LANES), BF16),
                        pltpu.VMEM((2, tq, tq), F32), pltpu.VMEM((2, tq, tq), F32),
                        pltpu.VMEM((2, 1, tq), F32),
                        pltpu.VMEM((2, LANES + SUM_ROWS, tq), F32)],
        compiler_params=_params(("arbitrary", "arbitrary")),
        name="diff_attention",
    )(slopes, proj, proj, proj, lq1.reshape(1, -1), lk1.reshape(1, -1), lq2.reshape(1, -1),
      lk2.reshape(1, -1), subln_g.reshape(-1, 1))


def _branch_body(oa_ref, ob_ref, oc_ref, w_ref, ga_ref, gb_ref, gc_ref, z_ref):
    z = None
    for n, (o_ref, g_ref) in enumerate(((oa_ref, ga_ref), (ob_ref, gb_ref), (oc_ref, gc_ref))):
        y = jnp.dot(o_ref[...], w_ref[0, n].astype(BF16), preferred_element_type=F32)
        t = g_ref[...].astype(F32) * y
        z = t if z is None else z + t
    z_ref[...] = z.astype(z_ref.dtype)


def _branch_merge(o_a, o_b, o_c, w_branch, layer, gates):
    S = o_a.shape[0]
    D = w_branch.shape[3]
    tm = min(1024, S)
    tn = 512
    nj = D // tn
    o_spec = pl.BlockSpec((tm, BRANCH_WIDTH), lambda i, j: (i, 0))
    gate_spec = lambda n: pl.BlockSpec((tm, tn), lambda i, j: (i, n * nj + j))
    return pl.pallas_call(
        _branch_body,
        grid=(S // tm, nj),
        in_specs=[o_spec, o_spec, o_spec,
                  pl.BlockSpec((1, N_BRANCHES, BRANCH_WIDTH, tn), lambda i, j: (layer, 0, 0, j)),
                  gate_spec(0), gate_spec(1), gate_spec(2)],
        out_specs=pl.BlockSpec((tm, tn), lambda i, j: (i, j)),
        out_shape=jax.ShapeDtypeStruct((S, D), BF16),
        compiler_params=_params(("arbitrary", "arbitrary")),
        name="branch_merge",
    )(o_a, o_b, o_c, w_branch, gates, gates, gates)


def _layer_norm(r, g, b):
    mu = jnp.mean(r, axis=1, keepdims=True)
    d = r - mu
    var = jnp.mean(d * d, axis=1, keepdims=True)
    return d * lax.rsqrt(var + LN_EPS) * g + b


def _split_bf16(a):
    hi = a.astype(BF16)
    return hi, (a - hi.astype(F32)).astype(BF16)


ROUTE_EXPERT, ROUTE_WEIGHT, ROUTE_RANK = 0, 2, 4


def _route_tile(logits, counts):
    rows = logits.shape[0]
    lane = lax.broadcasted_iota(jnp.int32, logits.shape, 1)
    row_min = lambda cond: jnp.min(jnp.where(cond, lane, LANES), axis=1, keepdims=True)
    row_max = lambda cond: jnp.max(jnp.where(cond, logits, NEG), axis=1, keepdims=True)
    is_group = lane < N_GROUPS
    g_max = row_max(is_group)
    g_top = row_min(is_group & (logits == g_max))
    p_group = 1.0 / jnp.sum(jnp.where(is_group, jnp.exp(logits - g_max), 0.0), axis=1, keepdims=True)
    e_lane = lane - N_GROUPS
    in_group = ((e_lane >= 0) & (e_lane < N_EXPERTS)
                & (jnp.right_shift(e_lane, EXPERTS_PER_GROUP.bit_length() - 1) == g_top))
    v1 = row_max(in_group)
    i1 = row_min(in_group & (logits == v1))
    rest = in_group & (lane != i1)
    v2 = row_max(rest)
    i2 = row_min(rest & (logits == v2))
    ratio = jnp.exp(v2 - v1)
    w1 = p_group / (1.0 + ratio)
    w2 = w1 * ratio
    member = (lane == i1) | (lane == i2)
    earlier = (lax.broadcasted_iota(jnp.int32, (rows, rows), 1)
               < lax.broadcasted_iota(jnp.int32, (rows, rows), 0)).astype(BF16)
    before = jnp.dot(earlier, member.astype(BF16), preferred_element_type=F32) + counts
    pick = lambda idx: jnp.sum(jnp.where(lane == idx, before, 0.0), axis=1, keepdims=True)
    record = jnp.zeros(logits.shape, F32)
    for at, value in ((ROUTE_EXPERT, (i1 - N_GROUPS).astype(F32)), (ROUTE_EXPERT + 1, (i2 - N_GROUPS).astype(F32)),
                      (ROUTE_WEIGHT, w1), (ROUTE_WEIGHT + 1, w2),
                      (ROUTE_RANK, pick(i1)), (ROUTE_RANK + 1, pick(i2))):
        record = jnp.where(lane == at, value, record)
    return record, counts + jnp.sum(member.astype(F32), axis=0, keepdims=True)


def _out_body(z_ref, w_ref, x_ref, g1_ref, lng_ref, lnb_ref, sc2_ref, sh2_ref, wr_hi_ref, wr_lo_ref, br_ref,
              x1_ref, route_ref, counts_ref, running_ref, *, alpha):
    @pl.when(pl.program_id(0) == 0)
    def _():
        running_ref[...] = jnp.zeros_like(running_ref)

    y = jnp.dot(z_ref[...], w_ref[...], preferred_element_type=F32)
    x1 = _layer_norm(alpha * x_ref[...] + g1_ref[...] * y, lng_ref[...], lnb_ref[...])
    x1_ref[...] = x1
    h_hi, h_lo = _split_bf16(x1 * (1.0 + sc2_ref[...]) + sh2_ref[...])
    w_hi = wr_hi_ref[...]
    logits = (jnp.dot(h_hi, w_hi, preferred_element_type=F32)
              + (jnp.dot(h_hi, wr_lo_ref[...], preferred_element_type=F32)
                 + jnp.dot(h_lo, w_hi, preferred_element_type=F32))) + br_ref[...]
    route_ref[...], counts = _route_tile(logits, running_ref[...])
    running_ref[...] = counts
    counts_ref[...] = jnp.broadcast_to(counts, counts_ref.shape)


def _mixer_out(z, w_out, x, g1, ln_g, ln_b, sc2, sh2, w_router, b_router, alpha):
    S, D = x.shape
    tm = min(256, S)
    row = pl.BlockSpec((tm, D), lambda i: (i, 0))
    vec = pl.BlockSpec((1, D), lambda i: (0, 0))
    router = pl.BlockSpec((D, LANES), lambda i: (0, 0))
    wr_hi, wr_lo = _split_bf16(w_router)
    return pl.pallas_call(
        functools.partial(_out_body, alpha=alpha),
        grid=(S // tm,),
        in_specs=[row, pl.BlockSpec((D, D), lambda i: (0, 0)), row, vec, vec, vec, vec, vec,
                  router, router, pl.BlockSpec((1, LANES), lambda i: (0, 0))],
        out_specs=[row, pl.BlockSpec((tm, LANES), lambda i: (i, 0)), pl.BlockSpec((8, LANES), lambda i: (0, 0))],
        out_shape=[jax.ShapeDtypeStruct((S, D), F32), jax.ShapeDtypeStruct((S, LANES), F32),
                   jax.ShapeDtypeStruct((8, LANES), F32)],
        scratch_shapes=[pltpu.VMEM((1, LANES), F32)],
        compiler_params=_params(("arbitrary",)),
        name="mixer_out_ln",
    )(z, w_out, x, g1, ln_g, ln_b, sc2, sh2, wr_hi, wr_lo, b_router)


def _dispatch_body(dest_ref, x_ref, xb_in_hbm, xb_hbm, sem, *, rows_per_step):
    del xb_in_hbm
    i = pl.program_id(0)

    def row_copy(r, d):
        return pltpu.make_async_copy(x_ref.at[pl.ds(r, 1)], xb_hbm.at[pl.ds(d, 1)], sem)

    def issue(g, carry):
        for u in range(ROW_UNROLL):
            r = g * ROW_UNROLL + u
            t = i * rows_per_step + r
            for k in range(2):
                row_copy(r, dest_ref[2 * t + k]).start()
        return carry

    def wait(g, carry):
        for _ in range(2 * ROW_UNROLL):
            row_copy(0, 0).wait()
        return carry

    lax.fori_loop(0, rows_per_step // ROW_UNROLL, issue, 0)
    lax.fori_loop(0, rows_per_step // ROW_UNROLL, wait, 0)


def _dispatch(x1, dest, padded_rows):
    T, D = x1.shape
    rows_per_step = min(512, T)
    xb0 = jnp.zeros((padded_rows, D), F32)
    return pl.pallas_call(
        functools.partial(_dispatch_body, rows_per_step=rows_per_step),
        grid_spec=pltpu.PrefetchScalarGridSpec(
            num_scalar_prefetch=1,
            grid=(T // rows_per_step,),
            in_specs=[pl.BlockSpec((rows_per_step, D), lambda i, d: (i, 0)),
                      pl.BlockSpec(memory_space=pl.ANY)],
            out_specs=pl.BlockSpec(memory_space=pl.ANY),
            scratch_shapes=[pltpu.SemaphoreType.DMA(())]),
        out_shape=jax.ShapeDtypeStruct((padded_rows, D), F32),
        input_output_aliases={2: 0},
        compiler_params=_params(("arbitrary",)),
        name="moe_dispatch",
    )(dest, x1, xb0)


def _expert_body(be_ref, nu_ref, first_ref, slot_ref, next_ref, xb_ref, sc_ref, sh_ref,
                 wg_hbm, wu_hbm, wd_hbm, yb_ref, wg_f, wu_f, wd_f, wg_s, wu_s, wd_s, sem, *, layer):
    b = pl.program_id(0)
    nu = nu_ref[0]
    base = layer * N_EXPERTS

    def weight_copies(e, slot):
        return (pltpu.make_async_copy(wg_hbm.at[base + e], wg_f.at[slot], sem.at[slot, 0]),
                pltpu.make_async_copy(wu_hbm.at[base + e], wu_f.at[slot], sem.at[slot, 1]),
                pltpu.make_async_copy(wd_hbm.at[base + e], wd_f.at[slot], sem.at[slot, 2]))

    @pl.when(b == 0)
    def _():
        for copy in weight_copies(be_ref[0], 0):
            copy.start()

    @pl.when((b < nu) & (first_ref[b] == 1))
    def _():
        e = be_ref[b]
        nxt = next_ref[b]
        for slot in range(2):
            @pl.when(slot_ref[b] == slot)
            def _():
                for copy in weight_copies(e, slot):
                    copy.wait()

                @pl.when(nxt >= 0)
                def _():
                    for copy in weight_copies(nxt, 1 - slot):
                        copy.start()

                wg_s[...] = wg_f[slot].astype(BF16)
                wu_s[...] = wu_f[slot].astype(BF16)
                wd_s[...] = wd_f[slot].astype(BF16)

    @pl.when(b < nu)
    def _():
        h = (xb_ref[...] * (1.0 + sc_ref[...]) + sh_ref[...]).astype(BF16)
        a = jnp.dot(h, wg_s[...], preferred_element_type=F32)
        u = jnp.dot(h, wu_s[...], preferred_element_type=F32)
        act = (a / (1.0 + jnp.exp(-a))) * u
        yb_ref[...] = jnp.dot(act.astype(BF16), wd_s[...], preferred_element_type=F32)

    @pl.when(b >= nu)
    def _():
        yb_ref[...] = jnp.zeros_like(yb_ref)


def _experts(xb, sc2, sh2, w_g, w_u, w_d, plan, layer):
    P, D = xb.shape
    DB = DISPATCH_BLOCK
    nblk = P // DB
    blk = lambda b, be, nu, *_: (jnp.minimum(b, nu[0] - 1), 0)
    vec = pl.BlockSpec((1, D), lambda b, *_: (0, 0))
    hbm = pl.BlockSpec(memory_space=pl.ANY)
    return pl.pallas_call(
        functools.partial(_expert_body, layer=layer),
        grid_spec=pltpu.PrefetchScalarGridSpec(
            num_scalar_prefetch=5,
            grid=(nblk,),
            in_specs=[pl.BlockSpec((DB, D), blk), vec, vec, hbm, hbm, hbm],
            out_specs=pl.BlockSpec((DB, D), lambda b, *_: (b, 0)),
            scratch_shapes=[pltpu.VMEM((2, D, D_EXPERT), F32), pltpu.VMEM((2, D, D_EXPERT), F32),
                            pltpu.VMEM((2, D_EXPERT, D), F32),
                            pltpu.VMEM((D, D_EXPERT), BF16), pltpu.VMEM((D, D_EXPERT), BF16),
                            pltpu.VMEM((D_EXPERT, D), BF16),
                            pltpu.SemaphoreType.DMA((2, 3))]),
        out_shape=jax.ShapeDtypeStruct((P, D), F32),
        compiler_params=_params(("arbitrary",)),
        name="moe_experts",
    )(*plan, xb, sc2, sh2, w_g, w_u, w_d)


def _combine_body(dest_ref, x1_ref, wt_ref, g2_ref, lng_ref, lnb_ref, yb_hbm, o_ref,
                  buf0, buf1, sem, *, tm, alpha):
    i = pl.program_id(0)
    bufs = (buf0, buf1)

    def row_copy(d, k, r):
        return pltpu.make_async_copy(yb_hbm.at[pl.ds(d, 1)], bufs[k].at[pl.ds(r, 1)], sem)

    def issue(g, carry):
        for u in range(ROW_UNROLL):
            r = g * ROW_UNROLL + u
            t = i * tm + r
            for k in range(2):
                row_copy(dest_ref[2 * t + k], k, r).start()
        return carry

    def wait(g, carry):
        for _ in range(ROW_UNROLL):
            for k in range(2):
                row_copy(0, k, 0).wait()
        return carry

    lax.fori_loop(0, tm // ROW_UNROLL, issue, 0)
    lax.fori_loop(0, tm // ROW_UNROLL, wait, 0)
    wt = wt_ref[...]
    y = buf0[...] * wt[:, 0:1] + buf1[...] * wt[:, 1:2]
    o_ref[...] = _layer_norm(alpha * x1_ref[...] + g2_ref[...] * y, lng_ref[...], lnb_ref[...])


def _combine_ln(x1, yb, dest, w_top, g2, ln_g, ln_b, alpha):
    T, D = x1.shape
    tm = min(256, T)
    row = pl.BlockSpec((tm, D), lambda i, d: (i, 0))
    vec = pl.BlockSpec((1, D), lambda i, d: (0, 0))
    return pl.pallas_call(
        functools.partial(_combine_body, tm=tm, alpha=alpha),
        grid_spec=pltpu.PrefetchScalarGridSpec(
            num_scalar_prefetch=1,
            grid=(T // tm,),
            in_specs=[row, pl.BlockSpec((tm, 2), lambda i, d: (i, 0)), vec, vec, vec,
                      pl.BlockSpec(memory_space=pl.ANY)],
            out_specs=row,
            scratch_shapes=[pltpu.VMEM((tm, D), F32), pltpu.VMEM((tm, D), F32),
                            pltpu.SemaphoreType.DMA(())]),
        out_shape=jax.ShapeDtypeStruct((T, D), F32),
        compiler_params=_params(("arbitrary",)),
        name="moe_combine_ln",
    )(dest, x1, w_top, g2, ln_g, ln_b, yb)


def _route(route, counts):
    T = route.shape[0]
    DB = DISPATCH_BLOCK
    expert_id = route[:, ROUTE_EXPERT:ROUTE_EXPERT + 2].astype(jnp.int32)
    w_top = route[:, ROUTE_WEIGHT:ROUTE_WEIGHT + 2]
    rank = route[:, ROUTE_RANK:ROUTE_RANK + 2].astype(jnp.int32)
    counts = counts[0, N_GROUPS:N_GROUPS + N_EXPERTS].astype(jnp.int32)
    padded = (counts + DB - 1) // DB * DB
    pend = jnp.cumsum(padded)
    pstart = pend - padded
    dest = (pstart[expert_id] + rank).astype(jnp.int32)
    P = ((2 * T + DB - 1) // DB) * DB + N_EXPERTS * DB
    nblk = P // DB
    block_e = jnp.minimum(jnp.searchsorted(pend, jnp.arange(nblk, dtype=jnp.int32) * DB, side='right'),
                          N_EXPERTS - 1).astype(jnp.int32)
    n_used = (pend[-1:] // DB).astype(jnp.int32)
    b_idx = jnp.arange(nblk, dtype=jnp.int32)
    first = (b_idx < n_used) & ((b_idx == 0) | (block_e != jnp.roll(block_e, 1)))
    slot = (jnp.cumsum(first.astype(jnp.int32)) - 1) % 2
    after = (pend // DB).astype(jnp.int32)[block_e]
    next_e = jnp.where(after < n_used, block_e[jnp.minimum(after, nblk - 1)], -1)
    plan = (block_e, n_used, first.astype(jnp.int32), slot.astype(jnp.int32), next_e.astype(jnp.int32))
    return dest.reshape(-1), w_top, plan, P


def kernel(x, c, w_ada, b_ada, w_in, w_branch_gate, b_branch_gate, attn_sinks, lambda_q1, lambda_k1,
           lambda_q2, lambda_k2, subln_g, w_branch, w_out, ln1_g, ln1_b, w_router_group, b_router_group,
           w_router_expert, b_router_expert, w_exp_gate, w_exp_up, w_exp_down, ln2_g, ln2_b):
    B, S, D = x.shape
    assert B == 1 and D == D_MODEL
    depth = w_in.shape[0]
    alpha = ALPHA
    xs = x.reshape(S, D)
    mod = _ada_mod(c, w_ada, b_ada)
    diff_slopes = jnp.exp2(-8.0 * jnp.arange(1, DIFF_HEADS + 1, dtype=F32) / DIFF_HEADS)
    diff_slopes = jnp.stack([diff_slopes, 1.0 / diff_slopes])
    zero_bias = jnp.zeros((1, w_in.shape[2]), F32)
    pad = LANES - N_GROUPS - N_EXPERTS
    w_eg = w_exp_gate.reshape(depth * N_EXPERTS, D, D_EXPERT)
    w_eu = w_exp_up.reshape(depth * N_EXPERTS, D, D_EXPERT)
    w_ed = w_exp_down.reshape(depth * N_EXPERTS, D_EXPERT, D)
    for l in range(depth):
        lambda_init = 0.8 - 0.6 * math.exp(-0.3 * l)
        sh1, sc1, g1, sh2, sc2, g2 = [mod[l, :, n * D:(n + 1) * D] for n in range(6)]
        proj = _mod_matmul(xs, sc1, sh1, w_in, l, zero_bias, sigmoid=False, name="in_proj")
        gates = _mod_matmul(xs, sc1, sh1, w_branch_gate, l, b_branch_gate[l].reshape(1, -1),
                            sigmoid=True, name="branch_gates")
        o_a = _swa_attention(proj, attn_sinks[l])
        o_b = _sb_attention(proj)
        o_c = _diff_attention(proj, diff_slopes, lambda_q1[l], lambda_k1[l], lambda_q2[l], lambda_k2[l],
                              subln_g[l], lambda_init)
        z = _branch_merge(o_a, o_b, o_c, w_branch, l, gates)
        w_router = jnp.pad(jnp.concatenate([w_router_group[l], w_router_expert[l]], axis=1), ((0, 0), (0, pad)))
        b_router = jnp.pad(jnp.concatenate([b_router_group[l], b_router_expert[l]]), (0, pad)).reshape(1, LANES)
        x1, route, counts = _mixer_out(z, w_out[l].astype(BF16), xs, g1, ln1_g[l].reshape(1, D),
                                       ln1_b[l].reshape(1, D), sc2, sh2, w_router, b_router, alpha)
        dest, w_top, plan, P = _route(route, counts)
        xb = _dispatch(x1, dest, P)
        yb = _experts(xb, sc2, sh2, w_eg, w_eu, w_ed, plan, l)
        xs = _combine_ln(x1, yb, dest, w_top, g2, ln2_g[l].reshape(1, D), ln2_b[l].reshape(1, D), alpha)
    return xs.reshape(B, S, D)
```

```python
import functools
import math

import jax
import jax.numpy as jnp
from jax import lax
from jax.experimental import pallas as pl
from jax.experimental.pallas import tpu as pltpu

F32 = jnp.float32
BF16 = jnp.bfloat16

D_MODEL = 2048
HEAD_DIM = 64
SWA_Q_HEADS = 16
SWA_KV_HEADS = 4
WINDOW = 128
SB_HEADS = 16
DIFF_HEADS = 8
BLOCK_Q = 128
BRANCH_WIDTH = 1024
A_KV = 256
N_BRANCHES = 3
N_GROUPS = 8
EXPERTS_PER_GROUP = 8
N_EXPERTS = 64
D_EXPERT = 384
DISPATCH_BLOCK = 128
DEPTH = 4
ALPHA = (2.0 * DEPTH) ** 0.25
LN_EPS = 1e-5
RMS_EPS = 1e-5
QK_SCALE = 1.0 / math.sqrt(HEAD_DIM)

COL_QA, COL_KA, COL_VA = 0, 1024, 1280
COL_QB, COL_KB, COL_VB = 1536, 2560, 3584
COL_QC, COL_KC, COL_VC = 4608, 5632, 6656

LANES = 128
VMEM_LIMIT = 56 * 1024 * 1024
NEG = -1e30
SB_EXIT = 88.0
ALIBI_CUT = 100.0
POS_SPLIT = 64
SUM_ROWS = 16
ROW_UNROLL = 8


def _params(sem):
    return pltpu.CompilerParams(dimension_semantics=sem, vmem_limit_bytes=VMEM_LIMIT)


def _ada_body(c_ref, w_ref, b_ref, o_ref):
    o_ref[0] = jnp.sum(c_ref[...] * w_ref[0], axis=0, keepdims=True) + b_ref[0]


def _ada_mod(c, w_ada, b_ada):
    L, D, N = w_ada.shape
    tn = 512
    return pl.pallas_call(
        _ada_body,
        grid=(L, N // tn),
        in_specs=[pl.BlockSpec((D, 1), lambda l, j: (0, 0)),
                  pl.BlockSpec((1, D, tn), lambda l, j: (l, 0, j)),
                  pl.BlockSpec((1, 1, tn), lambda l, j: (l, 0, j))],
        out_specs=pl.BlockSpec((1, 1, tn), lambda l, j: (l, 0, j)),
        out_shape=jax.ShapeDtypeStruct((L, 1, N), F32),
        compiler_params=_params(("arbitrary", "arbitrary")),
        name="ada_mod",
    )(c.reshape(D, 1), w_ada, b_ada.reshape(L, 1, N))


def _modmm_body(x_ref, sc_ref, sh_ref, w_ref, b_ref, o_ref, h_ref, *, sigmoid):
    @pl.when(pl.program_id(1) == 0)
    def _():
        h_ref[...] = (x_ref[...] * (1.0 + sc_ref[...]) + sh_ref[...]).astype(BF16)

    acc = jnp.dot(h_ref[...], w_ref[0].astype(BF16), preferred_element_type=F32) + b_ref[...]
    if sigmoid:
        acc = 1.0 / (1.0 + jnp.exp(-acc))
    o_ref[...] = acc.astype(o_ref.dtype)


def _mod_matmul(x, sc, sh, w, layer, b, *, sigmoid, name):
    M, K = x.shape
    N = w.shape[2]
    tm = min(1024, M)
    tn = 768
    return pl.pallas_call(
        functools.partial(_modmm_body, sigmoid=sigmoid),
        grid=(M // tm, N // tn),
        in_specs=[pl.BlockSpec((tm, K), lambda i, j: (i, 0)),
                  pl.BlockSpec((1, K), lambda i, j: (0, 0)),
                  pl.BlockSpec((1, K), lambda i, j: (0, 0)),
                  pl.BlockSpec((1, K, tn), lambda i, j: (layer, 0, j)),
                  pl.BlockSpec((1, tn), lambda i, j: (0, j))],
        out_specs=pl.BlockSpec((tm, tn), lambda i, j: (i, j)),
        out_shape=jax.ShapeDtypeStruct((M, N), BF16),
        scratch_shapes=[pltpu.VMEM((tm, K), BF16)],
        compiler_params=_params(("arbitrary", "arbitrary")),
        name=name,
    )(x, sc, sh, w, b)


def _swa_body(sinks_ref, q_ref, kp_ref, kc_ref, vp_ref, vc_ref, o_ref):
    i = pl.program_id(0)
    k = jnp.concatenate([kp_ref[...], kc_ref[...]], axis=0)
    v = jnp.concatenate([vp_ref[...], vc_ref[...]], axis=0)
    qi = lax.broadcasted_iota(jnp.int32, (BLOCK_Q, 2 * BLOCK_Q), 0)
    ki = lax.broadcasted_iota(jnp.int32, (BLOCK_Q, 2 * BLOCK_Q), 1)
    dist = qi + BLOCK_Q - ki
    valid = (dist >= 0) & (dist < WINDOW) & ((ki >= BLOCK_Q) | (i > 0))
    distf = dist.astype(F32)
    group = SWA_Q_HEADS // SWA_KV_HEADS
    outs = []
    for h in range(SWA_Q_HEADS):
        g = h // group
        slope = 2.0 ** (-8.0 * (h + 1) / SWA_Q_HEADS)
        qh = q_ref[:, h * HEAD_DIM:(h + 1) * HEAD_DIM]
        kg = k[:, g * HEAD_DIM:(g + 1) * HEAD_DIM]
        vg = v[:, g * HEAD_DIM:(g + 1) * HEAD_DIM]
        s = lax.dot_general(qh, kg, (((1,), (1,)), ((), ())), preferred_element_type=F32) * QK_SCALE
        s = jnp.where(valid, s - slope * distf, NEG)
        sink = sinks_ref[h]
        m = jnp.maximum(jnp.max(s, axis=1, keepdims=True), sink)
        p = jnp.where(valid, jnp.exp(s - m), 0.0)
        denom = jnp.sum(p, axis=1, keepdims=True) + jnp.exp(sink - m)
        o = jnp.dot(p.astype(BF16), vg, preferred_element_type=F32) / denom
        outs.append(o)
    o_ref[...] = jnp.concatenate(outs, axis=1).astype(o_ref.dtype)


def _swa_attention(proj, sinks):
    S = proj.shape[0]
    nb = S // BLOCK_Q
    kv_blk = lambda col: pl.BlockSpec((BLOCK_Q, A_KV), lambda i: (i, col // A_KV))
    kv_prev = lambda col: pl.BlockSpec((BLOCK_Q, A_KV), lambda i: (jnp.maximum(i - 1, 0), col // A_KV))
    return pl.pallas_call(
        _swa_body,
        grid=(nb,),
        in_specs=[pl.BlockSpec(memory_space=pltpu.SMEM),
                  pl.BlockSpec((BLOCK_Q, BRANCH_WIDTH), lambda i: (i, COL_QA // BRANCH_WIDTH)),
                  kv_prev(COL_KA), kv_blk(COL_KA), kv_prev(COL_VA), kv_blk(COL_VA)],
        out_specs=pl.BlockSpec((BLOCK_Q, BRANCH_WIDTH), lambda i: (i, 0)),
        out_shape=jax.ShapeDtypeStruct((S, BRANCH_WIDTH), BF16),
        compiler_params=_params(("arbitrary",)),
        name="swa_attention",
    )(sinks, proj, proj, proj, proj, proj)


def _sb_body(q_ref, k_ref, v_ref, o_ref, vt_ref, later_ref, carry_ref, acc_ref, *, tq):
    g = pl.program_id(1)
    nkb = vt_ref.shape[0]
    lane = lax.broadcasted_iota(jnp.int32, (1, LANES), 1)

    @pl.when(g == 0)
    def _():
        def setup(c, carry):
            off = pl.multiple_of(c * tq, tq)
            vt_ref[c] = v_ref[pl.ds(off, tq), :].astype(F32).T.astype(BF16)
            return carry

        lax.fori_loop(0, nkb, setup, 0)
        this_key = lax.broadcasted_iota(jnp.int32, later_ref.shape, 0)
        other_key = lax.broadcasted_iota(jnp.int32, later_ref.shape, 1)
        later_ref[...] = (other_key > this_key).astype(BF16)

    qs = q_ref[...] * QK_SCALE
    acc_ref[...] = jnp.zeros_like(acc_ref)

    def stream(tile, half, first, carry):
        in_half = (lane >= HEAD_DIM) if half else (lane < HEAD_DIM)
        q_tile = qs[tile * tq:(tile + 1) * tq, :]
        return dict(qm=jnp.where(in_half, q_tile, jnp.zeros_like(q_tile)), tile=tile, half=half,
                    first=first, carry=carry)

    def process(streams, nb, masked):
        n = nb * tq
        later = later_ref[...]
        if masked:
            k_row = lax.broadcasted_iota(jnp.int32, (n, tq), 0)
            q_col = lax.broadcasted_iota(jnp.int32, (n, tq), 1)
            before = k_row < q_col + (nb - 1) * tq
        zs = []
        for st in streams:
            off = pl.multiple_of(st["first"] * tq, tq)
            zs.append(lax.dot_general(k_ref[pl.ds(off, n), :], st["qm"], (((1,), (1,)), ((), ())),
                                      preferred_element_type=F32))
        log_rems, splits = [], []
        for z in zs:
            soft = jnp.log(1.0 + jnp.exp(-jnp.abs(z)))
            log_rem = -(jnp.maximum(z, 0.0) + soft)
            if masked:
                log_rem = jnp.where(before, log_rem, 0.0)
            hi = log_rem.astype(BF16)
            lo = (log_rem - hi.astype(F32)).astype(BF16)
            log_rems.append(log_rem)
            splits.append((hi, lo))
        afters, totals = [], []
        for (hi, lo), log_rem in zip(splits, log_rems):
            parts, total = [None] * nb, jnp.zeros((1, tq), F32)
            for b in reversed(range(nb)):
                blk = slice(b * tq, (b + 1) * tq)
                inside = (jnp.dot(later, hi[blk, :], preferred_element_type=F32)
                          + jnp.dot(later, lo[blk, :], preferred_element_type=F32))
                parts[b] = inside + total
                total = total + inside[0:1, :] + log_rem[b * tq:b * tq + 1, :]
            afters.append(parts[0] if nb == 1 else jnp.concatenate(parts, axis=0))
            totals.append(total)
        new_carries = []
        for st, z, log_rem, after, total in zip(streams, zs, log_rems, afters, totals):
            w = jnp.exp((log_rem + z) + (after + st["carry"]))
            if masked:
                w = jnp.where(before, w, 0.0)
            w = w.astype(BF16)
            rows = slice(st["half"] * HEAD_DIM, (st["half"] + 1) * HEAD_DIM)
            for b in range(nb):
                acc_ref[st["tile"], st["half"]] += jnp.dot(vt_ref[st["first"] + b][rows, :],
                                                           w[b * tq:(b + 1) * tq, :], preferred_element_type=F32)
            new_carries.append(st["carry"] + total)
        return new_carries

    zero = jnp.zeros((1, tq), F32)
    tiles_heads = [(tile, half) for tile in range(2) for half in range(2)]

    def keep(carries, pairs):
        for (tile, half), carry in zip(pairs, carries):
            carry_ref[tile, half] = carry

    @pl.when(g == 0)
    def _():
        keep(process([stream(0, half, 0, zero) for half in range(2)], 1, True), tiles_heads[:2])
        keep(process([stream(1, half, 0, zero) for half in range(2)], 2, True), tiles_heads[2:])

    @pl.when(g > 0)
    def _():
        keep(process([stream(tile, half, 2 * g + tile - 1, zero) for tile, half in tiles_heads], 2, True),
             tiles_heads)

    def top(carries):
        return jnp.maximum(jnp.max(carries[0]), jnp.max(carries[1]))

    for tile in range(2):
        carries = (carry_ref[tile, 0], carry_ref[tile, 1])

        def cond(state):
            j, _, best = state
            return (j >= 0) & (best > -SB_EXIT)

        def body(state, tile=tile):
            j, carries, _ = state
            carries = tuple(process([stream(tile, half, j, carries[half]) for half in range(2)], 1, False))
            return j - 1, carries, top(carries)

        lax.while_loop(cond, body, (2 * g + tile - 2, carries, top(carries)))
        o_ref[tile * tq:(tile + 1) * tq, :] = jnp.concatenate(
            [acc_ref[tile, 0], acc_ref[tile, 1]], axis=0).T.astype(o_ref.dtype)


def _sb_attention(proj):
    S = proj.shape[0]
    tq = min(256, S // 2)
    npair = BRANCH_WIDTH // LANES
    return pl.pallas_call(
        functools.partial(_sb_body, tq=tq),
        grid=(npair, S // (2 * tq)),
        in_specs=[pl.BlockSpec((2 * tq, LANES), lambda p, g: (g, COL_QB // LANES + p)),
                  pl.BlockSpec((S, LANES), lambda p, g: (0, COL_KB // LANES + p)),
                  pl.BlockSpec((S, LANES), lambda p, g: (0, COL_VB // LANES + p))],
        out_specs=pl.BlockSpec((2 * tq, LANES), lambda p, g: (g, p)),
        out_shape=jax.ShapeDtypeStruct((S, BRANCH_WIDTH), BF16),
        scratch_shapes=[pltpu.VMEM((S // tq, LANES, tq), BF16),
                        pltpu.VMEM((tq, tq), BF16),
                        pltpu.VMEM((2, 2, 1, tq), F32),
                        pltpu.VMEM((2, 2, HEAD_DIM, tq), F32)],
        compiler_params=_params(("arbitrary", "arbitrary")),
        name="stickbreak_attention",
    )(proj, proj, proj)


def _diff_body(slopes_ref, q_ref, k_ref, v_ref, lq1_ref, lk1_ref, lq2_ref, lk2_ref, g_ref, o_ref,
               kaug_ref, vt_ref, knorm_ref, qa_ref, sta_ref, stb_ref, bmaxa_ref, bmaxb_ref, m_ref, acc_ref, *, tq,
               lambda_init):
    h = pl.program_id(0)
    i = pl.program_id(1)
    slope = slopes_ref[0, h]
    inv_slope = slopes_ref[1, h]
    nkb = kaug_ref.shape[0] // tq
    lane = lax.broadcasted_iota(jnp.int32, (1, LANES), 1)

    def max_sq_norms(x):
        sq = x.astype(F32)
        sq = sq * sq
        first = jnp.sum(jnp.where(lane < HEAD_DIM, sq, 0.0), axis=1, keepdims=True)
        second = jnp.sum(jnp.where(lane >= HEAD_DIM, sq, 0.0), axis=1, keepdims=True)
        return jnp.max(first, axis=0, keepdims=True), jnp.max(second, axis=0, keepdims=True)

    @pl.when(i == 0)
    def _():
        def setup(c, carry):
            off = pl.multiple_of(c * tq, tq)
            n1, n2 = max_sq_norms(k_ref[pl.ds(off, tq), :])
            carry = (jnp.maximum(carry[0], n1), jnp.maximum(carry[1], n2))
            pos = off + lax.broadcasted_iota(jnp.int32, (tq, LANES), 0)
            lanes = lax.broadcasted_iota(jnp.int32, (tq, LANES), 1)
            coarse = (pos // POS_SPLIT) * POS_SPLIT
            posm = jnp.where(lanes == 0, coarse, jnp.where(lanes == 1, pos - coarse, 0))
            kaug_ref[pl.ds(off, tq), 0:LANES] = k_ref[pl.ds(off, tq), :]
            kaug_ref[pl.ds(off, tq), LANES:2 * LANES] = posm.astype(F32).astype(BF16)
            vt_ref[c, 0:LANES, :] = v_ref[pl.ds(off, tq), :].astype(F32).T.astype(BF16)
            ones_row = lax.broadcasted_iota(jnp.int32, (SUM_ROWS, tq), 0) == 0
            vt_ref[c, LANES:LANES + SUM_ROWS, :] = jnp.where(ones_row, 1.0, 0.0).astype(BF16)
            return carry

        zero = jnp.zeros((1, 1), F32)
        knorm_ref[0], knorm_ref[1] = lax.fori_loop(0, nkb, setup, (zero, zero))

    q1n, q2n = max_sq_norms(q_ref[...])
    qk = jnp.sqrt(jnp.maximum(q1n * knorm_ref[0], q2n * knorm_ref[1]))
    reach = (ALIBI_CUT + 2.0 * QK_SCALE * qk) * inv_slope
    keep = jnp.minimum(jnp.floor((reach - 1.0) * (1.0 / tq)) + 1.0, 1e6).astype(jnp.int32)
    n_off = jnp.clip(jnp.max(keep), 0, i)
    j0 = i - n_off

    qs = q_ref[...] * QK_SCALE
    bias_cols = jnp.broadcast_to(jnp.where(lane < 2, slope, 0.0).astype(BF16), (tq, LANES))
    for half in range(2):
        in_half = (lane >= HEAD_DIM) if half else (lane < HEAD_DIM)
        qa_ref[half] = jnp.concatenate([jnp.where(in_half, qs, jnp.zeros_like(qs)), bias_cols], axis=1)
    acc_ref[...] = jnp.zeros_like(acc_ref)
    m_ref[...] = jnp.full_like(m_ref, NEG)
    on_or_below_diagonal = (lax.broadcasted_iota(jnp.int32, (tq, tq), 0)
                            <= lax.broadcasted_iota(jnp.int32, (tq, tq), 1))

    def scores(jb, buf):
        st_ref, bmax_ref = buf
        off = pl.multiple_of(jb * tq, tq)
        kb = kaug_ref[pl.ds(off, tq), :]
        for half in range(2):
            st = lax.dot_general(kb, qa_ref[half], (((1,), (1,)), ((), ())),
                                 preferred_element_type=F32)
            st_ref[half] = st
            bmax_ref[half] = jnp.max(st, axis=0, keepdims=True)

    def softmax_pv(jb, buf, diagonal):
        st_ref, bmax_ref = buf
        vtb = vt_ref[jb]
        alphas, ps = [], []
        for half in range(2):
            st = st_ref[half]
            if diagonal:
                st = jnp.where(on_or_below_diagonal, st, NEG)
                block_max = jnp.max(st, axis=0, keepdims=True)
            else:
                block_max = bmax_ref[half]
            m_old = m_ref[half]
            m_new = jnp.maximum(m_old, block_max)
            alpha = jnp.exp(m_old - m_new)
            p = jnp.exp(st - m_new)
            m_ref[half] = m_new
            alphas.append(alpha)
            ps.append(p.astype(BF16))
        for half in range(2):
            acc_ref[half] = alphas[half] * acc_ref[half] + jnp.dot(vtb, ps[half], preferred_element_type=F32)

    buf_a, buf_b = (sta_ref, bmaxa_ref), (stb_ref, bmaxb_ref)
    scores(j0, buf_a)

    def pair(p, carry):
        scores(j0 + 2 * p + 1, buf_b)
        softmax_pv(j0 + 2 * p, buf_a, False)
        scores(j0 + 2 * p + 2, buf_a)
        softmax_pv(j0 + 2 * p + 1, buf_b, False)
        return carry

    lax.fori_loop(0, n_off // 2, pair, 0)

    @pl.when(n_off % 2 == 0)
    def _():
        softmax_pv(i, buf_a, True)

    @pl.when(n_off % 2 == 1)
    def _():
        scores(i, buf_b)
        softmax_pv(i - 1, buf_a, False)
        softmax_pv(i, buf_b, True)

    lam = (jnp.exp(jnp.sum(lq1_ref[...] * lk1_ref[...], axis=1, keepdims=True))
           - jnp.exp(jnp.sum(lq2_ref[...] * lk2_ref[...], axis=1, keepdims=True)) + lambda_init)
    o = (acc_ref[0, 0:LANES, :] * (1.0 / acc_ref[0, LANES:LANES + 1, :])
         - acc_ref[1, 0:LANES, :] * (lam / acc_ref[1, LANES:LANES + 1, :]))
    y = o * lax.rsqrt(jnp.mean(o * o, axis=0, keepdims=True) + RMS_EPS) * g_ref[...]
    o_ref[...] = (y * (1.0 - lambda_init)).T.astype(o_ref.dtype)


def _diff_attention(proj, slopes, lq1, lk1, lq2, lk2, subln_g, lambda_init):
    S = proj.shape[0]
    tq = min(512, S)
    vec = lambda n: pl.BlockSpec((1, n), lambda h, i: (0, 0))
    return pl.pallas_call(
        functools.partial(_diff_body, tq=tq, lambda_init=lambda_init),
        grid=(DIFF_HEADS, S // tq),
        in_specs=[pl.BlockSpec(memory_space=pltpu.SMEM),
                  pl.BlockSpec((tq, LANES), lambda h, i: (i, COL_QC // LANES + h)),
                  pl.BlockSpec((S, LANES), lambda h, i: (0, COL_KC // LANES + h)),
                  pl.BlockSpec((S, LANES), lambda h, i: (0, COL_VC // LANES + h)),
                  vec(HEAD_DIM), vec(HEAD_DIM), vec(HEAD_DIM), vec(HEAD_DIM),
                  pl.BlockSpec((2 * HEAD_DIM, 1), lambda h, i: (0, 0))],
        out_specs=pl.BlockSpec((tq, LANES), lambda h, i: (i, h)),
        out_shape=jax.ShapeDtypeStruct((S, BRANCH_WIDTH), BF16),
        scratch_shapes=[pltpu.VMEM((S, 2 * LANES), BF16),
                        pltpu.VMEM((S // tq, LANES + SUM_ROWS, tq), BF16),
                        pltpu.VMEM((2, 1, 1), F32),
                        pltpu.VMEM((2, tq, 2 * LANES), BF16),
                        pltpu.VMEM((2, tq, tq), F32), pltpu.VMEM((2, tq, tq), F32),
                        pltpu.VMEM((2, 1, tq), F32), pltpu.VMEM((2, 1, tq), F32),
                        pltpu.VMEM((2, 1, tq), F32),
                        pltpu.VMEM((2, LANES + SUM_ROWS, tq), F32)],
        compiler_params=_params(("arbitrary", "arbitrary")),
        name="diff_attention",
    )(slopes, proj, proj, proj, lq1.reshape(1, -1), lk1.reshape(1, -1), lq2.reshape(1, -1),
      lk2.reshape(1, -1), subln_g.reshape(-1, 1))


def _branch_body(oa_ref, ob_ref, oc_ref, w_ref, ga_ref, gb_ref, gc_ref, z_ref):
    z = None
    for n, (o_ref, g_ref) in enumerate(((oa_ref, ga_ref), (ob_ref, gb_ref), (oc_ref, gc_ref))):
        y = jnp.dot(o_ref[...], w_ref[0, n].astype(BF16), preferred_element_type=F32)
        t = g_ref[...].astype(F32) * y
        z = t if z is None else z + t
    z_ref[...] = z.astype(z_ref.dtype)


def _branch_merge(o_a, o_b, o_c, w_branch, layer, gates):
    S = o_a.shape[0]
    D = w_branch.shape[3]
    tm = min(1024, S)
    tn = 512
    nj = D // tn
    o_spec = pl.BlockSpec((tm, BRANCH_WIDTH), lambda i, j: (i, 0))
    gate_spec = lambda n: pl.BlockSpec((tm, tn), lambda i, j: (i, n * nj + j))
    return pl.pallas_call(
        _branch_body,
        grid=(S // tm, nj),
        in_specs=[o_spec, o_spec, o_spec,
                  pl.BlockSpec((1, N_BRANCHES, BRANCH_WIDTH, tn), lambda i, j: (layer, 0, 0, j)),
                  gate_spec(0), gate_spec(1), gate_spec(2)],
        out_specs=pl.BlockSpec((tm, tn), lambda i, j: (i, j)),
        out_shape=jax.ShapeDtypeStruct((S, D), BF16),
        compiler_params=_params(("arbitrary", "arbitrary")),
        name="branch_merge",
    )(o_a, o_b, o_c, w_branch, gates, gates, gates)


def _layer_norm(r, g, b):
    mu = jnp.mean(r, axis=1, keepdims=True)
    d = r - mu
    var = jnp.mean(d * d, axis=1, keepdims=True)
    return d * lax.rsqrt(var + LN_EPS) * g + b


def _split_bf16(a):
    hi = a.astype(BF16)
    return hi, (a - hi.astype(F32)).astype(BF16)


ROUTE_EXPERT, ROUTE_WEIGHT, ROUTE_RANK = 0, 2, 4


def _route_tile(logits, counts):
    rows = logits.shape[0]
    lane = lax.broadcasted_iota(jnp.int32, logits.shape, 1)
    row_min = lambda cond: jnp.min(jnp.where(cond, lane, LANES), axis=1, keepdims=True)
    row_max = lambda cond: jnp.max(jnp.where(cond, logits, NEG), axis=1, keepdims=True)
    is_group = lane < N_GROUPS
    g_max = row_max(is_group)
    g_top = row_min(is_group & (logits == g_max))
    p_group = 1.0 / jnp.sum(jnp.where(is_group, jnp.exp(logits - g_max), 0.0), axis=1, keepdims=True)
    e_lane = lane - N_GROUPS
    in_group = ((e_lane >= 0) & (e_lane < N_EXPERTS)
                & (jnp.right_shift(e_lane, EXPERTS_PER_GROUP.bit_length() - 1) == g_top))
    v1 = row_max(in_group)
    i1 = row_min(in_group & (logits == v1))
    rest = in_group & (lane != i1)
    v2 = row_max(rest)
    i2 = row_min(rest & (logits == v2))
    ratio = jnp.exp(v2 - v1)
    w1 = p_group / (1.0 + ratio)
    w2 = w1 * ratio
    member = (lane == i1) | (lane == i2)
    earlier = (lax.broadcasted_iota(jnp.int32, (rows, rows), 1)
               < lax.broadcasted_iota(jnp.int32, (rows, rows), 0)).astype(BF16)
    before = jnp.dot(earlier, member.astype(BF16), preferred_element_type=F32) + counts
    pick = lambda idx: jnp.sum(jnp.where(lane == idx, before, 0.0), axis=1, keepdims=True)
    record = jnp.zeros(logits.shape, F32)
    for at, value in ((ROUTE_EXPERT, (i1 - N_GROUPS).astype(F32)), (ROUTE_EXPERT + 1, (i2 - N_GROUPS).astype(F32)),
                      (ROUTE_WEIGHT, w1), (ROUTE_WEIGHT + 1, w2),
                      (ROUTE_RANK, pick(i1)), (ROUTE_RANK + 1, pick(i2))):
        record = jnp.where(lane == at, value, record)
    return record, counts + jnp.sum(member.astype(F32), axis=0, keepdims=True)


def _out_body(z_ref, w_ref, x_ref, g1_ref, lng_ref, lnb_ref, sc2_ref, sh2_ref, wr_hi_ref, wr_lo_ref, br_ref,
              x1_ref, route_ref, counts_ref, running_ref, *, alpha):
    @pl.when(pl.program_id(0) == 0)
    def _():
        running_ref[...] = jnp.zeros_like(running_ref)

    y = jnp.dot(z_ref[...], w_ref[...], preferred_element_type=F32)
    x1 = _layer_norm(alpha * x_ref[...] + g1_ref[...] * y, lng_ref[...], lnb_ref[...])
    x1_ref[...] = x1
    h_hi, h_lo = _split_bf16(x1 * (1.0 + sc2_ref[...]) + sh2_ref[...])
    w_hi = wr_hi_ref[...]
    logits = (jnp.dot(h_hi, w_hi, preferred_element_type=F32)
              + (jnp.dot(h_hi, wr_lo_ref[...], preferred_element_type=F32)
                 + jnp.dot(h_lo, w_hi, preferred_element_type=F32))) + br_ref[...]
    route_ref[...], counts = _route_tile(logits, running_ref[...])
    running_ref[...] = counts
    counts_ref[...] = jnp.broadcast_to(counts, counts_ref.shape)


def _mixer_out(z, w_out, x, g1, ln_g, ln_b, sc2, sh2, w_router, b_router, alpha):
    S, D = x.shape
    tm = min(256, S)
    row = pl.BlockSpec((tm, D), lambda i: (i, 0))
    vec = pl.BlockSpec((1, D), lambda i: (0, 0))
    router = pl.BlockSpec((D, LANES), lambda i: (0, 0))
    wr_hi, wr_lo = _split_bf16(w_router)
    return pl.pallas_call(
        functools.partial(_out_body, alpha=alpha),
        grid=(S // tm,),
        in_specs=[row, pl.BlockSpec((D, D), lambda i: (0, 0)), row, vec, vec, vec, vec, vec,
                  router, router, pl.BlockSpec((1, LANES), lambda i: (0, 0))],
        out_specs=[row, pl.BlockSpec((tm, LANES), lambda i: (i, 0)), pl.BlockSpec((8, LANES), lambda i: (0, 0))],
        out_shape=[jax.ShapeDtypeStruct((S, D), F32), jax.ShapeDtypeStruct((S, LANES), F32),
                   jax.ShapeDtypeStruct((8, LANES), F32)],
        scratch_shapes=[pltpu.VMEM((1, LANES), F32)],
        compiler_params=_params(("arbitrary",)),
        name="mixer_out_ln",
    )(z, w_out, x, g1, ln_g, ln_b, sc2, sh2, wr_hi, wr_lo, b_router)


def _dispatch_body(dest_ref, x_ref, xb_in_hbm, xb_hbm, sem, *, rows_per_step):
    del xb_in_hbm
    i = pl.program_id(0)
    n_tokens = pl.num_programs(0) * rows_per_step

    def row_copy(r, d):
        return pltpu.make_async_copy(x_ref.at[pl.ds(r, 1)], xb_hbm.at[pl.ds(d, 1)], sem)

    def issue(g, carry):
        for u in range(ROW_UNROLL):
            r = g * ROW_UNROLL + u
            t = i * rows_per_step + r
            for k in range(2):
                row_copy(r, dest_ref[k * n_tokens + t]).start()
        return carry

    def wait(g, carry):
        for _ in range(2 * ROW_UNROLL):
            row_copy(0, 0).wait()
        return carry

    lax.fori_loop(0, rows_per_step // ROW_UNROLL, issue, 0)
    lax.fori_loop(0, rows_per_step // ROW_UNROLL, wait, 0)


def _dispatch(x1, dest, padded_rows):
    T, D = x1.shape
    rows_per_step = min(512, T)
    xb0 = jnp.zeros((padded_rows, D), F32)
    return pl.pallas_call(
        functools.partial(_dispatch_body, rows_per_step=rows_per_step),
        grid_spec=pltpu.PrefetchScalarGridSpec(
            num_scalar_prefetch=1,
            grid=(T // rows_per_step,),
            in_specs=[pl.BlockSpec((rows_per_step, D), lambda i, d: (i, 0)),
                      pl.BlockSpec(memory_space=pl.ANY)],
            out_specs=pl.BlockSpec(memory_space=pl.ANY),
            scratch_shapes=[pltpu.SemaphoreType.DMA(())]),
        out_shape=jax.ShapeDtypeStruct((padded_rows, D), F32),
        input_output_aliases={2: 0},
        compiler_params=_params(("arbitrary",)),
        name="moe_dispatch",
    )(dest, x1, xb0)


def _expert_body(be_ref, nu_ref, first_ref, slot_ref, next_ref, xb_ref, sc_ref, sh_ref,
                 wg_hbm, wu_hbm, wd_hbm, yb_ref, wg_f, wu_f, wd_f, wg_s, wu_s, wd_s, sem, *, layer):
    b = pl.program_id(0)
    nu = nu_ref[0]
    base = layer * N_EXPERTS

    def weight_copies(e, slot):
        return (pltpu.make_async_copy(wg_hbm.at[base + e], wg_f.at[slot], sem.at[slot, 0]),
                pltpu.make_async_copy(wu_hbm.at[base + e], wu_f.at[slot], sem.at[slot, 1]),
                pltpu.make_async_copy(wd_hbm.at[base + e], wd_f.at[slot], sem.at[slot, 2]))

    @pl.when(b == 0)
    def _():
        for copy in weight_copies(be_ref[0], 0):
            copy.start()

    @pl.when((b < nu) & (first_ref[b] == 1))
    def _():
        e = be_ref[b]
        nxt = next_ref[b]
        for slot in range(2):
            @pl.when(slot_ref[b] == slot)
            def _():
                for copy in weight_copies(e, slot):
                    copy.wait()

                @pl.when(nxt >= 0)
                def _():
                    for copy in weight_copies(nxt, 1 - slot):
                        copy.start()

                wg_s[...] = wg_f[slot].astype(BF16)
                wu_s[...] = wu_f[slot].astype(BF16)
                wd_s[...] = wd_f[slot].astype(BF16)

    @pl.when(b < nu)
    def _():
        h = (xb_ref[...] * (1.0 + sc_ref[...]) + sh_ref[...]).astype(BF16)
        a = jnp.dot(h, wg_s[...], preferred_element_type=F32)
        u = jnp.dot(h, wu_s[...], preferred_element_type=F32)
        act = (a / (1.0 + jnp.exp(-a))) * u
        yb_ref[...] = jnp.dot(act.astype(BF16), wd_s[...], preferred_element_type=F32)

    @pl.when(b >= nu)
    def _():
        yb_ref[...] = jnp.zeros_like(yb_ref)


def _experts(xb, sc2, sh2, w_g, w_u, w_d, plan, layer):
    P, D = xb.shape
    DB = DISPATCH_BLOCK
    nblk = P // DB
    blk = lambda b, be, nu, *_: (jnp.minimum(b, nu[0] - 1), 0)
    vec = pl.BlockSpec((1, D), lambda b, *_: (0, 0))
    hbm = pl.BlockSpec(memory_space=pl.ANY)
    return pl.pallas_call(
        functools.partial(_expert_body, layer=layer),
        grid_spec=pltpu.PrefetchScalarGridSpec(
            num_scalar_prefetch=5,
            grid=(nblk,),
            in_specs=[pl.BlockSpec((DB, D), blk), vec, vec, hbm, hbm, hbm],
            out_specs=pl.BlockSpec((DB, D), lambda b, *_: (b, 0)),
            scratch_shapes=[pltpu.VMEM((2, D, D_EXPERT), F32), pltpu.VMEM((2, D, D_EXPERT), F32),
                            pltpu.VMEM((2, D_EXPERT, D), F32),
                            pltpu.VMEM((D, D_EXPERT), BF16), pltpu.VMEM((D, D_EXPERT), BF16),
                            pltpu.VMEM((D_EXPERT, D), BF16),
                            pltpu.SemaphoreType.DMA((2, 3))]),
        out_shape=jax.ShapeDtypeStruct((P, D), F32),
        compiler_params=_params(("arbitrary",)),
        name="moe_experts",
    )(*plan, xb, sc2, sh2, w_g, w_u, w_d)


def _combine_body(dest_ref, x1_ref, wt_ref, g2_ref, lng_ref, lnb_ref, yb_hbm, o_ref,
                  buf0, buf1, sem, *, tm, alpha):
    i = pl.program_id(0)
    n_tokens = pl.num_programs(0) * tm
    bufs = (buf0, buf1)

    def row_copy(d, k, r):
        return pltpu.make_async_copy(yb_hbm.at[pl.ds(d, 1)], bufs[k].at[pl.ds(r, 1)], sem)

    def issue(g, carry):
        for u in range(ROW_UNROLL):
            r = g * ROW_UNROLL + u
            t = i * tm + r
            for k in range(2):
                row_copy(dest_ref[k * n_tokens + t], k, r).start()
        return carry

    def wait(g, carry):
        for _ in range(ROW_UNROLL):
            for k in range(2):
                row_copy(0, k, 0).wait()
        return carry

    lax.fori_loop(0, tm // ROW_UNROLL, issue, 0)
    lax.fori_loop(0, tm // ROW_UNROLL, wait, 0)
    wt = wt_ref[...]
    y = buf0[...] * wt[:, 0:1] + buf1[...] * wt[:, 1:2]
    o_ref[...] = _layer_norm(alpha * x1_ref[...] + g2_ref[...] * y, lng_ref[...], lnb_ref[...])


def _combine_ln(x1, yb, dest, w_top, g2, ln_g, ln_b, alpha):
    T, D = x1.shape
    tm = min(256, T)
    row = pl.BlockSpec((tm, D), lambda i, d: (i, 0))
    vec = pl.BlockSpec((1, D), lambda i, d: (0, 0))
    return pl.pallas_call(
        functools.partial(_combine_body, tm=tm, alpha=alpha),
        grid_spec=pltpu.PrefetchScalarGridSpec(
            num_scalar_prefetch=1,
            grid=(T // tm,),
            in_specs=[row, pl.BlockSpec((tm, 2), lambda i, d: (i, 0)), vec, vec, vec,
                      pl.BlockSpec(memory_space=pl.ANY)],
            out_specs=row,
            scratch_shapes=[pltpu.VMEM((tm, D), F32), pltpu.VMEM((tm, D), F32),
                            pltpu.SemaphoreType.DMA(())]),
        out_shape=jax.ShapeDtypeStruct((T, D), F32),
        compiler_params=_params(("arbitrary",)),
        name="moe_combine_ln",
    )(dest, x1, w_top, g2, ln_g, ln_b, yb)


def _route(route, counts):
    T = route.shape[0]
    DB = DISPATCH_BLOCK
    w_top = route[:, ROUTE_WEIGHT:ROUTE_WEIGHT + 2]
    by_choice = route[:, :8].T
    expert_id = by_choice[ROUTE_EXPERT:ROUTE_EXPERT + 2].astype(jnp.int32)
    rank = by_choice[ROUTE_RANK:ROUTE_RANK + 2].astype(jnp.int32)
    counts = counts[0, N_GROUPS:N_GROUPS + N_EXPERTS].astype(jnp.int32)
    padded = (counts + DB - 1) // DB * DB
    pend = jnp.cumsum(padded)
    pstart = pend - padded
    dest = (pstart[expert_id] + rank).astype(jnp.int32)
    P = ((2 * T + DB - 1) // DB) * DB + N_EXPERTS * DB
    nblk = P // DB
    block_e = jnp.minimum(jnp.searchsorted(pend, jnp.arange(nblk, dtype=jnp.int32) * DB, side='right'),
                          N_EXPERTS - 1).astype(jnp.int32)
    n_used = (pend[-1:] // DB).astype(jnp.int32)
    b_idx = jnp.arange(nblk, dtype=jnp.int32)
    first = (b_idx < n_used) & ((b_idx == 0) | (block_e != jnp.roll(block_e, 1)))
    slot = (jnp.cumsum(first.astype(jnp.int32)) - 1) % 2
    after = (pend // DB).astype(jnp.int32)[block_e]
    next_e = jnp.where(after < n_used, block_e[jnp.minimum(after, nblk - 1)], -1)
    plan = (block_e, n_used, first.astype(jnp.int32), slot.astype(jnp.int32), next_e.astype(jnp.int32))
    return dest.reshape(-1), w_top, plan, P


def kernel(x, c, w_ada, b_ada, w_in, w_branch_gate, b_branch_gate, attn_sinks, lambda_q1, lambda_k1,
           lambda_q2, lambda_k2, subln_g, w_branch, w_out, ln1_g, ln1_b, w_router_group, b_router_group,
           w_router_expert, b_router_expert, w_exp_gate, w_exp_up, w_exp_down, ln2_g, ln2_b):
    B, S, D = x.shape
    assert B == 1 and D == D_MODEL
    depth = w_in.shape[0]
    alpha = ALPHA
    xs = x.reshape(S, D)
    mod = _ada_mod(c, w_ada, b_ada)
    diff_slopes = jnp.exp2(-8.0 * jnp.arange(1, DIFF_HEADS + 1, dtype=F32) / DIFF_HEADS)
    diff_slopes = jnp.stack([diff_slopes, 1.0 / diff_slopes])
    zero_bias = jnp.zeros((1, w_in.shape[2]), F32)
    pad = LANES - N_GROUPS - N_EXPERTS
    w_eg = w_exp_gate.reshape(depth * N_EXPERTS, D, D_EXPERT)
    w_eu = w_exp_up.reshape(depth * N_EXPERTS, D, D_EXPERT)
    w_ed = w_exp_down.reshape(depth * N_EXPERTS, D_EXPERT, D)
    for l in range(depth):
        lambda_init = 0.8 - 0.6 * math.exp(-0.3 * l)
        sh1, sc1, g1, sh2, sc2, g2 = [mod[l, :, n * D:(n + 1) * D] for n in range(6)]
        proj = _mod_matmul(xs, sc1, sh1, w_in, l, zero_bias, sigmoid=False, name="in_proj")
        gates = _mod_matmul(xs, sc1, sh1, w_branch_gate, l, b_branch_gate[l].reshape(1, -1),
                            sigmoid=True, name="branch_gates")
        o_a = _swa_attention(proj, attn_sinks[l])
        o_b = _sb_attention(proj)
        o_c = _diff_attention(proj, diff_slopes, lambda_q1[l], lambda_k1[l], lambda_q2[l], lambda_k2[l],
                              subln_g[l], lambda_init)
        z = _branch_merge(o_a, o_b, o_c, w_branch, l, gates)
        w_router = jnp.pad(jnp.concatenate([w_router_group[l], w_router_expert[l]], axis=1), ((0, 0), (0, pad)))
        b_router = jnp.pad(jnp.concatenate([b_router_group[l], b_router_expert[l]]), (0, pad)).reshape(1, LANES)
        x1, route, counts = _mixer_out(z, w_out[l].astype(BF16), xs, g1, ln1_g[l].reshape(1, D),
                                       ln1_b[l].reshape(1, D), sc2, sh2, w_router, b_router, alpha)
        dest, w_top, plan, P = _route(route, counts)
        xb = _dispatch(x1, dest, P)
        yb = _experts(xb, sc2, sh2, w_eg, w_eu, w_ed, plan, l)
        xs = _combine_ln(x1, yb, dest, w_top, g2, ln2_g[l].reshape(1, D), ln2_b[l].reshape(1, D), alpha)
    return xs.reshape(B, S, D)
```

```python
import functools
import math

import jax
import jax.numpy as jnp
from jax import lax
from jax.experimental import pallas as pl
from jax.experimental.pallas import tpu as pltpu

F32 = jnp.float32
BF16 = jnp.bfloat16

D_MODEL = 2048
HEAD_DIM = 64
SWA_Q_HEADS = 16
SWA_KV_HEADS = 4
WINDOW = 128
SB_HEADS = 16
DIFF_HEADS = 8
BLOCK_Q = 128
BRANCH_WIDTH = 1024
A_KV = 256
N_BRANCHES = 3
N_GROUPS = 8
EXPERTS_PER_GROUP = 8
N_EXPERTS = 64
D_EXPERT = 384
DISPATCH_BLOCK = 128
DEPTH = 4
ALPHA = (2.0 * DEPTH) ** 0.25
LN_EPS = 1e-5
RMS_EPS = 1e-5
QK_SCALE = 1.0 / math.sqrt(HEAD_DIM)

COL_QA, COL_KA, COL_VA = 0, 1024, 1280
COL_QB, COL_KB, COL_VB = 1536, 2560, 3584
COL_QC, COL_KC, COL_VC = 4608, 5632, 6656

LANES = 128
VMEM_LIMIT = 56 * 1024 * 1024
NEG = -1e30
SB_EXIT = 88.0
ALIBI_CUT = 100.0
POS_SPLIT = 64
SUM_ROWS = 16
RANK_BITS = 16
ROW_UNROLL = 8


def _params(sem):
    return pltpu.CompilerParams(dimension_semantics=sem, vmem_limit_bytes=VMEM_LIMIT)


def _ada_body(c_ref, w_ref, b_ref, o_ref):
    o_ref[0] = jnp.sum(c_ref[...] * w_ref[0], axis=0, keepdims=True) + b_ref[0]


def _ada_mod(c, w_ada, b_ada):
    L, D, N = w_ada.shape
    tn = 512
    return pl.pallas_call(
        _ada_body,
        grid=(L, N // tn),
        in_specs=[pl.BlockSpec((D, 1), lambda l, j: (0, 0)),
                  pl.BlockSpec((1, D, tn), lambda l, j: (l, 0, j)),
                  pl.BlockSpec((1, 1, tn), lambda l, j: (l, 0, j))],
        out_specs=pl.BlockSpec((1, 1, tn), lambda l, j: (l, 0, j)),
        out_shape=jax.ShapeDtypeStruct((L, 1, N), F32),
        compiler_params=_params(("arbitrary", "arbitrary")),
        name="ada_mod",
    )(c.reshape(D, 1), w_ada, b_ada.reshape(L, 1, N))


def _modmm_body(x_ref, sc_ref, sh_ref, w_ref, b_ref, o_ref, h_ref, *, sigmoid):
    @pl.when(pl.program_id(1) == 0)
    def _():
        h_ref[...] = (x_ref[...] * (1.0 + sc_ref[...]) + sh_ref[...]).astype(BF16)

    acc = jnp.dot(h_ref[...], w_ref[0].astype(BF16), preferred_element_type=F32) + b_ref[...]
    if sigmoid:
        acc = 1.0 / (1.0 + jnp.exp(-acc))
    o_ref[...] = acc.astype(o_ref.dtype)


def _mod_matmul(x, sc, sh, w, layer, b, *, sigmoid, name):
    M, K = x.shape
    N = w.shape[2]
    tm = min(1024, M)
    tn = 768
    return pl.pallas_call(
        functools.partial(_modmm_body, sigmoid=sigmoid),
        grid=(M // tm, N // tn),
        in_specs=[pl.BlockSpec((tm, K), lambda i, j: (i, 0)),
                  pl.BlockSpec((1, K), lambda i, j: (0, 0)),
                  pl.BlockSpec((1, K), lambda i, j: (0, 0)),
                  pl.BlockSpec((1, K, tn), lambda i, j: (layer, 0, j)),
                  pl.BlockSpec((1, tn), lambda i, j: (0, j))],
        out_specs=pl.BlockSpec((tm, tn), lambda i, j: (i, j)),
        out_shape=jax.ShapeDtypeStruct((M, N), BF16),
        scratch_shapes=[pltpu.VMEM((tm, K), BF16)],
        compiler_params=_params(("arbitrary", "arbitrary")),
        name=name,
    )(x, sc, sh, w, b)


def _swa_body(sinks_ref, q_ref, kp_ref, kc_ref, vp_ref, vc_ref, o_ref):
    i = pl.program_id(0)
    k = jnp.concatenate([kp_ref[...], kc_ref[...]], axis=0)
    v = jnp.concatenate([vp_ref[...], vc_ref[...]], axis=0)
    qi = lax.broadcasted_iota(jnp.int32, (BLOCK_Q, 2 * BLOCK_Q), 0)
    ki = lax.broadcasted_iota(jnp.int32, (BLOCK_Q, 2 * BLOCK_Q), 1)
    dist = qi + BLOCK_Q - ki
    valid = (dist >= 0) & (dist < WINDOW) & ((ki >= BLOCK_Q) | (i > 0))
    distf = dist.astype(F32)
    group = SWA_Q_HEADS // SWA_KV_HEADS
    outs = []
    for h in range(SWA_Q_HEADS):
        g = h // group
        slope = 2.0 ** (-8.0 * (h + 1) / SWA_Q_HEADS)
        qh = q_ref[:, h * HEAD_DIM:(h + 1) * HEAD_DIM]
        kg = k[:, g * HEAD_DIM:(g + 1) * HEAD_DIM]
        vg = v[:, g * HEAD_DIM:(g + 1) * HEAD_DIM]
        s = lax.dot_general(qh, kg, (((1,), (1,)), ((), ())), preferred_element_type=F32) * QK_SCALE
        s = jnp.where(valid, s - slope * distf, NEG)
        sink = sinks_ref[h]
        m = jnp.maximum(jnp.max(s, axis=1, keepdims=True), sink)
        p = jnp.where(valid, jnp.exp(s - m), 0.0)
        denom = jnp.sum(p, axis=1, keepdims=True) + jnp.exp(sink - m)
        o = jnp.dot(p.astype(BF16), vg, preferred_element_type=F32) / denom
        outs.append(o)
    o_ref[...] = jnp.concatenate(outs, axis=1).astype(o_ref.dtype)


def _swa_attention(proj, sinks):
    S = proj.shape[0]
    nb = S // BLOCK_Q
    kv_blk = lambda col: pl.BlockSpec((BLOCK_Q, A_KV), lambda i: (i, col // A_KV))
    kv_prev = lambda col: pl.BlockSpec((BLOCK_Q, A_KV), lambda i: (jnp.maximum(i - 1, 0), col // A_KV))
    return pl.pallas_call(
        _swa_body,
        grid=(nb,),
        in_specs=[pl.BlockSpec(memory_space=pltpu.SMEM),
                  pl.BlockSpec((BLOCK_Q, BRANCH_WIDTH), lambda i: (i, COL_QA // BRANCH_WIDTH)),
                  kv_prev(COL_KA), kv_blk(COL_KA), kv_prev(COL_VA), kv_blk(COL_VA)],
        out_specs=pl.BlockSpec((BLOCK_Q, BRANCH_WIDTH), lambda i: (i, 0)),
        out_shape=jax.ShapeDtypeStruct((S, BRANCH_WIDTH), BF16),
        compiler_params=_params(("arbitrary",)),
        name="swa_attention",
    )(sinks, proj, proj, proj, proj, proj)


def _sb_body(q_ref, k_ref, v_ref, o_ref, vt_ref, later_ref, carry_ref, acc_ref, *, tq):
    g = pl.program_id(1)
    nkb = vt_ref.shape[0]
    lane = lax.broadcasted_iota(jnp.int32, (1, LANES), 1)

    @pl.when(g == 0)
    def _():
        def setup(c, carry):
            off = pl.multiple_of(c * tq, tq)
            vt_ref[c] = v_ref[pl.ds(off, tq), :].astype(F32).T.astype(BF16)
            return carry

        lax.fori_loop(0, nkb, setup, 0)
        this_key = lax.broadcasted_iota(jnp.int32, later_ref.shape, 0)
        other_key = lax.broadcasted_iota(jnp.int32, later_ref.shape, 1)
        later_ref[...] = (other_key > this_key).astype(BF16)

    qs = q_ref[...] * QK_SCALE
    acc_ref[...] = jnp.zeros_like(acc_ref)

    def stream(tile, half, first, carry):
        in_half = (lane >= HEAD_DIM) if half else (lane < HEAD_DIM)
        q_tile = qs[tile * tq:(tile + 1) * tq, :]
        return dict(qm=jnp.where(in_half, q_tile, jnp.zeros_like(q_tile)), tile=tile, half=half,
                    first=first, carry=carry)

    def process(streams, nb, masked):
        n = nb * tq
        later = later_ref[...]
        if masked:
            k_row = lax.broadcasted_iota(jnp.int32, (n, tq), 0)
            q_col = lax.broadcasted_iota(jnp.int32, (n, tq), 1)
            before = k_row < q_col + (nb - 1) * tq
        zs = []
        for st in streams:
            off = pl.multiple_of(st["first"] * tq, tq)
            zs.append(lax.dot_general(k_ref[pl.ds(off, n), :], st["qm"], (((1,), (1,)), ((), ())),
                                      preferred_element_type=F32))
        log_rems, splits = [], []
        for z in zs:
            soft = jnp.log(1.0 + jnp.exp(-jnp.abs(z)))
            log_rem = -(jnp.maximum(z, 0.0) + soft)
            if masked:
                log_rem = jnp.where(before, log_rem, 0.0)
            hi = log_rem.astype(BF16)
            lo = (log_rem - hi.astype(F32)).astype(BF16)
            log_rems.append(log_rem)
            splits.append((hi, lo))
        afters, totals = [], []
        for (hi, lo), log_rem in zip(splits, log_rems):
            parts, total = [None] * nb, jnp.zeros((1, tq), F32)
            for b in reversed(range(nb)):
                blk = slice(b * tq, (b + 1) * tq)
                inside = (jnp.dot(later, hi[blk, :], preferred_element_type=F32)
                          + jnp.dot(later, lo[blk, :], preferred_element_type=F32))
                parts[b] = inside + total
                total = total + inside[0:1, :] + log_rem[b * tq:b * tq + 1, :]
            afters.append(parts[0] if nb == 1 else jnp.concatenate(parts, axis=0))
            totals.append(total)
        new_carries = []
        for st, z, log_rem, after, total in zip(streams, zs, log_rems, afters, totals):
            w = jnp.exp((log_rem + z) + (after + st["carry"]))
            if masked:
                w = jnp.where(before, w, 0.0)
            w = w.astype(BF16)
            rows = slice(st["half"] * HEAD_DIM, (st["half"] + 1) * HEAD_DIM)
            for b in range(nb):
                acc_ref[st["tile"], st["half"]] += jnp.dot(vt_ref[st["first"] + b][rows, :],
                                                           w[b * tq:(b + 1) * tq, :], preferred_element_type=F32)
            new_carries.append(st["carry"] + total)
        return new_carries

    zero = jnp.zeros((1, tq), F32)
    tiles_heads = [(tile, half) for tile in range(2) for half in range(2)]

    def keep(carries, pairs):
        for (tile, half), carry in zip(pairs, carries):
            carry_ref[tile, half] = carry

    @pl.when(g == 0)
    def _():
        keep(process([stream(0, half, 0, zero) for half in range(2)], 1, True), tiles_heads[:2])
        keep(process([stream(1, half, 0, zero) for half in range(2)], 2, True), tiles_heads[2:])

    @pl.when(g > 0)
    def _():
        keep(process([stream(tile, half, 2 * g + tile - 1, zero) for tile, half in tiles_heads], 2, True),
             tiles_heads)

    def top(carries):
        return jnp.maximum(jnp.max(carries[0]), jnp.max(carries[1]))

    for tile in range(2):
        carries = (carry_ref[tile, 0], carry_ref[tile, 1])

        def cond(state):
            j, _, best = state
            return (j >= 0) & (best > -SB_EXIT)

        def body(state, tile=tile):
            j, carries, _ = state
            carries = tuple(process([stream(tile, half, j, carries[half]) for half in range(2)], 1, False))
            return j - 1, carries, top(carries)

        lax.while_loop(cond, body, (2 * g + tile - 2, carries, top(carries)))
        o_ref[tile * tq:(tile + 1) * tq, :] = jnp.concatenate(
            [acc_ref[tile, 0], acc_ref[tile, 1]], axis=0).T.astype(o_ref.dtype)


def _sb_attention(proj):
    S = proj.shape[0]
    tq = min(256, S // 2)
    npair = BRANCH_WIDTH // LANES
    return pl.pallas_call(
        functools.partial(_sb_body, tq=tq),
        grid=(npair, S // (2 * tq)),
        in_specs=[pl.BlockSpec((2 * tq, LANES), lambda p, g: (g, COL_QB // LANES + p)),
                  pl.BlockSpec((S, LANES), lambda p, g: (0, COL_KB // LANES + p)),
                  pl.BlockSpec((S, LANES), lambda p, g: (0, COL_VB // LANES + p))],
        out_specs=pl.BlockSpec((2 * tq, LANES), lambda p, g: (g, p)),
        out_shape=jax.ShapeDtypeStruct((S, BRANCH_WIDTH), BF16),
        scratch_shapes=[pltpu.VMEM((S // tq, LANES, tq), BF16),
                        pltpu.VMEM((tq, tq), BF16),
                        pltpu.VMEM((2, 2, 1, tq), F32),
                        pltpu.VMEM((2, 2, HEAD_DIM, tq), F32)],
        compiler_params=_params(("arbitrary", "arbitrary")),
        name="stickbreak_attention",
    )(proj, proj, proj)


def _diff_body(slopes_ref, q_ref, k_ref, v_ref, lq1_ref, lk1_ref, lq2_ref, lk2_ref, g_ref, o_ref,
               kaug_ref, vt_ref, knorm_ref, qa_ref, sta_ref, stb_ref, bmaxa_ref, bmaxb_ref, m_ref, acc_ref, *, tq,
               lambda_init):
    h = pl.program_id(0)
    i = pl.program_id(1)
    slope = slopes_ref[0, h]
    inv_slope = slopes_ref[1, h]
    nkb = kaug_ref.shape[0] // tq
    lane = lax.broadcasted_iota(jnp.int32, (1, LANES), 1)

    def max_sq_norms(x):
        sq = x.astype(F32)
        sq = sq * sq
        first = jnp.sum(jnp.where(lane < HEAD_DIM, sq, 0.0), axis=1, keepdims=True)
        second = jnp.sum(jnp.where(lane >= HEAD_DIM, sq, 0.0), axis=1, keepdims=True)
        return jnp.max(first, axis=0, keepdims=True), jnp.max(second, axis=0, keepdims=True)

    @pl.when(i == 0)
    def _():
        def setup(c, carry):
            off = pl.multiple_of(c * tq, tq)
            n1, n2 = max_sq_norms(k_ref[pl.ds(off, tq), :])
            carry = (jnp.maximum(carry[0], n1), jnp.maximum(carry[1], n2))
            pos = off + lax.broadcasted_iota(jnp.int32, (tq, LANES), 0)
            lanes = lax.broadcasted_iota(jnp.int32, (tq, LANES), 1)
            coarse = (pos // POS_SPLIT) * POS_SPLIT
            posm = jnp.where(lanes == 0, coarse, jnp.where(lanes == 1, pos - coarse, 0))
            kaug_ref[pl.ds(off, tq), 0:LANES] = k_ref[pl.ds(off, tq), :]
            kaug_ref[pl.ds(off, tq), LANES:2 * LANES] = posm.astype(F32).astype(BF16)
            vt_ref[c, 0:LANES, :] = v_ref[pl.ds(off, tq), :].astype(F32).T.astype(BF16)
            ones_row = lax.broadcasted_iota(jnp.int32, (SUM_ROWS, tq), 0) == 0
            vt_ref[c, LANES:LANES + SUM_ROWS, :] = jnp.where(ones_row, 1.0, 0.0).astype(BF16)
            return carry

        zero = jnp.zeros((1, 1), F32)
        knorm_ref[0], knorm_ref[1] = lax.fori_loop(0, nkb, setup, (zero, zero))

    q1n, q2n = max_sq_norms(q_ref[...])
    qk = jnp.sqrt(jnp.maximum(q1n * knorm_ref[0], q2n * knorm_ref[1]))
    reach = (ALIBI_CUT + 2.0 * QK_SCALE * qk) * inv_slope
    keep = jnp.minimum(jnp.floor((reach - 1.0) * (1.0 / tq)) + 1.0, 1e6).astype(jnp.int32)
    n_off = jnp.clip(jnp.max(keep), 0, i)
    j0 = i - n_off

    qs = q_ref[...] * QK_SCALE
    bias_cols = jnp.broadcast_to(jnp.where(lane < 2, slope, 0.0).astype(BF16), (tq, LANES))
    for half in range(2):
        in_half = (lane >= HEAD_DIM) if half else (lane < HEAD_DIM)
        qa_ref[half] = jnp.concatenate([jnp.where(in_half, qs, jnp.zeros_like(qs)), bias_cols], axis=1)
    acc_ref[...] = jnp.zeros_like(acc_ref)
    m_ref[...] = jnp.full_like(m_ref, NEG)
    on_or_below_diagonal = (lax.broadcasted_iota(jnp.int32, (tq, tq), 0)
                            <= lax.broadcasted_iota(jnp.int32, (tq, tq), 1))

    def scores(jb, buf):
        st_ref, bmax_ref = buf
        off = pl.multiple_of(jb * tq, tq)
        kb = kaug_ref[pl.ds(off, tq), :]
        for half in range(2):
            st = lax.dot_general(kb, qa_ref[half], (((1,), (1,)), ((), ())),
                                 preferred_element_type=F32)
            st_ref[half] = st
            bmax_ref[half] = jnp.max(st, axis=0, keepdims=True)

    def softmax_pv(jb, buf, diagonal):
        st_ref, bmax_ref = buf
        vtb = vt_ref[jb]
        alphas, ps = [], []
        for half in range(2):
            st = st_ref[half]
            if diagonal:
                st = jnp.where(on_or_below_diagonal, st, NEG)
                block_max = jnp.max(st, axis=0, keepdims=True)
            else:
                block_max = bmax_ref[half]
            m_old = m_ref[half]
            m_new = jnp.maximum(m_old, block_max)
            alpha = jnp.exp(m_old - m_new)
            p = jnp.exp(st - m_new)
            m_ref[half] = m_new
            alphas.append(alpha)
            ps.append(p.astype(BF16))
        for half in range(2):
            acc_ref[half] = alphas[half] * acc_ref[half] + jnp.dot(vtb, ps[half], preferred_element_type=F32)

    buf_a, buf_b = (sta_ref, bmaxa_ref), (stb_ref, bmaxb_ref)
    scores(j0, buf_a)

    def pair(p, carry):
        scores(j0 + 2 * p + 1, buf_b)
        softmax_pv(j0 + 2 * p, buf_a, False)
        scores(j0 + 2 * p + 2, buf_a)
        softmax_pv(j0 + 2 * p + 1, buf_b, False)
        return carry

    lax.fori_loop(0, n_off // 2, pair, 0)

    @pl.when(n_off % 2 == 0)
    def _():
        softmax_pv(i, buf_a, True)

    @pl.when(n_off % 2 == 1)
    def _():
        scores(i, buf_b)
        softmax_pv(i - 1, buf_a, False)
        softmax_pv(i, buf_b, True)

    lam = (jnp.exp(jnp.sum(lq1_ref[...] * lk1_ref[...], axis=1, keepdims=True))
           - jnp.exp(jnp.sum(lq2_ref[...] * lk2_ref[...], axis=1, keepdims=True)) + lambda_init)
    o = (acc_ref[0, 0:LANES, :] * (1.0 / acc_ref[0, LANES:LANES + 1, :])
         - acc_ref[1, 0:LANES, :] * (lam / acc_ref[1, LANES:LANES + 1, :]))
    y = o * lax.rsqrt(jnp.mean(o * o, axis=0, keepdims=True) + RMS_EPS) * g_ref[...]
    o_ref[...] = (y * (1.0 - lambda_init)).T.astype(o_ref.dtype)


def _diff_attention(proj, slopes, lq1, lk1, lq2, lk2, subln_g, lambda_init):
    S = proj.shape[0]
    tq = min(512, S)
    vec = lambda n: pl.BlockSpec((1, n), lambda h, i: (0, 0))
    return pl.pallas_call(
        functools.partial(_diff_body, tq=tq, lambda_init=lambda_init),
        grid=(DIFF_HEADS, S // tq),
        in_specs=[pl.BlockSpec(memory_space=pltpu.SMEM),
                  pl.BlockSpec((tq, LANES), lambda h, i: (i, COL_QC // LANES + h)),
                  pl.BlockSpec((S, LANES), lambda h, i: (0, COL_KC // LANES + h)),
                  pl.BlockSpec((S, LANES), lambda h, i: (0, COL_VC // LANES + h)),
                  vec(HEAD_DIM), vec(HEAD_DIM), vec(HEAD_DIM), vec(HEAD_DIM),
                  pl.BlockSpec((2 * HEAD_DIM, 1), lambda h, i: (0, 0))],
        out_specs=pl.BlockSpec((tq, LANES), lambda h, i: (i, h)),
        out_shape=jax.ShapeDtypeStruct((S, BRANCH_WIDTH), BF16),
        scratch_shapes=[pltpu.VMEM((S, 2 * LANES), BF16),
                        pltpu.VMEM((S // tq, LANES + SUM_ROWS, tq), BF16),
                        pltpu.VMEM((2, 1, 1), F32),
                        pltpu.VMEM((2, tq, 2 * LANES), BF16),
                        pltpu.VMEM((2, tq, tq), F32), pltpu.VMEM((2, tq, tq), F32),
                        pltpu.VMEM((2, 1, tq), F32), pltpu.VMEM((2, 1, tq), F32),
                        pltpu.VMEM((2, 1, tq), F32),
                        pltpu.VMEM((2, LANES + SUM_ROWS, tq), F32)],
        compiler_params=_params(("arbitrary", "arbitrary")),
        name="diff_attention",
    )(slopes, proj, proj, proj, lq1.reshape(1, -1), lk1.reshape(1, -1), lq2.reshape(1, -1),
      lk2.reshape(1, -1), subln_g.reshape(-1, 1))


def _branch_body(oa_ref, ob_ref, oc_ref, w_ref, ga_ref, gb_ref, gc_ref, z_ref):
    z = None
    for n, (o_ref, g_ref) in enumerate(((oa_ref, ga_ref), (ob_ref, gb_ref), (oc_ref, gc_ref))):
        y = jnp.dot(o_ref[...], w_ref[0, n].astype(BF16), preferred_element_type=F32)
        t = g_ref[...].astype(F32) * y
        z = t if z is None else z + t
    z_ref[...] = z.astype(z_ref.dtype)


def _branch_merge(o_a, o_b, o_c, w_branch, layer, gates):
    S = o_a.shape[0]
    D = w_branch.shape[3]
    tm = min(1024, S)
    tn = 512
    nj = D // tn
    o_spec = pl.BlockSpec((tm, BRANCH_WIDTH), lambda i, j: (i, 0))
    gate_spec = lambda n: pl.BlockSpec((tm, tn), lambda i, j: (i, n * nj + j))
    return pl.pallas_call(
        _branch_body,
        grid=(S // tm, nj),
        in_specs=[o_spec, o_spec, o_spec,
                  pl.BlockSpec((1, N_BRANCHES, BRANCH_WIDTH, tn), lambda i, j: (layer, 0, 0, j)),
                  gate_spec(0), gate_spec(1), gate_spec(2)],
        out_specs=pl.BlockSpec((tm, tn), lambda i, j: (i, j)),
        out_shape=jax.ShapeDtypeStruct((S, D), BF16),
        compiler_params=_params(("arbitrary", "arbitrary")),
        name="branch_merge",
    )(o_a, o_b, o_c, w_branch, gates, gates, gates)


def _layer_norm(r, g, b):
    mu = jnp.mean(r, axis=1, keepdims=True)
    d = r - mu
    var = jnp.mean(d * d, axis=1, keepdims=True)
    return d * lax.rsqrt(var + LN_EPS) * g + b


def _split_bf16(a):
    hi = a.astype(BF16)
    return hi, (a - hi.astype(F32)).astype(BF16)


ROUTE_EXPERT, ROUTE_WEIGHT, ROUTE_RANK = 0, 2, 4


def _route_tile(logits, counts):
    rows = logits.shape[0]
    lane = lax.broadcasted_iota(jnp.int32, logits.shape, 1)
    row_min = lambda cond: jnp.min(jnp.where(cond, lane, LANES), axis=1, keepdims=True)
    row_max = lambda cond: jnp.max(jnp.where(cond, logits, NEG), axis=1, keepdims=True)
    is_group = lane < N_GROUPS
    g_max = row_max(is_group)
    g_top = row_min(is_group & (logits == g_max))
    p_group = 1.0 / jnp.sum(jnp.where(is_group, jnp.exp(logits - g_max), 0.0), axis=1, keepdims=True)
    e_lane = lane - N_GROUPS
    in_group = ((e_lane >= 0) & (e_lane < N_EXPERTS)
                & (jnp.right_shift(e_lane, EXPERTS_PER_GROUP.bit_length() - 1) == g_top))
    v1 = row_max(in_group)
    i1 = row_min(in_group & (logits == v1))
    rest = in_group & (lane != i1)
    v2 = row_max(rest)
    i2 = row_min(rest & (logits == v2))
    ratio = jnp.exp(v2 - v1)
    w1 = p_group / (1.0 + ratio)
    w2 = w1 * ratio
    member = (lane == i1) | (lane == i2)
    earlier = (lax.broadcasted_iota(jnp.int32, (rows, rows), 1)
               < lax.broadcasted_iota(jnp.int32, (rows, rows), 0)).astype(BF16)
    before = jnp.dot(earlier, member.astype(BF16), preferred_element_type=F32) + counts
    pick = lambda idx: jnp.sum(jnp.where(lane == idx, before, 0.0), axis=1, keepdims=True)
    record = jnp.zeros(logits.shape, F32)
    for at, value in ((ROUTE_EXPERT, (i1 - N_GROUPS).astype(F32)), (ROUTE_EXPERT + 1, (i2 - N_GROUPS).astype(F32)),
                      (ROUTE_WEIGHT, w1), (ROUTE_WEIGHT + 1, w2),
                      (ROUTE_RANK, pick(i1)), (ROUTE_RANK + 1, pick(i2))):
        record = jnp.where(lane == at, value, record)
    return record, counts + jnp.sum(member.astype(F32), axis=0, keepdims=True)


def _out_body(z_ref, w_ref, x_ref, g1_ref, lng_ref, lnb_ref, sc2_ref, sh2_ref, wr_hi_ref, wr_lo_ref, br_ref,
              x1_ref, route_ref, counts_ref, running_ref, *, alpha):
    @pl.when(pl.program_id(0) == 0)
    def _():
        running_ref[...] = jnp.zeros_like(running_ref)

    y = jnp.dot(z_ref[...], w_ref[...], preferred_element_type=F32)
    x1 = _layer_norm(alpha * x_ref[...] + g1_ref[...] * y, lng_ref[...], lnb_ref[...])
    x1_ref[...] = x1
    h_hi, h_lo = _split_bf16(x1 * (1.0 + sc2_ref[...]) + sh2_ref[...])
    w_hi = wr_hi_ref[...]
    logits = (jnp.dot(h_hi, w_hi, preferred_element_type=F32)
              + (jnp.dot(h_hi, wr_lo_ref[...], preferred_element_type=F32)
                 + jnp.dot(h_lo, w_hi, preferred_element_type=F32))) + br_ref[...]
    route_ref[...], counts = _route_tile(logits, running_ref[...])
    running_ref[...] = counts
    counts_ref[...] = jnp.broadcast_to(counts, counts_ref.shape)


def _mixer_out(z, w_out, x, g1, ln_g, ln_b, sc2, sh2, w_router, b_router, alpha):
    S, D = x.shape
    tm = min(256, S)
    row = pl.BlockSpec((tm, D), lambda i: (i, 0))
    vec = pl.BlockSpec((1, D), lambda i: (0, 0))
    router = pl.BlockSpec((D, LANES), lambda i: (0, 0))
    wr_hi, wr_lo = _split_bf16(w_router)
    return pl.pallas_call(
        functools.partial(_out_body, alpha=alpha),
        grid=(S // tm,),
        in_specs=[row, pl.BlockSpec((D, D), lambda i: (0, 0)), row, vec, vec, vec, vec, vec,
                  router, router, pl.BlockSpec((1, LANES), lambda i: (0, 0))],
        out_specs=[row, pl.BlockSpec((tm, LANES), lambda i: (i, 0)), pl.BlockSpec((8, LANES), lambda i: (0, 0))],
        out_shape=[jax.ShapeDtypeStruct((S, D), F32), jax.ShapeDtypeStruct((S, LANES), F32),
                   jax.ShapeDtypeStruct((8, LANES), F32)],
        scratch_shapes=[pltpu.VMEM((1, LANES), F32)],
        compiler_params=_params(("arbitrary",)),
        name="mixer_out_ln",
    )(z, w_out, x, g1, ln_g, ln_b, sc2, sh2, wr_hi, wr_lo, b_router)


def _slot(choice_ref, pstart_ref, at):
    code = choice_ref[at]
    return pstart_ref[code >> RANK_BITS] + (code & ((1 << RANK_BITS) - 1))


def _dispatch_body(choice_ref, pstart_ref, x_ref, xb_in_hbm, xb_hbm, sem, *, rows_per_step):
    del xb_in_hbm
    i = pl.program_id(0)
    n_tokens = pl.num_programs(0) * rows_per_step

    def row_copy(r, d):
        return pltpu.make_async_copy(x_ref.at[pl.ds(r, 1)], xb_hbm.at[pl.ds(d, 1)], sem)

    def issue(g, carry):
        for u in range(ROW_UNROLL):
            r = g * ROW_UNROLL + u
            t = i * rows_per_step + r
            for k in range(2):
                row_copy(r, _slot(choice_ref, pstart_ref, k * n_tokens + t)).start()
        return carry

    def wait(g, carry):
        for _ in range(2 * ROW_UNROLL):
            row_copy(0, 0).wait()
        return carry

    lax.fori_loop(0, rows_per_step // ROW_UNROLL, issue, 0)
    lax.fori_loop(0, rows_per_step // ROW_UNROLL, wait, 0)


def _dispatch(x1, slots, padded_rows):
    T, D = x1.shape
    rows_per_step = min(512, T)
    xb0 = jnp.zeros((padded_rows, D), F32)
    return pl.pallas_call(
        functools.partial(_dispatch_body, rows_per_step=rows_per_step),
        grid_spec=pltpu.PrefetchScalarGridSpec(
            num_scalar_prefetch=2,
            grid=(T // rows_per_step,),
            in_specs=[pl.BlockSpec((rows_per_step, D), lambda i, *_: (i, 0)),
                      pl.BlockSpec(memory_space=pl.ANY)],
            out_specs=pl.BlockSpec(memory_space=pl.ANY),
            scratch_shapes=[pltpu.SemaphoreType.DMA(())]),
        out_shape=jax.ShapeDtypeStruct((padded_rows, D), F32),
        input_output_aliases={3: 0},
        compiler_params=_params(("arbitrary",)),
        name="moe_dispatch",
    )(*slots, x1, xb0)


def _expert_body(be_ref, nu_ref, first_ref, slot_ref, next_ref, xb_ref, sc_ref, sh_ref,
                 wg_hbm, wu_hbm, wd_hbm, yb_ref, wg_f, wu_f, wd_f, wg_s, wu_s, wd_s, sem, *, layer):
    b = pl.program_id(0)
    nu = nu_ref[0]
    base = layer * N_EXPERTS

    def weight_copies(e, slot):
        return (pltpu.make_async_copy(wg_hbm.at[base + e], wg_f.at[slot], sem.at[slot, 0]),
                pltpu.make_async_copy(wu_hbm.at[base + e], wu_f.at[slot], sem.at[slot, 1]),
                pltpu.make_async_copy(wd_hbm.at[base + e], wd_f.at[slot], sem.at[slot, 2]))

    @pl.when(b == 0)
    def _():
        for copy in weight_copies(be_ref[0], 0):
            copy.start()

    @pl.when((b < nu) & (first_ref[b] == 1))
    def _():
        e = be_ref[b]
        nxt = next_ref[b]
        for slot in range(2):
            @pl.when(slot_ref[b] == slot)
            def _():
                for copy in weight_copies(e, slot):
                    copy.wait()

                @pl.when(nxt >= 0)
                def _():
                    for copy in weight_copies(nxt, 1 - slot):
                        copy.start()

                wg_s[...] = wg_f[slot].astype(BF16)
                wu_s[...] = wu_f[slot].astype(BF16)
                wd_s[...] = wd_f[slot].astype(BF16)

    @pl.when(b < nu)
    def _():
        h = (xb_ref[...] * (1.0 + sc_ref[...]) + sh_ref[...]).astype(BF16)
        a = jnp.dot(h, wg_s[...], preferred_element_type=F32)
        u = jnp.dot(h, wu_s[...], preferred_element_type=F32)
        act = (a / (1.0 + jnp.exp(-a))) * u
        yb_ref[...] = jnp.dot(act.astype(BF16), wd_s[...], preferred_element_type=F32)

    @pl.when(b >= nu)
    def _():
        yb_ref[...] = jnp.zeros_like(yb_ref)


def _experts(xb, sc2, sh2, w_g, w_u, w_d, plan, layer):
    P, D = xb.shape
    DB = DISPATCH_BLOCK
    nblk = P // DB
    blk = lambda b, be, nu, *_: (jnp.minimum(b, nu[0] - 1), 0)
    vec = pl.BlockSpec((1, D), lambda b, *_: (0, 0))
    hbm = pl.BlockSpec(memory_space=pl.ANY)
    return pl.pallas_call(
        functools.partial(_expert_body, layer=layer),
        grid_spec=pltpu.PrefetchScalarGridSpec(
            num_scalar_prefetch=5,
            grid=(nblk,),
            in_specs=[pl.BlockSpec((DB, D), blk), vec, vec, hbm, hbm, hbm],
            out_specs=pl.BlockSpec((DB, D), lambda b, *_: (b, 0)),
            scratch_shapes=[pltpu.VMEM((2, D, D_EXPERT), F32), pltpu.VMEM((2, D, D_EXPERT), F32),
                            pltpu.VMEM((2, D_EXPERT, D), F32),
                            pltpu.VMEM((D, D_EXPERT), BF16), pltpu.VMEM((D, D_EXPERT), BF16),
                            pltpu.VMEM((D_EXPERT, D), BF16),
                            pltpu.SemaphoreType.DMA((2, 3))]),
        out_shape=jax.ShapeDtypeStruct((P, D), F32),
        compiler_params=_params(("arbitrary",)),
        name="moe_experts",
    )(*plan, xb, sc2, sh2, w_g, w_u, w_d)


def _combine_body(choice_ref, pstart_ref, x1_ref, wt_ref, g2_ref, lng_ref, lnb_ref, yb_hbm, o_ref,
                  buf0, buf1, sem, *, tm, alpha):
    i = pl.program_id(0)
    n_tokens = pl.num_programs(0) * tm
    bufs = (buf0, buf1)

    def row_copy(d, k, r):
        return pltpu.make_async_copy(yb_hbm.at[pl.ds(d, 1)], bufs[k].at[pl.ds(r, 1)], sem)

    def issue(g, carry):
        for u in range(ROW_UNROLL):
            r = g * ROW_UNROLL + u
            t = i * tm + r
            for k in range(2):
                row_copy(_slot(choice_ref, pstart_ref, k * n_tokens + t), k, r).start()
        return carry

    def wait(g, carry):
        for _ in range(ROW_UNROLL):
            for k in range(2):
                row_copy(0, k, 0).wait()
        return carry

    lax.fori_loop(0, tm // ROW_UNROLL, issue, 0)
    lax.fori_loop(0, tm // ROW_UNROLL, wait, 0)
    wt = wt_ref[...]
    y = buf0[...] * wt[:, 0:1] + buf1[...] * wt[:, 1:2]
    o_ref[...] = _layer_norm(alpha * x1_ref[...] + g2_ref[...] * y, lng_ref[...], lnb_ref[...])


def _combine_ln(x1, yb, slots, w_top, g2, ln_g, ln_b, alpha):
    T, D = x1.shape
    tm = min(256, T)
    row = pl.BlockSpec((tm, D), lambda i, *_: (i, 0))
    vec = pl.BlockSpec((1, D), lambda i, *_: (0, 0))
    return pl.pallas_call(
        functools.partial(_combine_body, tm=tm, alpha=alpha),
        grid_spec=pltpu.PrefetchScalarGridSpec(
            num_scalar_prefetch=2,
            grid=(T // tm,),
            in_specs=[row, pl.BlockSpec((tm, 2), lambda i, *_: (i, 0)), vec, vec, vec,
                      pl.BlockSpec(memory_space=pl.ANY)],
            out_specs=row,
            scratch_shapes=[pltpu.VMEM((tm, D), F32), pltpu.VMEM((tm, D), F32),
                            pltpu.SemaphoreType.DMA(())]),
        out_shape=jax.ShapeDtypeStruct((T, D), F32),
        compiler_params=_params(("arbitrary",)),
        name="moe_combine_ln",
    )(*slots, x1, w_top, g2, ln_g, ln_b, yb)


def _route(route, counts):
    T = route.shape[0]
    DB = DISPATCH_BLOCK
    w_top = route[:, ROUTE_WEIGHT:ROUTE_WEIGHT + 2]
    by_choice = route[:, :8].T
    expert_id = by_choice[ROUTE_EXPERT:ROUTE_EXPERT + 2].astype(jnp.int32)
    rank = by_choice[ROUTE_RANK:ROUTE_RANK + 2].astype(jnp.int32)
    counts = counts[0, N_GROUPS:N_GROUPS + N_EXPERTS].astype(jnp.int32)
    padded = (counts + DB - 1) // DB * DB
    pend = jnp.cumsum(padded)
    pstart = pend - padded
    choice = ((expert_id << RANK_BITS) | rank).astype(jnp.int32).reshape(-1)
    P = ((2 * T + DB - 1) // DB) * DB + N_EXPERTS * DB
    nblk = P // DB
    block_e = jnp.minimum(jnp.searchsorted(pend, jnp.arange(nblk, dtype=jnp.int32) * DB, side='right'),
                          N_EXPERTS - 1).astype(jnp.int32)
    n_used = (pend[-1:] // DB).astype(jnp.int32)
    b_idx = jnp.arange(nblk, dtype=jnp.int32)
    first = (b_idx < n_used) & ((b_idx == 0) | (block_e != jnp.roll(block_e, 1)))
    slot = (jnp.cumsum(first.astype(jnp.int32)) - 1) % 2
    after = (pend // DB).astype(jnp.int32)[block_e]
    next_e = jnp.where(after < n_used, block_e[jnp.minimum(after, nblk - 1)], -1)
    plan = (block_e, n_used, first.astype(jnp.int32), slot.astype(jnp.int32), next_e.astype(jnp.int32))
    return (choice, pstart.astype(jnp.int32)), w_top, plan, P


def kernel(x, c, w_ada, b_ada, w_in, w_branch_gate, b_branch_gate, attn_sinks, lambda_q1, lambda_k1,
           lambda_q2, lambda_k2, subln_g, w_branch, w_out, ln1_g, ln1_b, w_router_group, b_router_group,
           w_router_expert, b_router_expert, w_exp_gate, w_exp_up, w_exp_down, ln2_g, ln2_b):
    B, S, D = x.shape
    assert B == 1 and D == D_MODEL
    depth = w_in.shape[0]
    alpha = ALPHA
    xs = x.reshape(S, D)
    mod = _ada_mod(c, w_ada, b_ada)
    diff_slopes = jnp.exp2(-8.0 * jnp.arange(1, DIFF_HEADS + 1, dtype=F32) / DIFF_HEADS)
    diff_slopes = jnp.stack([diff_slopes, 1.0 / diff_slopes])
    zero_bias = jnp.zeros((1, w_in.shape[2]), F32)
    pad = LANES - N_GROUPS - N_EXPERTS
    w_eg = w_exp_gate.reshape(depth * N_EXPERTS, D, D_EXPERT)
    w_eu = w_exp_up.reshape(depth * N_EXPERTS, D, D_EXPERT)
    w_ed = w_exp_down.reshape(depth * N_EXPERTS, D_EXPERT, D)
    for l in range(depth):
        lambda_init = 0.8 - 0.6 * math.exp(-0.3 * l)
        sh1, sc1, g1, sh2, sc2, g2 = [mod[l, :, n * D:(n + 1) * D] for n in range(6)]
        proj = _mod_matmul(xs, sc1, sh1, w_in, l, zero_bias, sigmoid=False, name="in_proj")
        gates = _mod_matmul(xs, sc1, sh1, w_branch_gate, l, b_branch_gate[l].reshape(1, -1),
                            sigmoid=True, name="branch_gates")
        o_a = _swa_attention(proj, attn_sinks[l])
        o_b = _sb_attention(proj)
        o_c = _diff_attention(proj, diff_slopes, lambda_q1[l], lambda_k1[l], lambda_q2[l], lambda_k2[l],
                              subln_g[l], lambda_init)
        z = _branch_merge(o_a, o_b, o_c, w_branch, l, gates)
        w_router = jnp.pad(jnp.concatenate([w_router_group[l], w_router_expert[l]], axis=1), ((0, 0), (0, pad)))
        b_router = jnp.pad(jnp.concatenate([b_router_group[l], b_router_expert[l]]), (0, pad)).reshape(1, LANES)
        x1, route, counts = _mixer_out(z, w_out[l].astype(BF16), xs, g1, ln1_g[l].reshape(1, D),
                                       ln1_b[l].reshape(1, D), sc2, sh2, w_router, b_router, alpha)
        slots, w_top, plan, P = _route(route, counts)
        xb = _dispatch(x1, slots, P)
        yb = _experts(xb, sc2, sh2, w_eg, w_eu, w_ed, plan, l)
        xs = _combine_ln(x1, yb, slots, w_top, g2, ln2_g[l].reshape(1, D), ln2_b[l].reshape(1, D), alpha)
    return xs.reshape(B, S, D)
```

```python
import functools
import math

import jax
import jax.numpy as jnp
from jax import lax
from jax.experimental import pallas as pl
from jax.experimental.pallas import tpu as pltpu

F32 = jnp.float32
BF16 = jnp.bfloat16

D_MODEL = 2048
HEAD_DIM = 64
SWA_Q_HEADS = 16
SWA_KV_HEADS = 4
WINDOW = 128
SB_HEADS = 16
DIFF_HEADS = 8
BLOCK_Q = 128
BRANCH_WIDTH = 1024
A_KV = 256
N_BRANCHES = 3
N_GROUPS = 8
EXPERTS_PER_GROUP = 8
N_EXPERTS = 64
D_EXPERT = 384
DISPATCH_BLOCK = 128
DEPTH = 4
ALPHA = (2.0 * DEPTH) ** 0.25
LN_EPS = 1e-5
RMS_EPS = 1e-5
QK_SCALE = 1.0 / math.sqrt(HEAD_DIM)

COL_QA, COL_KA, COL_VA = 0, 1024, 1280
COL_QB, COL_KB, COL_VB = 1536, 2560, 3584
COL_QC, COL_KC, COL_VC = 4608, 5632, 6656

LANES = 128
VMEM_LIMIT = 56 * 1024 * 1024
NEG = -1e30
SB_EXIT = 88.0
ALIBI_CUT = 100.0
POS_SPLIT = 64
SUM_ROWS = 16
RANK_BITS = 16
ROW_UNROLL = 8


def _params(sem):
    return pltpu.CompilerParams(dimension_semantics=sem, vmem_limit_bytes=VMEM_LIMIT)


def _ada_body(c_ref, w_ref, b_ref, o_ref):
    o_ref[0] = jnp.sum(c_ref[...] * w_ref[0], axis=0, keepdims=True) + b_ref[0]


def _ada_mod(c, w_ada, b_ada):
    L, D, N = w_ada.shape
    tn = 512
    return pl.pallas_call(
        _ada_body,
        grid=(L, N // tn),
        in_specs=[pl.BlockSpec((D, 1), lambda l, j: (0, 0)),
                  pl.BlockSpec((1, D, tn), lambda l, j: (l, 0, j)),
                  pl.BlockSpec((1, 1, tn), lambda l, j: (l, 0, j))],
        out_specs=pl.BlockSpec((1, 1, tn), lambda l, j: (l, 0, j)),
        out_shape=jax.ShapeDtypeStruct((L, 1, N), F32),
        compiler_params=_params(("arbitrary", "arbitrary")),
        name="ada_mod",
    )(c.reshape(D, 1), w_ada, b_ada.reshape(L, 1, N))


def _modmm_body(x_ref, sc_ref, sh_ref, w_ref, b_ref, o_ref, h_ref, *, sigmoid):
    @pl.when(pl.program_id(1) == 0)
    def _():
        h_ref[...] = (x_ref[...] * (1.0 + sc_ref[...]) + sh_ref[...]).astype(BF16)

    acc = jnp.dot(h_ref[...], w_ref[0].astype(BF16), preferred_element_type=F32) + b_ref[...]
    if sigmoid:
        acc = 1.0 / (1.0 + jnp.exp(-acc))
    o_ref[...] = acc.astype(o_ref.dtype)


def _mod_matmul(x, sc, sh, w, layer, b, *, sigmoid, name):
    M, K = x.shape
    N = w.shape[2]
    tm = min(1024, M)
    tn = 768
    return pl.pallas_call(
        functools.partial(_modmm_body, sigmoid=sigmoid),
        grid=(M // tm, N // tn),
        in_specs=[pl.BlockSpec((tm, K), lambda i, j: (i, 0)),
                  pl.BlockSpec((1, K), lambda i, j: (0, 0)),
                  pl.BlockSpec((1, K), lambda i, j: (0, 0)),
                  pl.BlockSpec((1, K, tn), lambda i, j: (layer, 0, j)),
                  pl.BlockSpec((1, tn), lambda i, j: (0, j))],
        out_specs=pl.BlockSpec((tm, tn), lambda i, j: (i, j)),
        out_shape=jax.ShapeDtypeStruct((M, N), BF16),
        scratch_shapes=[pltpu.VMEM((tm, K), BF16)],
        compiler_params=_params(("arbitrary", "arbitrary")),
        name=name,
    )(x, sc, sh, w, b)


def _swa_body(sinks_ref, q_ref, kp_ref, kc_ref, vp_ref, vc_ref, o_ref):
    i = pl.program_id(0)
    k = jnp.concatenate([kp_ref[...], kc_ref[...]], axis=0)
    v = jnp.concatenate([vp_ref[...], vc_ref[...]], axis=0)
    qi = lax.broadcasted_iota(jnp.int32, (BLOCK_Q, 2 * BLOCK_Q), 0)
    ki = lax.broadcasted_iota(jnp.int32, (BLOCK_Q, 2 * BLOCK_Q), 1)
    dist = qi + BLOCK_Q - ki
    valid = (dist >= 0) & (dist < WINDOW) & ((ki >= BLOCK_Q) | (i > 0))
    distf = dist.astype(F32)
    group = SWA_Q_HEADS // SWA_KV_HEADS
    outs = []
    for h in range(SWA_Q_HEADS):
        g = h // group
        slope = 2.0 ** (-8.0 * (h + 1) / SWA_Q_HEADS)
        qh = q_ref[:, h * HEAD_DIM:(h + 1) * HEAD_DIM]
        kg = k[:, g * HEAD_DIM:(g + 1) * HEAD_DIM]
        vg = v[:, g * HEAD_DIM:(g + 1) * HEAD_DIM]
        s = lax.dot_general(qh, kg, (((1,), (1,)), ((), ())), preferred_element_type=F32) * QK_SCALE
        s = jnp.where(valid, s - slope * distf, NEG)
        sink = sinks_ref[h]
        m = jnp.maximum(jnp.max(s, axis=1, keepdims=True), sink)
        p = jnp.where(valid, jnp.exp(s - m), 0.0)
        denom = jnp.sum(p, axis=1, keepdims=True) + jnp.exp(sink - m)
        o = jnp.dot(p.astype(BF16), vg, preferred_element_type=F32) / denom
        outs.append(o)
    o_ref[...] = jnp.concatenate(outs, axis=1).astype(o_ref.dtype)


def _swa_attention(proj, sinks):
    S = proj.shape[0]
    nb = S // BLOCK_Q
    kv_blk = lambda col: pl.BlockSpec((BLOCK_Q, A_KV), lambda i: (i, col // A_KV))
    kv_prev = lambda col: pl.BlockSpec((BLOCK_Q, A_KV), lambda i: (jnp.maximum(i - 1, 0), col // A_KV))
    return pl.pallas_call(
        _swa_body,
        grid=(nb,),
        in_specs=[pl.BlockSpec(memory_space=pltpu.SMEM),
                  pl.BlockSpec((BLOCK_Q, BRANCH_WIDTH), lambda i: (i, COL_QA // BRANCH_WIDTH)),
                  kv_prev(COL_KA), kv_blk(COL_KA), kv_prev(COL_VA), kv_blk(COL_VA)],
        out_specs=pl.BlockSpec((BLOCK_Q, BRANCH_WIDTH), lambda i: (i, 0)),
        out_shape=jax.ShapeDtypeStruct((S, BRANCH_WIDTH), BF16),
        compiler_params=_params(("arbitrary",)),
        name="swa_attention",
    )(sinks, proj, proj, proj, proj, proj)


def _sb_body(q_ref, k_ref, v_ref, o_ref, vt_ref, later_ref, carry_ref, acc_ref, *, tq):
    g = pl.program_id(1)
    nkb = vt_ref.shape[0]
    lane = lax.broadcasted_iota(jnp.int32, (1, LANES), 1)

    @pl.when(g == 0)
    def _():
        def setup(c, carry):
            off = pl.multiple_of(c * tq, tq)
            vt_ref[c] = v_ref[pl.ds(off, tq), :].astype(F32).T.astype(BF16)
            return carry

        lax.fori_loop(0, nkb, setup, 0)
        this_key = lax.broadcasted_iota(jnp.int32, later_ref.shape, 0)
        other_key = lax.broadcasted_iota(jnp.int32, later_ref.shape, 1)
        later_ref[...] = (other_key > this_key).astype(BF16)

    qs = q_ref[...] * QK_SCALE
    acc_ref[...] = jnp.zeros_like(acc_ref)

    def stream(tile, half, first, carry):
        in_half = (lane >= HEAD_DIM) if half else (lane < HEAD_DIM)
        q_tile = qs[tile * tq:(tile + 1) * tq, :]
        return dict(qm=jnp.where(in_half, q_tile, jnp.zeros_like(q_tile)), tile=tile, half=half,
                    first=first, carry=carry)

    def process(streams, nb, masked):
        n = nb * tq
        later = later_ref[...]
        if masked:
            k_row = lax.broadcasted_iota(jnp.int32, (n, tq), 0)
            q_col = lax.broadcasted_iota(jnp.int32, (n, tq), 1)
            before = k_row < q_col + (nb - 1) * tq
        zs = []
        for st in streams:
            off = pl.multiple_of(st["first"] * tq, tq)
            zs.append(lax.dot_general(k_ref[pl.ds(off, n), :], st["qm"], (((1,), (1,)), ((), ())),
                                      preferred_element_type=F32))
        log_rems, splits = [], []
        for z in zs:
            soft = jnp.log(1.0 + jnp.exp(-jnp.abs(z)))
            log_rem = -(jnp.maximum(z, 0.0) + soft)
            if masked:
                log_rem = jnp.where(before, log_rem, 0.0)
            hi = log_rem.astype(BF16)
            lo = (log_rem - hi.astype(F32)).astype(BF16)
            log_rems.append(log_rem)
            splits.append((hi, lo))
        afters, totals = [], []
        for (hi, lo), log_rem in zip(splits, log_rems):
            parts, total = [None] * nb, jnp.zeros((1, tq), F32)
            for b in reversed(range(nb)):
                blk = slice(b * tq, (b + 1) * tq)
                inside = (jnp.dot(later, hi[blk, :], preferred_element_type=F32)
                          + jnp.dot(later, lo[blk, :], preferred_element_type=F32))
                parts[b] = inside + total
                total = total + inside[0:1, :] + log_rem[b * tq:b * tq + 1, :]
            afters.append(parts[0] if nb == 1 else jnp.concatenate(parts, axis=0))
            totals.append(total)
        new_carries = []
        for st, z, log_rem, after, total in zip(streams, zs, log_rems, afters, totals):
            w = jnp.exp((log_rem + z) + (after + st["carry"]))
            if masked:
                w = jnp.where(before, w, 0.0)
            w = w.astype(BF16)
            rows = slice(st["half"] * HEAD_DIM, (st["half"] + 1) * HEAD_DIM)
            for b in range(nb):
                acc_ref[st["tile"], st["half"]] += jnp.dot(vt_ref[st["first"] + b][rows, :],
                                                           w[b * tq:(b + 1) * tq, :], preferred_element_type=F32)
            new_carries.append(st["carry"] + total)
        return new_carries

    zero = jnp.zeros((1, tq), F32)
    tiles_heads = [(tile, half) for tile in range(2) for half in range(2)]

    def keep(carries, pairs):
        for (tile, half), carry in zip(pairs, carries):
            carry_ref[tile, half] = carry

    @pl.when(g == 0)
    def _():
        keep(process([stream(0, half, 0, zero) for half in range(2)], 1, True), tiles_heads[:2])
        keep(process([stream(1, half, 0, zero) for half in range(2)], 2, True), tiles_heads[2:])

    @pl.when(g > 0)
    def _():
        keep(process([stream(tile, half, 2 * g + tile - 1, zero) for tile, half in tiles_heads], 2, True),
             tiles_heads)

    def top(carries):
        return jnp.maximum(jnp.max(carries[0]), jnp.max(carries[1]))

    for tile in range(2):
        carries = (carry_ref[tile, 0], carry_ref[tile, 1])

        def cond(state):
            j, _, best = state
            return (j >= 0) & (best > -SB_EXIT)

        def body(state, tile=tile):
            j, carries, _ = state
            carries = tuple(process([stream(tile, half, j, carries[half]) for half in range(2)], 1, False))
            return j - 1, carries, top(carries)

        lax.while_loop(cond, body, (2 * g + tile - 2, carries, top(carries)))
        o_ref[tile * tq:(tile + 1) * tq, :] = jnp.concatenate(
            [acc_ref[tile, 0], acc_ref[tile, 1]], axis=0).T.astype(o_ref.dtype)


def _sb_attention(proj):
    S = proj.shape[0]
    tq = min(256, S // 2)
    npair = BRANCH_WIDTH // LANES
    return pl.pallas_call(
        functools.partial(_sb_body, tq=tq),
        grid=(npair, S // (2 * tq)),
        in_specs=[pl.BlockSpec((2 * tq, LANES), lambda p, g: (g, COL_QB // LANES + p)),
                  pl.BlockSpec((S, LANES), lambda p, g: (0, COL_KB // LANES + p)),
                  pl.BlockSpec((S, LANES), lambda p, g: (0, COL_VB // LANES + p))],
        out_specs=pl.BlockSpec((2 * tq, LANES), lambda p, g: (g, p)),
        out_shape=jax.ShapeDtypeStruct((S, BRANCH_WIDTH), BF16),
        scratch_shapes=[pltpu.VMEM((S // tq, LANES, tq), BF16),
                        pltpu.VMEM((tq, tq), BF16),
                        pltpu.VMEM((2, 2, 1, tq), F32),
                        pltpu.VMEM((2, 2, HEAD_DIM, tq), F32)],
        compiler_params=_params(("arbitrary", "arbitrary")),
        name="stickbreak_attention",
    )(proj, proj, proj)


def _diff_body(slopes_ref, q_ref, k_ref, v_ref, lq1_ref, lk1_ref, lq2_ref, lk2_ref, g_ref, o_ref,
               kaug_ref, vt_ref, knorm_ref, qa_ref, sta_ref, stb_ref, bmaxa_ref, bmaxb_ref, m_ref, acc_ref, *, tq,
               lambda_init):
    h = pl.program_id(0)
    i = pl.program_id(1)
    slope = slopes_ref[0, h]
    inv_slope = slopes_ref[1, h]
    nkb = kaug_ref.shape[0] // tq
    lane = lax.broadcasted_iota(jnp.int32, (1, LANES), 1)

    def max_sq_norms(x):
        sq = x.astype(F32)
        sq = sq * sq
        first = jnp.sum(jnp.where(lane < HEAD_DIM, sq, 0.0), axis=1, keepdims=True)
        second = jnp.sum(jnp.where(lane >= HEAD_DIM, sq, 0.0), axis=1, keepdims=True)
        return jnp.max(first, axis=0, keepdims=True), jnp.max(second, axis=0, keepdims=True)

    @pl.when(i == 0)
    def _():
        def setup(c, carry):
            off = pl.multiple_of(c * tq, tq)
            n1, n2 = max_sq_norms(k_ref[pl.ds(off, tq), :])
            carry = (jnp.maximum(carry[0], n1), jnp.maximum(carry[1], n2))
            pos = off + lax.broadcasted_iota(jnp.int32, (tq, LANES), 0)
            lanes = lax.broadcasted_iota(jnp.int32, (tq, LANES), 1)
            coarse = (pos // POS_SPLIT) * POS_SPLIT
            posm = jnp.where(lanes == 0, coarse, jnp.where(lanes == 1, pos - coarse, 0))
            kaug_ref[pl.ds(off, tq), 0:LANES] = k_ref[pl.ds(off, tq), :]
            kaug_ref[pl.ds(off, tq), LANES:2 * LANES] = posm.astype(F32).astype(BF16)
            vt_ref[c, 0:LANES, :] = v_ref[pl.ds(off, tq), :].astype(F32).T.astype(BF16)
            ones_row = lax.broadcasted_iota(jnp.int32, (SUM_ROWS, tq), 0) == 0
            vt_ref[c, LANES:LANES + SUM_ROWS, :] = jnp.where(ones_row, 1.0, 0.0).astype(BF16)
            return carry

        zero = jnp.zeros((1, 1), F32)
        knorm_ref[0], knorm_ref[1] = lax.fori_loop(0, nkb, setup, (zero, zero))

    q1n, q2n = max_sq_norms(q_ref[...])
    qk = jnp.sqrt(jnp.maximum(q1n * knorm_ref[0], q2n * knorm_ref[1]))
    reach = (ALIBI_CUT + 2.0 * QK_SCALE * qk) * inv_slope
    keep = jnp.minimum(jnp.floor((reach - 1.0) * (1.0 / tq)) + 1.0, 1e6).astype(jnp.int32)
    n_off = jnp.clip(jnp.max(keep), 0, i)
    j0 = i - n_off

    qs = q_ref[...] * QK_SCALE
    bias_cols = jnp.broadcast_to(jnp.where(lane < 2, slope, 0.0).astype(BF16), (tq, LANES))
    for half in range(2):
        in_half = (lane >= HEAD_DIM) if half else (lane < HEAD_DIM)
        qa_ref[half] = jnp.concatenate([jnp.where(in_half, qs, jnp.zeros_like(qs)), bias_cols], axis=1)
    acc_ref[...] = jnp.zeros_like(acc_ref)
    m_ref[...] = jnp.full_like(m_ref, NEG)
    on_or_below_diagonal = (lax.broadcasted_iota(jnp.int32, (tq, tq), 0)
                            <= lax.broadcasted_iota(jnp.int32, (tq, tq), 1))

    def scores(jb, buf):
        st_ref, bmax_ref = buf
        off = pl.multiple_of(jb * tq, tq)
        kb = kaug_ref[pl.ds(off, tq), :]
        for half in range(2):
            st = lax.dot_general(kb, qa_ref[half], (((1,), (1,)), ((), ())),
                                 preferred_element_type=F32)
            st_ref[half] = st
            bmax_ref[half] = jnp.max(st, axis=0, keepdims=True)

    def softmax_pv(jb, buf, diagonal):
        st_ref, bmax_ref = buf
        vtb = vt_ref[jb]
        alphas, ps = [], []
        for half in range(2):
            st = st_ref[half]
            if diagonal:
                st = jnp.where(on_or_below_diagonal, st, NEG)
                block_max = jnp.max(st, axis=0, keepdims=True)
            else:
                block_max = bmax_ref[half]
            m_old = m_ref[half]
            m_new = jnp.maximum(m_old, block_max)
            alpha = jnp.exp(m_old - m_new)
            p = jnp.exp(st - m_new)
            m_ref[half] = m_new
            alphas.append(alpha)
            ps.append(p.astype(BF16))
        for half in range(2):
            acc_ref[half] = alphas[half] * acc_ref[half] + jnp.dot(vtb, ps[half], preferred_element_type=F32)

    buf_a, buf_b = (sta_ref, bmaxa_ref), (stb_ref, bmaxb_ref)
    scores(j0, buf_a)

    def pair(p, carry):
        scores(j0 + 2 * p + 1, buf_b)
        softmax_pv(j0 + 2 * p, buf_a, False)
        scores(j0 + 2 * p + 2, buf_a)
        softmax_pv(j0 + 2 * p + 1, buf_b, False)
        return carry

    lax.fori_loop(0, n_off // 2, pair, 0)

    @pl.when(n_off % 2 == 0)
    def _():
        softmax_pv(i, buf_a, True)

    @pl.when(n_off % 2 == 1)
    def _():
        scores(i, buf_b)
        softmax_pv(i - 1, buf_a, False)
        softmax_pv(i, buf_b, True)

    lam = (jnp.exp(jnp.sum(lq1_ref[...] * lk1_ref[...], axis=1, keepdims=True))
           - jnp.exp(jnp.sum(lq2_ref[...] * lk2_ref[...], axis=1, keepdims=True)) + lambda_init)
    o = (acc_ref[0, 0:LANES, :] * (1.0 / acc_ref[0, LANES:LANES + 1, :])
         - acc_ref[1, 0:LANES, :] * (lam / acc_ref[1, LANES:LANES + 1, :]))
    y = o * lax.rsqrt(jnp.mean(o * o, axis=0, keepdims=True) + RMS_EPS) * g_ref[...]
    o_ref[...] = (y * (1.0 - lambda_init)).T.astype(o_ref.dtype)


def _diff_attention(proj, slopes, lq1, lk1, lq2, lk2, subln_g, lambda_init):
    S = proj.shape[0]
    tq = min(512, S)
    vec = lambda n: pl.BlockSpec((1, n), lambda h, i: (0, 0))
    return pl.pallas_call(
        functools.partial(_diff_body, tq=tq, lambda_init=lambda_init),
        grid=(DIFF_HEADS, S // tq),
        in_specs=[pl.BlockSpec(memory_space=pltpu.SMEM),
                  pl.BlockSpec((tq, LANES), lambda h, i: (i, COL_QC // LANES + h)),
                  pl.BlockSpec((S, LANES), lambda h, i: (0, COL_KC // LANES + h)),
                  pl.BlockSpec((S, LANES), lambda h, i: (0, COL_VC // LANES + h)),
                  vec(HEAD_DIM), vec(HEAD_DIM), vec(HEAD_DIM), vec(HEAD_DIM),
                  pl.BlockSpec((2 * HEAD_DIM, 1), lambda h, i: (0, 0))],
        out_specs=pl.BlockSpec((tq, LANES), lambda h, i: (i, h)),
        out_shape=jax.ShapeDtypeStruct((S, BRANCH_WIDTH), BF16),
        scratch_shapes=[pltpu.VMEM((S, 2 * LANES), BF16),
                        pltpu.VMEM((S // tq, LANES + SUM_ROWS, tq), BF16),
                        pltpu.VMEM((2, 1, 1), F32),
                        pltpu.VMEM((2, tq, 2 * LANES), BF16),
                        pltpu.VMEM((2, tq, tq), F32), pltpu.VMEM((2, tq, tq), F32),
                        pltpu.VMEM((2, 1, tq), F32), pltpu.VMEM((2, 1, tq), F32),
                        pltpu.VMEM((2, 1, tq), F32),
                        pltpu.VMEM((2, LANES + SUM_ROWS, tq), F32)],
        compiler_params=_params(("arbitrary", "arbitrary")),
        name="diff_attention",
    )(slopes, proj, proj, proj, lq1.reshape(1, -1), lk1.reshape(1, -1), lq2.reshape(1, -1),
      lk2.reshape(1, -1), subln_g.reshape(-1, 1))


def _branch_body(oa_ref, ob_ref, oc_ref, w_ref, ga_ref, gb_ref, gc_ref, z_ref):
    z = None
    for n, (o_ref, g_ref) in enumerate(((oa_ref, ga_ref), (ob_ref, gb_ref), (oc_ref, gc_ref))):
        y = jnp.dot(o_ref[...], w_ref[0, n].astype(BF16), preferred_element_type=F32)
        t = g_ref[...].astype(F32) * y
        z = t if z is None else z + t
    z_ref[...] = z.astype(z_ref.dtype)


def _branch_merge(o_a, o_b, o_c, w_branch, layer, gates):
    S = o_a.shape[0]
    D = w_branch.shape[3]
    tm = min(1024, S)
    tn = 512
    nj = D // tn
    o_spec = pl.BlockSpec((tm, BRANCH_WIDTH), lambda i, j: (i, 0))
    gate_spec = lambda n: pl.BlockSpec((tm, tn), lambda i, j: (i, n * nj + j))
    return pl.pallas_call(
        _branch_body,
        grid=(S // tm, nj),
        in_specs=[o_spec, o_spec, o_spec,
                  pl.BlockSpec((1, N_BRANCHES, BRANCH_WIDTH, tn), lambda i, j: (layer, 0, 0, j)),
                  gate_spec(0), gate_spec(1), gate_spec(2)],
        out_specs=pl.BlockSpec((tm, tn), lambda i, j: (i, j)),
        out_shape=jax.ShapeDtypeStruct((S, D), BF16),
        compiler_params=_params(("arbitrary", "arbitrary")),
        name="branch_merge",
    )(o_a, o_b, o_c, w_branch, gates, gates, gates)


def _layer_norm(r, g, b):
    mu = jnp.mean(r, axis=1, keepdims=True)
    d = r - mu
    var = jnp.mean(d * d, axis=1, keepdims=True)
    return d * lax.rsqrt(var + LN_EPS) * g + b


def _split_bf16(a):
    hi = a.astype(BF16)
    return hi, (a - hi.astype(F32)).astype(BF16)


ROUTE_EXPERT, ROUTE_WEIGHT, ROUTE_RANK = 0, 2, 4


def _route_tile(logits, counts):
    rows = logits.shape[0]
    lane = lax.broadcasted_iota(jnp.int32, logits.shape, 1)
    row_min = lambda cond: jnp.min(jnp.where(cond, lane, LANES), axis=1, keepdims=True)
    row_max = lambda cond: jnp.max(jnp.where(cond, logits, NEG), axis=1, keepdims=True)
    is_group = lane < N_GROUPS
    g_max = row_max(is_group)
    g_top = row_min(is_group & (logits == g_max))
    p_group = 1.0 / jnp.sum(jnp.where(is_group, jnp.exp(logits - g_max), 0.0), axis=1, keepdims=True)
    e_lane = lane - N_GROUPS
    in_group = ((e_lane >= 0) & (e_lane < N_EXPERTS)
                & (jnp.right_shift(e_lane, EXPERTS_PER_GROUP.bit_length() - 1) == g_top))
    v1 = row_max(in_group)
    i1 = row_min(in_group & (logits == v1))
    rest = in_group & (lane != i1)
    v2 = row_max(rest)
    i2 = row_min(rest & (logits == v2))
    ratio = jnp.exp(v2 - v1)
    w1 = p_group / (1.0 + ratio)
    w2 = w1 * ratio
    member = (lane == i1) | (lane == i2)
    earlier = (lax.broadcasted_iota(jnp.int32, (rows, rows), 1)
               < lax.broadcasted_iota(jnp.int32, (rows, rows), 0)).astype(BF16)
    before = jnp.dot(earlier, member.astype(BF16), preferred_element_type=F32) + counts
    pick = lambda idx: jnp.sum(jnp.where(lane == idx, before, 0.0), axis=1, keepdims=True)
    record = jnp.zeros(logits.shape, F32)
    for at, value in ((ROUTE_EXPERT, (i1 - N_GROUPS).astype(F32)), (ROUTE_EXPERT + 1, (i2 - N_GROUPS).astype(F32)),
                      (ROUTE_WEIGHT, w1), (ROUTE_WEIGHT + 1, w2),
                      (ROUTE_RANK, pick(i1)), (ROUTE_RANK + 1, pick(i2))):
        record = jnp.where(lane == at, value, record)
    return record, counts + jnp.sum(member.astype(F32), axis=0, keepdims=True)


def _out_body(z_ref, w_ref, x_ref, g1_ref, lng_ref, lnb_ref, sc2_ref, sh2_ref, wr_hi_ref, wr_lo_ref, br_ref,
              x1_ref, route_ref, counts_ref, running_ref, *, alpha):
    @pl.when(pl.program_id(0) == 0)
    def _():
        running_ref[...] = jnp.zeros_like(running_ref)

    y = jnp.dot(z_ref[...], w_ref[...], preferred_element_type=F32)
    x1 = _layer_norm(alpha * x_ref[...] + g1_ref[...] * y, lng_ref[...], lnb_ref[...])
    x1_ref[...] = x1
    h_hi, h_lo = _split_bf16(x1 * (1.0 + sc2_ref[...]) + sh2_ref[...])
    w_hi = wr_hi_ref[...]
    logits = (jnp.dot(h_hi, w_hi, preferred_element_type=F32)
              + (jnp.dot(h_hi, wr_lo_ref[...], preferred_element_type=F32)
                 + jnp.dot(h_lo, w_hi, preferred_element_type=F32))) + br_ref[...]
    route_ref[...], counts = _route_tile(logits, running_ref[...])
    running_ref[...] = counts
    counts_ref[...] = jnp.broadcast_to(counts, counts_ref.shape)


def _mixer_out(z, w_out, x, g1, ln_g, ln_b, sc2, sh2, w_router, b_router, alpha):
    S, D = x.shape
    tm = min(256, S)
    row = pl.BlockSpec((tm, D), lambda i: (i, 0))
    vec = pl.BlockSpec((1, D), lambda i: (0, 0))
    router = pl.BlockSpec((D, LANES), lambda i: (0, 0))
    wr_hi, wr_lo = _split_bf16(w_router)
    return pl.pallas_call(
        functools.partial(_out_body, alpha=alpha),
        grid=(S // tm,),
        in_specs=[row, pl.BlockSpec((D, D), lambda i: (0, 0)), row, vec, vec, vec, vec, vec,
                  router, router, pl.BlockSpec((1, LANES), lambda i: (0, 0))],
        out_specs=[row, pl.BlockSpec((tm, LANES), lambda i: (i, 0)), pl.BlockSpec((8, LANES), lambda i: (0, 0))],
        out_shape=[jax.ShapeDtypeStruct((S, D), F32), jax.ShapeDtypeStruct((S, LANES), F32),
                   jax.ShapeDtypeStruct((8, LANES), F32)],
        scratch_shapes=[pltpu.VMEM((1, LANES), F32)],
        compiler_params=_params(("arbitrary",)),
        name="mixer_out_ln",
    )(z, w_out, x, g1, ln_g, ln_b, sc2, sh2, wr_hi, wr_lo, b_router)


def _slot(choice_ref, pstart_ref, at):
    code = choice_ref[at]
    return pstart_ref[code >> RANK_BITS] + (code & ((1 << RANK_BITS) - 1))


def _dispatch_body(choice_ref, pstart_ref, pend_ref, nu_ref, x_ref, xb_hbm, zero_ref, sem, zero_sem, *,
                   rows_per_step, nblk):
    i = pl.program_id(0)
    n_tokens = pl.num_programs(0) * rows_per_step
    DB = DISPATCH_BLOCK

    @pl.when(i == 0)
    def _():
        zero_ref[...] = jnp.zeros_like(zero_ref)

        def zero_copy(row):
            return pltpu.make_async_copy(zero_ref, xb_hbm.at[pl.ds(pl.multiple_of(row, DB), DB)], zero_sem)

        def has_rows(e):
            return pend_ref[e] > pstart_ref[e]

        def tail_start(b, carry):
            zero_copy(b * DB).start()
            return carry

        def tail_wait(b, carry):
            zero_copy(0).wait()
            return carry

        for e in range(N_EXPERTS):
            @pl.when(has_rows(e))
            def _():
                zero_copy(pend_ref[e] - DB).start()
        lax.fori_loop(nu_ref[0], nblk, tail_start, 0)
        for e in range(N_EXPERTS):
            @pl.when(has_rows(e))
            def _():
                zero_copy(0).wait()
        lax.fori_loop(nu_ref[0], nblk, tail_wait, 0)

    def row_copy(r, d):
        return pltpu.make_async_copy(x_ref.at[pl.ds(r, 1)], xb_hbm.at[pl.ds(d, 1)], sem)

    def issue(g, carry):
        for u in range(ROW_UNROLL):
            r = g * ROW_UNROLL + u
            t = i * rows_per_step + r
            for k in range(2):
                row_copy(r, _slot(choice_ref, pstart_ref, k * n_tokens + t)).start()
        return carry

    def wait(g, carry):
        for _ in range(2 * ROW_UNROLL):
            row_copy(0, 0).wait()
        return carry

    lax.fori_loop(0, rows_per_step // ROW_UNROLL, issue, 0)
    lax.fori_loop(0, rows_per_step // ROW_UNROLL, wait, 0)


def _dispatch(x1, slots, pend, n_used, padded_rows):
    T, D = x1.shape
    rows_per_step = min(512, T)
    nblk = padded_rows // DISPATCH_BLOCK
    return pl.pallas_call(
        functools.partial(_dispatch_body, rows_per_step=rows_per_step, nblk=nblk),
        grid_spec=pltpu.PrefetchScalarGridSpec(
            num_scalar_prefetch=4,
            grid=(T // rows_per_step,),
            in_specs=[pl.BlockSpec((rows_per_step, D), lambda i, *_: (i, 0))],
            out_specs=pl.BlockSpec(memory_space=pl.ANY),
            scratch_shapes=[pltpu.VMEM((DISPATCH_BLOCK, D), F32), pltpu.SemaphoreType.DMA(()),
                            pltpu.SemaphoreType.DMA(())]),
        out_shape=jax.ShapeDtypeStruct((padded_rows, D), F32),
        compiler_params=_params(("arbitrary",)),
        name="moe_dispatch",
    )(*slots, pend, n_used, x1)


def _expert_body(be_ref, nu_ref, first_ref, slot_ref, next_ref, xb_ref, sc_ref, sh_ref,
                 wg_hbm, wu_hbm, wd_hbm, yb_ref, wg_f, wu_f, wd_f, wg_s, wu_s, wd_s, sem, *, layer):
    b = pl.program_id(0)
    nu = nu_ref[0]
    base = layer * N_EXPERTS

    def weight_copies(e, slot):
        return (pltpu.make_async_copy(wg_hbm.at[base + e], wg_f.at[slot], sem.at[slot, 0]),
                pltpu.make_async_copy(wu_hbm.at[base + e], wu_f.at[slot], sem.at[slot, 1]),
                pltpu.make_async_copy(wd_hbm.at[base + e], wd_f.at[slot], sem.at[slot, 2]))

    @pl.when(b == 0)
    def _():
        for copy in weight_copies(be_ref[0], 0):
            copy.start()

    @pl.when((b < nu) & (first_ref[b] == 1))
    def _():
        e = be_ref[b]
        nxt = next_ref[b]
        for slot in range(2):
            @pl.when(slot_ref[b] == slot)
            def _():
                for copy in weight_copies(e, slot):
                    copy.wait()

                @pl.when(nxt >= 0)
                def _():
                    for copy in weight_copies(nxt, 1 - slot):
                        copy.start()

                wg_s[...] = wg_f[slot].astype(BF16)
                wu_s[...] = wu_f[slot].astype(BF16)
                wd_s[...] = wd_f[slot].astype(BF16)

    @pl.when(b < nu)
    def _():
        h = (xb_ref[...] * (1.0 + sc_ref[...]) + sh_ref[...]).astype(BF16)
        a = jnp.dot(h, wg_s[...], preferred_element_type=F32)
        u = jnp.dot(h, wu_s[...], preferred_element_type=F32)
        act = (a / (1.0 + jnp.exp(-a))) * u
        yb_ref[...] = jnp.dot(act.astype(BF16), wd_s[...], preferred_element_type=F32)

    @pl.when(b >= nu)
    def _():
        yb_ref[...] = jnp.zeros_like(yb_ref)


def _experts(xb, sc2, sh2, w_g, w_u, w_d, plan, layer):
    P, D = xb.shape
    DB = DISPATCH_BLOCK
    nblk = P // DB
    blk = lambda b, be, nu, *_: (jnp.minimum(b, nu[0] - 1), 0)
    vec = pl.BlockSpec((1, D), lambda b, *_: (0, 0))
    hbm = pl.BlockSpec(memory_space=pl.ANY)
    return pl.pallas_call(
        functools.partial(_expert_body, layer=layer),
        grid_spec=pltpu.PrefetchScalarGridSpec(
            num_scalar_prefetch=5,
            grid=(nblk,),
            in_specs=[pl.BlockSpec((DB, D), blk), vec, vec, hbm, hbm, hbm],
            out_specs=pl.BlockSpec((DB, D), lambda b, *_: (b, 0)),
            scratch_shapes=[pltpu.VMEM((2, D, D_EXPERT), F32), pltpu.VMEM((2, D, D_EXPERT), F32),
                            pltpu.VMEM((2, D_EXPERT, D), F32),
                            pltpu.VMEM((D, D_EXPERT), BF16), pltpu.VMEM((D, D_EXPERT), BF16),
                            pltpu.VMEM((D_EXPERT, D), BF16),
                            pltpu.SemaphoreType.DMA((2, 3))]),
        out_shape=jax.ShapeDtypeStruct((P, D), F32),
        compiler_params=_params(("arbitrary",)),
        name="moe_experts",
    )(*plan, xb, sc2, sh2, w_g, w_u, w_d)


def _combine_body(choice_ref, pstart_ref, x1_ref, wt_ref, g2_ref, lng_ref, lnb_ref, yb_hbm, o_ref,
                  buf0, buf1, sem, *, tm, alpha):
    i = pl.program_id(0)
    n_tokens = pl.num_programs(0) * tm
    bufs = (buf0, buf1)

    def row_copy(d, k, r):
        return pltpu.make_async_copy(yb_hbm.at[pl.ds(d, 1)], bufs[k].at[pl.ds(r, 1)], sem)

    def issue(g, carry):
        for u in range(ROW_UNROLL):
            r = g * ROW_UNROLL + u
            t = i * tm + r
            for k in range(2):
                row_copy(_slot(choice_ref, pstart_ref, k * n_tokens + t), k, r).start()
        return carry

    def wait(g, carry):
        for _ in range(ROW_UNROLL):
            for k in range(2):
                row_copy(0, k, 0).wait()
        return carry

    lax.fori_loop(0, tm // ROW_UNROLL, issue, 0)
    lax.fori_loop(0, tm // ROW_UNROLL, wait, 0)
    wt = wt_ref[...]
    y = buf0[...] * wt[:, 0:1] + buf1[...] * wt[:, 1:2]
    o_ref[...] = _layer_norm(alpha * x1_ref[...] + g2_ref[...] * y, lng_ref[...], lnb_ref[...])


def _combine_ln(x1, yb, slots, w_top, g2, ln_g, ln_b, alpha):
    T, D = x1.shape
    tm = min(256, T)
    row = pl.BlockSpec((tm, D), lambda i, *_: (i, 0))
    vec = pl.BlockSpec((1, D), lambda i, *_: (0, 0))
    return pl.pallas_call(
        functools.partial(_combine_body, tm=tm, alpha=alpha),
        grid_spec=pltpu.PrefetchScalarGridSpec(
            num_scalar_prefetch=2,
            grid=(T // tm,),
            in_specs=[row, pl.BlockSpec((tm, 2), lambda i, *_: (i, 0)), vec, vec, vec,
                      pl.BlockSpec(memory_space=pl.ANY)],
            out_specs=row,
            scratch_shapes=[pltpu.VMEM((tm, D), F32), pltpu.VMEM((tm, D), F32),
                            pltpu.SemaphoreType.DMA(())]),
        out_shape=jax.ShapeDtypeStruct((T, D), F32),
        compiler_params=_params(("arbitrary",)),
        name="moe_combine_ln",
    )(*slots, x1, w_top, g2, ln_g, ln_b, yb)


def _route(route, counts):
    T = route.shape[0]
    DB = DISPATCH_BLOCK
    w_top = route[:, ROUTE_WEIGHT:ROUTE_WEIGHT + 2]
    by_choice = route[:, :8].T
    expert_id = by_choice[ROUTE_EXPERT:ROUTE_EXPERT + 2].astype(jnp.int32)
    rank = by_choice[ROUTE_RANK:ROUTE_RANK + 2].astype(jnp.int32)
    counts = counts[0, N_GROUPS:N_GROUPS + N_EXPERTS].astype(jnp.int32)
    padded = (counts + DB - 1) // DB * DB
    pend = jnp.cumsum(padded)
    pstart = pend - padded
    choice = ((expert_id << RANK_BITS) | rank).astype(jnp.int32).reshape(-1)
    P = ((2 * T + DB - 1) // DB) * DB + N_EXPERTS * DB
    nblk = P // DB
    block_e = jnp.minimum(jnp.searchsorted(pend, jnp.arange(nblk, dtype=jnp.int32) * DB, side='right'),
                          N_EXPERTS - 1).astype(jnp.int32)
    n_used = (pend[-1:] // DB).astype(jnp.int32)
    b_idx = jnp.arange(nblk, dtype=jnp.int32)
    first = (b_idx < n_used) & ((b_idx == 0) | (block_e != jnp.roll(block_e, 1)))
    slot = (jnp.cumsum(first.astype(jnp.int32)) - 1) % 2
    after = (pend // DB).astype(jnp.int32)[block_e]
    next_e = jnp.where(after < n_used, block_e[jnp.minimum(after, nblk - 1)], -1)
    plan = (block_e, n_used, first.astype(jnp.int32), slot.astype(jnp.int32), next_e.astype(jnp.int32))
    return (choice, pstart.astype(jnp.int32)), pend.astype(jnp.int32), w_top, plan, P


def kernel(x, c, w_ada, b_ada, w_in, w_branch_gate, b_branch_gate, attn_sinks, lambda_q1, lambda_k1,
           lambda_q2, lambda_k2, subln_g, w_branch, w_out, ln1_g, ln1_b, w_router_group, b_router_group,
           w_router_expert, b_router_expert, w_exp_gate, w_exp_up, w_exp_down, ln2_g, ln2_b):
    B, S, D = x.shape
    assert B == 1 and D == D_MODEL
    depth = w_in.shape[0]
    alpha = ALPHA
    xs = x.reshape(S, D)
    mod = _ada_mod(c, w_ada, b_ada)
    diff_slopes = jnp.exp2(-8.0 * jnp.arange(1, DIFF_HEADS + 1, dtype=F32) / DIFF_HEADS)
    diff_slopes = jnp.stack([diff_slopes, 1.0 / diff_slopes])
    zero_bias = jnp.zeros((1, w_in.shape[2]), F32)
    pad = LANES - N_GROUPS - N_EXPERTS
    w_eg = w_exp_gate.reshape(depth * N_EXPERTS, D, D_EXPERT)
    w_eu = w_exp_up.reshape(depth * N_EXPERTS, D, D_EXPERT)
    w_ed = w_exp_down.reshape(depth * N_EXPERTS, D_EXPERT, D)
    for l in range(depth):
        lambda_init = 0.8 - 0.6 * math.exp(-0.3 * l)
        sh1, sc1, g1, sh2, sc2, g2 = [mod[l, :, n * D:(n + 1) * D] for n in range(6)]
        proj = _mod_matmul(xs, sc1, sh1, w_in, l, zero_bias, sigmoid=False, name="in_proj")
        gates = _mod_matmul(xs, sc1, sh1, w_branch_gate, l, b_branch_gate[l].reshape(1, -1),
                            sigmoid=True, name="branch_gates")
        o_a = _swa_attention(proj, attn_sinks[l])
        o_b = _sb_attention(proj)
        o_c = _diff_attention(proj, diff_slopes, lambda_q1[l], lambda_k1[l], lambda_q2[l], lambda_k2[l],
                              subln_g[l], lambda_init)
        z = _branch_merge(o_a, o_b, o_c, w_branch, l, gates)
        w_router = jnp.pad(jnp.concatenate([w_router_group[l], w_router_expert[l]], axis=1), ((0, 0), (0, pad)))
        b_router = jnp.pad(jnp.concatenate([b_router_group[l], b_router_expert[l]]), (0, pad)).reshape(1, LANES)
        x1, route, counts = _mixer_out(z, w_out[l].astype(BF16), xs, g1, ln1_g[l].reshape(1, D),
                                       ln1_b[l].reshape(1, D), sc2, sh2, w_router, b_router, alpha)
        slots, pend, w_top, plan, P = _route(route, counts)
        xb = _dispatch(x1, slots, pend, plan[1], P)
        yb = _experts(xb, sc2, sh2, w_eg, w_eu, w_ed, plan, l)
        xs = _combine_ln(x1, yb, slots, w_top, g2, ln2_g[l].reshape(1, D), ln2_b[l].reshape(1, D), alpha)
    return xs.reshape(B, S, D)
```

```python
import functools
import math

import jax
import jax.numpy as jnp
from jax import lax
from jax.experimental import pallas as pl
from jax.experimental.pallas import tpu as pltpu

F32 = jnp.float32
BF16 = jnp.bfloat16

D_MODEL = 2048
HEAD_DIM = 64
SWA_Q_HEADS = 16
SWA_KV_HEADS = 4
WINDOW = 128
SB_HEADS = 16
DIFF_HEADS = 8
BLOCK_Q = 128
BRANCH_WIDTH = 1024
A_KV = 256
N_BRANCHES = 3
N_GROUPS = 8
EXPERTS_PER_GROUP = 8
N_EXPERTS = 64
D_EXPERT = 384
DISPATCH_BLOCK = 128
DEPTH = 4
ALPHA = (2.0 * DEPTH) ** 0.25
LN_EPS = 1e-5
RMS_EPS = 1e-5
QK_SCALE = 1.0 / math.sqrt(HEAD_DIM)

COL_QA, COL_KA, COL_VA = 0, 1024, 1280
COL_QB, COL_KB, COL_VB = 1536, 2560, 3584
COL_QC, COL_KC, COL_VC = 4608, 5632, 6656

LANES = 128
VMEM_LIMIT = 56 * 1024 * 1024
NEG = -1e30
SB_EXIT = 88.0
ALIBI_CUT = 100.0
POS_SPLIT = 64
SUM_ROWS = 16
RANK_BITS = 16
ROW_UNROLL = 16


def _params(sem):
    return pltpu.CompilerParams(dimension_semantics=sem, vmem_limit_bytes=VMEM_LIMIT)


def _ada_body(c_ref, w_ref, b_ref, o_ref):
    o_ref[0] = jnp.sum(c_ref[...] * w_ref[0], axis=0, keepdims=True) + b_ref[0]


def _ada_mod(c, w_ada, b_ada):
    L, D, N = w_ada.shape
    tn = 512
    return pl.pallas_call(
        _ada_body,
        grid=(L, N // tn),
        in_specs=[pl.BlockSpec((D, 1), lambda l, j: (0, 0)),
                  pl.BlockSpec((1, D, tn), lambda l, j: (l, 0, j)),
                  pl.BlockSpec((1, 1, tn), lambda l, j: (l, 0, j))],
        out_specs=pl.BlockSpec((1, 1, tn), lambda l, j: (l, 0, j)),
        out_shape=jax.ShapeDtypeStruct((L, 1, N), F32),
        compiler_params=_params(("arbitrary", "arbitrary")),
        name="ada_mod",
    )(c.reshape(D, 1), w_ada, b_ada.reshape(L, 1, N))


def _modmm_body(x_ref, sc_ref, sh_ref, w_ref, b_ref, o_ref, h_ref, *, sigmoid):
    @pl.when(pl.program_id(1) == 0)
    def _():
        h_ref[...] = (x_ref[...] * (1.0 + sc_ref[...]) + sh_ref[...]).astype(BF16)

    acc = jnp.dot(h_ref[...], w_ref[0].astype(BF16), preferred_element_type=F32) + b_ref[...]
    if sigmoid:
        acc = 1.0 / (1.0 + jnp.exp(-acc))
    o_ref[...] = acc.astype(o_ref.dtype)


def _mod_matmul(x, sc, sh, w, layer, b, *, sigmoid, name):
    M, K = x.shape
    N = w.shape[2]
    tm = min(1024, M)
    tn = 768
    return pl.pallas_call(
        functools.partial(_modmm_body, sigmoid=sigmoid),
        grid=(M // tm, N // tn),
        in_specs=[pl.BlockSpec((tm, K), lambda i, j: (i, 0)),
                  pl.BlockSpec((1, K), lambda i, j: (0, 0)),
                  pl.BlockSpec((1, K), lambda i, j: (0, 0)),
                  pl.BlockSpec((1, K, tn), lambda i, j: (layer, 0, j)),
                  pl.BlockSpec((1, tn), lambda i, j: (0, j))],
        out_specs=pl.BlockSpec((tm, tn), lambda i, j: (i, j)),
        out_shape=jax.ShapeDtypeStruct((M, N), BF16),
        scratch_shapes=[pltpu.VMEM((tm, K), BF16)],
        compiler_params=_params(("arbitrary", "arbitrary")),
        name=name,
    )(x, sc, sh, w, b)


def _swa_body(sinks_ref, q_ref, kp_ref, kc_ref, vp_ref, vc_ref, o_ref):
    i = pl.program_id(0)
    k = jnp.concatenate([kp_ref[...], kc_ref[...]], axis=0)
    v = jnp.concatenate([vp_ref[...], vc_ref[...]], axis=0)
    qi = lax.broadcasted_iota(jnp.int32, (BLOCK_Q, 2 * BLOCK_Q), 0)
    ki = lax.broadcasted_iota(jnp.int32, (BLOCK_Q, 2 * BLOCK_Q), 1)
    dist = qi + BLOCK_Q - ki
    valid = (dist >= 0) & (dist < WINDOW) & ((ki >= BLOCK_Q) | (i > 0))
    distf = dist.astype(F32)
    group = SWA_Q_HEADS // SWA_KV_HEADS
    outs = []
    for h in range(SWA_Q_HEADS):
        g = h // group
        slope = 2.0 ** (-8.0 * (h + 1) / SWA_Q_HEADS)
        qh = q_ref[:, h * HEAD_DIM:(h + 1) * HEAD_DIM]
        kg = k[:, g * HEAD_DIM:(g + 1) * HEAD_DIM]
        vg = v[:, g * HEAD_DIM:(g + 1) * HEAD_DIM]
        s = lax.dot_general(qh, kg, (((1,), (1,)), ((), ())), preferred_element_type=F32) * QK_SCALE
        s = jnp.where(valid, s - slope * distf, NEG)
        sink = sinks_ref[h]
        m = jnp.maximum(jnp.max(s, axis=1, keepdims=True), sink)
        p = jnp.where(valid, jnp.exp(s - m), 0.0)
        denom = jnp.sum(p, axis=1, keepdims=True) + jnp.exp(sink - m)
        o = jnp.dot(p.astype(BF16), vg, preferred_element_type=F32) / denom
        outs.append(o)
    o_ref[...] = jnp.concatenate(outs, axis=1).astype(o_ref.dtype)


def _swa_attention(proj, sinks):
    S = proj.shape[0]
    nb = S // BLOCK_Q
    kv_blk = lambda col: pl.BlockSpec((BLOCK_Q, A_KV), lambda i: (i, col // A_KV))
    kv_prev = lambda col: pl.BlockSpec((BLOCK_Q, A_KV), lambda i: (jnp.maximum(i - 1, 0), col // A_KV))
    return pl.pallas_call(
        _swa_body,
        grid=(nb,),
        in_specs=[pl.BlockSpec(memory_space=pltpu.SMEM),
                  pl.BlockSpec((BLOCK_Q, BRANCH_WIDTH), lambda i: (i, COL_QA // BRANCH_WIDTH)),
                  kv_prev(COL_KA), kv_blk(COL_KA), kv_prev(COL_VA), kv_blk(COL_VA)],
        out_specs=pl.BlockSpec((BLOCK_Q, BRANCH_WIDTH), lambda i: (i, 0)),
        out_shape=jax.ShapeDtypeStruct((S, BRANCH_WIDTH), BF16),
        compiler_params=_params(("arbitrary",)),
        name="swa_attention",
    )(sinks, proj, proj, proj, proj, proj)


def _sb_body(q_ref, k_ref, v_ref, o_ref, vt_ref, later_ref, carry_ref, acc_ref, *, tq):
    g = pl.program_id(1)
    nkb = vt_ref.shape[0]
    lane = lax.broadcasted_iota(jnp.int32, (1, LANES), 1)

    @pl.when(g == 0)
    def _():
        def setup(c, carry):
            off = pl.multiple_of(c * tq, tq)
            vt_ref[c] = v_ref[pl.ds(off, tq), :].astype(F32).T.astype(BF16)
            return carry

        lax.fori_loop(0, nkb, setup, 0)
        this_key = lax.broadcasted_iota(jnp.int32, later_ref.shape, 0)
        other_key = lax.broadcasted_iota(jnp.int32, later_ref.shape, 1)
        later_ref[...] = (other_key > this_key).astype(BF16)

    qs = q_ref[...] * QK_SCALE
    acc_ref[...] = jnp.zeros_like(acc_ref)

    def stream(tile, half, first, carry):
        in_half = (lane >= HEAD_DIM) if half else (lane < HEAD_DIM)
        q_tile = qs[tile * tq:(tile + 1) * tq, :]
        return dict(qm=jnp.where(in_half, -q_tile, jnp.zeros_like(q_tile)), tile=tile, half=half,
                    first=first, carry=carry)

    def process(streams, nb, masked):
        n = nb * tq
        later = later_ref[...]
        if masked:
            k_row = lax.broadcasted_iota(jnp.int32, (n, tq), 0)
            q_col = lax.broadcasted_iota(jnp.int32, (n, tq), 1)
            before = k_row < q_col + (nb - 1) * tq
        zs = []
        for st in streams:
            off = pl.multiple_of(st["first"] * tq, tq)
            zs.append(lax.dot_general(k_ref[pl.ds(off, n), :], st["qm"], (((1,), (1,)), ((), ())),
                                      preferred_element_type=F32))
        if masked:
            zs = [jnp.where(before, neg_z, -NEG) for neg_z in zs]
        log_rems, splits = [], []
        for neg_z in zs:
            soft = jnp.log(1.0 + jnp.exp(-jnp.abs(neg_z)))
            log_rem = jnp.minimum(neg_z, 0.0) - soft
            hi = log_rem.astype(BF16)
            lo = (log_rem - hi.astype(F32)).astype(BF16)
            log_rems.append(log_rem)
            splits.append((hi, lo))
        afters, totals = [], []
        for (hi, lo), log_rem in zip(splits, log_rems):
            parts, total = [None] * nb, jnp.zeros((1, tq), F32)
            for b in reversed(range(nb)):
                blk = slice(b * tq, (b + 1) * tq)
                inside = (jnp.dot(later, hi[blk, :], preferred_element_type=F32)
                          + jnp.dot(later, lo[blk, :], preferred_element_type=F32))
                parts[b] = inside + total
                total = total + inside[0:1, :] + log_rem[b * tq:b * tq + 1, :]
            afters.append(parts[0] if nb == 1 else jnp.concatenate(parts, axis=0))
            totals.append(total)
        new_carries = []
        for st, neg_z, log_rem, after, total in zip(streams, zs, log_rems, afters, totals):
            w = jnp.exp((log_rem - neg_z) + (after + st["carry"])).astype(BF16)
            rows = slice(st["half"] * HEAD_DIM, (st["half"] + 1) * HEAD_DIM)
            for b in range(nb):
                acc_ref[st["tile"], st["half"]] += jnp.dot(vt_ref[st["first"] + b][rows, :],
                                                           w[b * tq:(b + 1) * tq, :], preferred_element_type=F32)
            new_carries.append(st["carry"] + total)
        return new_carries

    zero = jnp.zeros((1, tq), F32)
    tiles_heads = [(tile, half) for tile in range(2) for half in range(2)]

    def keep(carries, pairs):
        for (tile, half), carry in zip(pairs, carries):
            carry_ref[tile, half] = carry

    @pl.when(g == 0)
    def _():
        keep(process([stream(0, half, 0, zero) for half in range(2)], 1, True), tiles_heads[:2])
        keep(process([stream(1, half, 0, zero) for half in range(2)], 2, True), tiles_heads[2:])

    @pl.when(g > 0)
    def _():
        keep(process([stream(tile, half, 2 * g + tile - 1, zero) for tile, half in tiles_heads], 2, True),
             tiles_heads)

    def top(carries):
        return jnp.maximum(jnp.max(carries[0]), jnp.max(carries[1]))

    for tile in range(2):
        carries = (carry_ref[tile, 0], carry_ref[tile, 1])

        def cond(state):
            j, _, best = state
            return (j >= 0) & (best > -SB_EXIT)

        def body(state, tile=tile):
            j, carries, _ = state
            carries = tuple(process([stream(tile, half, j, carries[half]) for half in range(2)], 1, False))
            return j - 1, carries, top(carries)

        lax.while_loop(cond, body, (2 * g + tile - 2, carries, top(carries)))
        o_ref[tile * tq:(tile + 1) * tq, :] = jnp.concatenate(
            [acc_ref[tile, 0], acc_ref[tile, 1]], axis=0).T.astype(o_ref.dtype)


def _sb_attention(proj):
    S = proj.shape[0]
    tq = min(256, S // 2)
    npair = BRANCH_WIDTH // LANES
    return pl.pallas_call(
        functools.partial(_sb_body, tq=tq),
        grid=(npair, S // (2 * tq)),
        in_specs=[pl.BlockSpec((2 * tq, LANES), lambda p, g: (g, COL_QB // LANES + p)),
                  pl.BlockSpec((S, LANES), lambda p, g: (0, COL_KB // LANES + p)),
                  pl.BlockSpec((S, LANES), lambda p, g: (0, COL_VB // LANES + p))],
        out_specs=pl.BlockSpec((2 * tq, LANES), lambda p, g: (g, p)),
        out_shape=jax.ShapeDtypeStruct((S, BRANCH_WIDTH), BF16),
        scratch_shapes=[pltpu.VMEM((S // tq, LANES, tq), BF16),
                        pltpu.VMEM((tq, tq), BF16),
                        pltpu.VMEM((2, 2, 1, tq), F32),
                        pltpu.VMEM((2, 2, HEAD_DIM, tq), F32)],
        compiler_params=_params(("arbitrary", "arbitrary")),
        name="stickbreak_attention",
    )(proj, proj, proj)


def _diff_body(slopes_ref, q_ref, k_ref, v_ref, lq1_ref, lk1_ref, lq2_ref, lk2_ref, g_ref, o_ref,
               kaug_ref, vt_ref, knorm_ref, qa_ref, sta_ref, stb_ref, bmaxa_ref, bmaxb_ref, m_ref, acc_ref, *, tq,
               lambda_init):
    h = pl.program_id(0)
    i = pl.program_id(1)
    slope = slopes_ref[0, h]
    inv_slope = slopes_ref[1, h]
    nkb = kaug_ref.shape[0] // tq
    lane = lax.broadcasted_iota(jnp.int32, (1, LANES), 1)

    def max_sq_norms(x):
        sq = x.astype(F32)
        sq = sq * sq
        first = jnp.sum(jnp.where(lane < HEAD_DIM, sq, 0.0), axis=1, keepdims=True)
        second = jnp.sum(jnp.where(lane >= HEAD_DIM, sq, 0.0), axis=1, keepdims=True)
        return jnp.max(first, axis=0, keepdims=True), jnp.max(second, axis=0, keepdims=True)

    @pl.when(i == 0)
    def _():
        def setup(c, carry):
            off = pl.multiple_of(c * tq, tq)
            n1, n2 = max_sq_norms(k_ref[pl.ds(off, tq), :])
            carry = (jnp.maximum(carry[0], n1), jnp.maximum(carry[1], n2))
            pos = off + lax.broadcasted_iota(jnp.int32, (tq, LANES), 0)
            lanes = lax.broadcasted_iota(jnp.int32, (tq, LANES), 1)
            coarse = (pos // POS_SPLIT) * POS_SPLIT
            posm = jnp.where(lanes == 0, coarse, jnp.where(lanes == 1, pos - coarse, 0))
            kaug_ref[pl.ds(off, tq), 0:LANES] = k_ref[pl.ds(off, tq), :]
            kaug_ref[pl.ds(off, tq), LANES:2 * LANES] = posm.astype(F32).astype(BF16)
            vt_ref[c, 0:LANES, :] = v_ref[pl.ds(off, tq), :].astype(F32).T.astype(BF16)
            ones_row = lax.broadcasted_iota(jnp.int32, (SUM_ROWS, tq), 0) == 0
            vt_ref[c, LANES:LANES + SUM_ROWS, :] = jnp.where(ones_row, 1.0, 0.0).astype(BF16)
            return carry

        zero = jnp.zeros((1, 1), F32)
        knorm_ref[0], knorm_ref[1] = lax.fori_loop(0, nkb, setup, (zero, zero))

    q1n, q2n = max_sq_norms(q_ref[...])
    qk = jnp.sqrt(jnp.maximum(q1n * knorm_ref[0], q2n * knorm_ref[1]))
    reach = (ALIBI_CUT + 2.0 * QK_SCALE * qk) * inv_slope
    keep = jnp.minimum(jnp.floor((reach - 1.0) * (1.0 / tq)) + 1.0, 1e6).astype(jnp.int32)
    n_off = jnp.clip(jnp.max(keep), 0, i)
    j0 = i - n_off

    qs = q_ref[...] * QK_SCALE
    bias_cols = jnp.broadcast_to(jnp.where(lane < 2, slope, 0.0).astype(BF16), (tq, LANES))
    for half in range(2):
        in_half = (lane >= HEAD_DIM) if half else (lane < HEAD_DIM)
        qa_ref[half] = jnp.concatenate([jnp.where(in_half, qs, jnp.zeros_like(qs)), bias_cols], axis=1)
    acc_ref[...] = jnp.zeros_like(acc_ref)
    m_ref[...] = jnp.full_like(m_ref, NEG)
    on_or_below_diagonal = (lax.broadcasted_iota(jnp.int32, (tq, tq), 0)
                            <= lax.broadcasted_iota(jnp.int32, (tq, tq), 1))

    def scores(jb, buf):
        st_ref, bmax_ref = buf
        off = pl.multiple_of(jb * tq, tq)
        kb = kaug_ref[pl.ds(off, tq), :]
        for half in range(2):
            st = lax.dot_general(kb, qa_ref[half], (((1,), (1,)), ((), ())),
                                 preferred_element_type=F32)
            st_ref[half] = st
            bmax_ref[half] = jnp.max(st, axis=0, keepdims=True)

    def softmax_pv(jb, buf, diagonal):
        st_ref, bmax_ref = buf
        vtb = vt_ref[jb]
        alphas, ps = [], []
        for half in range(2):
            st = st_ref[half]
            if diagonal:
                st = jnp.where(on_or_below_diagonal, st, NEG)
                block_max = jnp.max(st, axis=0, keepdims=True)
            else:
                block_max = bmax_ref[half]
            m_old = m_ref[half]
            m_new = jnp.maximum(m_old, block_max)
            alpha = jnp.exp(m_old - m_new)
            p = jnp.exp(st - m_new)
            m_ref[half] = m_new
            alphas.append(alpha)
            ps.append(p.astype(BF16))
        for half in range(2):
            acc_ref[half] = alphas[half] * acc_ref[half] + jnp.dot(vtb, ps[half], preferred_element_type=F32)

    buf_a, buf_b = (sta_ref, bmaxa_ref), (stb_ref, bmaxb_ref)
    scores(j0, buf_a)

    def pair(p, carry):
        scores(j0 + 2 * p + 1, buf_b)
        softmax_pv(j0 + 2 * p, buf_a, False)
        scores(j0 + 2 * p + 2, buf_a)
        softmax_pv(j0 + 2 * p + 1, buf_b, False)
        return carry

    lax.fori_loop(0, n_off // 2, pair, 0)

    @pl.when(n_off % 2 == 0)
    def _():
        softmax_pv(i, buf_a, True)

    @pl.when(n_off % 2 == 1)
    def _():
        scores(i, buf_b)
        softmax_pv(i - 1, buf_a, False)
        softmax_pv(i, buf_b, True)

    lam = (jnp.exp(jnp.sum(lq1_ref[...] * lk1_ref[...], axis=1, keepdims=True))
           - jnp.exp(jnp.sum(lq2_ref[...] * lk2_ref[...], axis=1, keepdims=True)) + lambda_init)
    o = (acc_ref[0, 0:LANES, :] * (1.0 / acc_ref[0, LANES:LANES + 1, :])
         - acc_ref[1, 0:LANES, :] * (lam / acc_ref[1, LANES:LANES + 1, :]))
    y = o * lax.rsqrt(jnp.mean(o * o, axis=0, keepdims=True) + RMS_EPS) * g_ref[...]
    o_ref[...] = (y * (1.0 - lambda_init)).T.astype(o_ref.dtype)


def _diff_attention(proj, slopes, lq1, lk1, lq2, lk2, subln_g, lambda_init):
    S = proj.shape[0]
    tq = min(512, S)
    vec = lambda n: pl.BlockSpec((1, n), lambda h, i: (0, 0))
    return pl.pallas_call(
        functools.partial(_diff_body, tq=tq, lambda_init=lambda_init),
        grid=(DIFF_HEADS, S // tq),
        in_specs=[pl.BlockSpec(memory_space=pltpu.SMEM),
                  pl.BlockSpec((tq, LANES), lambda h, i: (i, COL_QC // LANES + h)),
                  pl.BlockSpec((S, LANES), lambda h, i: (0, COL_KC // LANES + h)),
                  pl.BlockSpec((S, LANES), lambda h, i: (0, COL_VC // LANES + h)),
                  vec(HEAD_DIM), vec(HEAD_DIM), vec(HEAD_DIM), vec(HEAD_DIM),
                  pl.BlockSpec((2 * HEAD_DIM, 1), lambda h, i: (0, 0))],
        out_specs=pl.BlockSpec((tq, LANES), lambda h, i: (i, h)),
        out_shape=jax.ShapeDtypeStruct((S, BRANCH_WIDTH), BF16),
        scratch_shapes=[pltpu.VMEM((S, 2 * LANES), BF16),
                        pltpu.VMEM((S // tq, LANES + SUM_ROWS, tq), BF16),
                        pltpu.VMEM((2, 1, 1), F32),
                        pltpu.VMEM((2, tq, 2 * LANES), BF16),
                        pltpu.VMEM((2, tq, tq), F32), pltpu.VMEM((2, tq, tq), F32),
                        pltpu.VMEM((2, 1, tq), F32), pltpu.VMEM((2, 1, tq), F32),
                        pltpu.VMEM((2, 1, tq), F32),
                        pltpu.VMEM((2, LANES + SUM_ROWS, tq), F32)],
        compiler_params=_params(("arbitrary", "arbitrary")),
        name="diff_attention",
    )(slopes, proj, proj, proj, lq1.reshape(1, -1), lk1.reshape(1, -1), lq2.reshape(1, -1),
      lk2.reshape(1, -1), subln_g.reshape(-1, 1))


def _branch_body(oa_ref, ob_ref, oc_ref, w_ref, ga_ref, gb_ref, gc_ref, z_ref):
    z = None
    for n, (o_ref, g_ref) in enumerate(((oa_ref, ga_ref), (ob_ref, gb_ref), (oc_ref, gc_ref))):
        y = jnp.dot(o_ref[...], w_ref[0, n].astype(BF16), preferred_element_type=F32)
        t = g_ref[...].astype(F32) * y
        z = t if z is None else z + t
    z_ref[...] = z.astype(z_ref.dtype)


def _branch_merge(o_a, o_b, o_c, w_branch, layer, gates):
    S = o_a.shape[0]
    D = w_branch.shape[3]
    tm = min(1024, S)
    tn = 512
    nj = D // tn
    o_spec = pl.BlockSpec((tm, BRANCH_WIDTH), lambda i, j: (i, 0))
    gate_spec = lambda n: pl.BlockSpec((tm, tn), lambda i, j: (i, n * nj + j))
    return pl.pallas_call(
        _branch_body,
        grid=(S // tm, nj),
        in_specs=[o_spec, o_spec, o_spec,
                  pl.BlockSpec((1, N_BRANCHES, BRANCH_WIDTH, tn), lambda i, j: (layer, 0, 0, j)),
                  gate_spec(0), gate_spec(1), gate_spec(2)],
        out_specs=pl.BlockSpec((tm, tn), lambda i, j: (i, j)),
        out_shape=jax.ShapeDtypeStruct((S, D), BF16),
        compiler_params=_params(("arbitrary", "arbitrary")),
        name="branch_merge",
    )(o_a, o_b, o_c, w_branch, gates, gates, gates)


def _layer_norm(r, g, b):
    mu = jnp.mean(r, axis=1, keepdims=True)
    d = r - mu
    var = jnp.mean(d * d, axis=1, keepdims=True)
    return d * lax.rsqrt(var + LN_EPS) * g + b


def _split_bf16(a):
    hi = a.astype(BF16)
    return hi, (a - hi.astype(F32)).astype(BF16)


ROUTE_EXPERT, ROUTE_WEIGHT, ROUTE_RANK = 0, 2, 4


def _route_tile(logits, counts):
    rows = logits.shape[0]
    lane = lax.broadcasted_iota(jnp.int32, logits.shape, 1)
    row_min = lambda cond: jnp.min(jnp.where(cond, lane, LANES), axis=1, keepdims=True)
    row_max = lambda cond: jnp.max(jnp.where(cond, logits, NEG), axis=1, keepdims=True)
    is_group = lane < N_GROUPS
    g_max = row_max(is_group)
    g_top = row_min(is_group & (logits == g_max))
    p_group = 1.0 / jnp.sum(jnp.where(is_group, jnp.exp(logits - g_max), 0.0), axis=1, keepdims=True)
    e_lane = lane - N_GROUPS
    in_group = ((e_lane >= 0) & (e_lane < N_EXPERTS)
                & (jnp.right_shift(e_lane, EXPERTS_PER_GROUP.bit_length() - 1) == g_top))
    v1 = row_max(in_group)
    i1 = row_min(in_group & (logits == v1))
    rest = in_group & (lane != i1)
    v2 = row_max(rest)
    i2 = row_min(rest & (logits == v2))
    ratio = jnp.exp(v2 - v1)
    w1 = p_group / (1.0 + ratio)
    w2 = w1 * ratio
    member = (lane == i1) | (lane == i2)
    earlier = (lax.broadcasted_iota(jnp.int32, (rows, rows), 1)
               < lax.broadcasted_iota(jnp.int32, (rows, rows), 0)).astype(BF16)
    before = jnp.dot(earlier, member.astype(BF16), preferred_element_type=F32) + counts
    pick = lambda idx: jnp.sum(jnp.where(lane == idx, before, 0.0), axis=1, keepdims=True)
    record = jnp.zeros(logits.shape, F32)
    for at, value in ((ROUTE_EXPERT, (i1 - N_GROUPS).astype(F32)), (ROUTE_EXPERT + 1, (i2 - N_GROUPS).astype(F32)),
                      (ROUTE_WEIGHT, w1), (ROUTE_WEIGHT + 1, w2),
                      (ROUTE_RANK, pick(i1)), (ROUTE_RANK + 1, pick(i2))):
        record = jnp.where(lane == at, value, record)
    return record, counts + jnp.sum(member.astype(F32), axis=0, keepdims=True)


def _out_body(z_ref, w_ref, x_ref, g1_ref, lng_ref, lnb_ref, sc2_ref, sh2_ref, wr_hi_ref, wr_lo_ref, br_ref,
              x1_ref, route_ref, counts_ref, running_ref, *, alpha):
    @pl.when(pl.program_id(0) == 0)
    def _():
        running_ref[...] = jnp.zeros_like(running_ref)

    y = jnp.dot(z_ref[...], w_ref[...], preferred_element_type=F32)
    x1 = _layer_norm(alpha * x_ref[...] + g1_ref[...] * y, lng_ref[...], lnb_ref[...])
    x1_ref[...] = x1
    h_hi, h_lo = _split_bf16(x1 * (1.0 + sc2_ref[...]) + sh2_ref[...])
    w_hi = wr_hi_ref[...]
    logits = (jnp.dot(h_hi, w_hi, preferred_element_type=F32)
              + (jnp.dot(h_hi, wr_lo_ref[...], preferred_element_type=F32)
                 + jnp.dot(h_lo, w_hi, preferred_element_type=F32))) + br_ref[...]
    route_ref[...], counts = _route_tile(logits, running_ref[...])
    running_ref[...] = counts
    counts_ref[...] = jnp.broadcast_to(counts, counts_ref.shape)


def _mixer_out(z, w_out, x, g1, ln_g, ln_b, sc2, sh2, w_router, b_router, alpha):
    S, D = x.shape
    tm = min(256, S)
    row = pl.BlockSpec((tm, D), lambda i: (i, 0))
    vec = pl.BlockSpec((1, D), lambda i: (0, 0))
    router = pl.BlockSpec((D, LANES), lambda i: (0, 0))
    wr_hi, wr_lo = _split_bf16(w_router)
    return pl.pallas_call(
        functools.partial(_out_body, alpha=alpha),
        grid=(S // tm,),
        in_specs=[row, pl.BlockSpec((D, D), lambda i: (0, 0)), row, vec, vec, vec, vec, vec,
                  router, router, pl.BlockSpec((1, LANES), lambda i: (0, 0))],
        out_specs=[row, pl.BlockSpec((tm, LANES), lambda i: (i, 0)), pl.BlockSpec((8, LANES), lambda i: (0, 0))],
        out_shape=[jax.ShapeDtypeStruct((S, D), F32), jax.ShapeDtypeStruct((S, LANES), F32),
                   jax.ShapeDtypeStruct((8, LANES), F32)],
        scratch_shapes=[pltpu.VMEM((1, LANES), F32)],
        compiler_params=_params(("arbitrary",)),
        name="mixer_out_ln",
    )(z, w_out, x, g1, ln_g, ln_b, sc2, sh2, wr_hi, wr_lo, b_router)


def _slot(choice_ref, pstart_ref, at):
    code = choice_ref[at]
    return pstart_ref[code >> RANK_BITS] + (code & ((1 << RANK_BITS) - 1))


def _dispatch_body(choice_ref, pstart_ref, pend_ref, nu_ref, x_ref, xb_hbm, zero_ref, sem, zero_sem, *,
                   rows_per_step, nblk):
    i = pl.program_id(0)
    n_tokens = pl.num_programs(0) * rows_per_step
    DB = DISPATCH_BLOCK

    @pl.when(i == 0)
    def _():
        zero_ref[...] = jnp.zeros_like(zero_ref)

        def zero_copy(row):
            return pltpu.make_async_copy(zero_ref, xb_hbm.at[pl.ds(pl.multiple_of(row, DB), DB)], zero_sem)

        def has_rows(e):
            return pend_ref[e] > pstart_ref[e]

        def tail_start(b, carry):
            zero_copy(b * DB).start()
            return carry

        def tail_wait(b, carry):
            zero_copy(0).wait()
            return carry

        for e in range(N_EXPERTS):
            @pl.when(has_rows(e))
            def _():
                zero_copy(pend_ref[e] - DB).start()
        lax.fori_loop(nu_ref[0], nblk, tail_start, 0)
        for e in range(N_EXPERTS):
            @pl.when(has_rows(e))
            def _():
                zero_copy(0).wait()
        lax.fori_loop(nu_ref[0], nblk, tail_wait, 0)

    def row_copy(r, d):
        return pltpu.make_async_copy(x_ref.at[pl.ds(r, 1)], xb_hbm.at[pl.ds(d, 1)], sem)

    def issue(g, carry):
        for u in range(ROW_UNROLL):
            r = g * ROW_UNROLL + u
            t = i * rows_per_step + r
            for k in range(2):
                row_copy(r, _slot(choice_ref, pstart_ref, k * n_tokens + t)).start()
        return carry

    def wait(g, carry):
        for _ in range(2 * ROW_UNROLL):
            row_copy(0, 0).wait()
        return carry

    lax.fori_loop(0, rows_per_step // ROW_UNROLL, issue, 0)
    lax.fori_loop(0, rows_per_step // ROW_UNROLL, wait, 0)


def _dispatch(x1, slots, pend, n_used, padded_rows):
    T, D = x1.shape
    rows_per_step = min(512, T)
    nblk = padded_rows // DISPATCH_BLOCK
    return pl.pallas_call(
        functools.partial(_dispatch_body, rows_per_step=rows_per_step, nblk=nblk),
        grid_spec=pltpu.PrefetchScalarGridSpec(
            num_scalar_prefetch=4,
            grid=(T // rows_per_step,),
            in_specs=[pl.BlockSpec((rows_per_step, D), lambda i, *_: (i, 0))],
            out_specs=pl.BlockSpec(memory_space=pl.ANY),
            scratch_shapes=[pltpu.VMEM((DISPATCH_BLOCK, D), F32), pltpu.SemaphoreType.DMA(()),
                            pltpu.SemaphoreType.DMA(())]),
        out_shape=jax.ShapeDtypeStruct((padded_rows, D), F32),
        compiler_params=_params(("arbitrary",)),
        name="moe_dispatch",
    )(*slots, pend, n_used, x1)


def _expert_body(be_ref, nu_ref, first_ref, slot_ref, next_ref, xb_ref, sc_ref, sh_ref,
                 wg_hbm, wu_hbm, wd_hbm, yb_ref, wg_f, wu_f, wd_f, wg_s, wu_s, wd_s, sem, *, layer):
    b = pl.program_id(0)
    nu = nu_ref[0]
    base = layer * N_EXPERTS

    def weight_copies(e, slot):
        return (pltpu.make_async_copy(wg_hbm.at[base + e], wg_f.at[slot], sem.at[slot, 0]),
                pltpu.make_async_copy(wu_hbm.at[base + e], wu_f.at[slot], sem.at[slot, 1]),
                pltpu.make_async_copy(wd_hbm.at[base + e], wd_f.at[slot], sem.at[slot, 2]))

    @pl.when(b == 0)
    def _():
        for copy in weight_copies(be_ref[0], 0):
            copy.start()

    @pl.when((b < nu) & (first_ref[b] == 1))
    def _():
        e = be_ref[b]
        nxt = next_ref[b]
        for slot in range(2):
            @pl.when(slot_ref[b] == slot)
            def _():
                for copy in weight_copies(e, slot):
                    copy.wait()

                @pl.when(nxt >= 0)
                def _():
                    for copy in weight_copies(nxt, 1 - slot):
                        copy.start()

                wg_s[...] = wg_f[slot].astype(BF16)
                wu_s[...] = wu_f[slot].astype(BF16)
                wd_s[...] = wd_f[slot].astype(BF16)

    @pl.when(b < nu)
    def _():
        h = (xb_ref[...] * (1.0 + sc_ref[...]) + sh_ref[...]).astype(BF16)
        a = jnp.dot(h, wg_s[...], preferred_element_type=F32)
        u = jnp.dot(h, wu_s[...], preferred_element_type=F32)
        act = (a / (1.0 + jnp.exp(-a))) * u
        yb_ref[...] = jnp.dot(act.astype(BF16), wd_s[...], preferred_element_type=F32)

    @pl.when(b >= nu)
    def _():
        yb_ref[...] = jnp.zeros_like(yb_ref)


def _experts(xb, sc2, sh2, w_g, w_u, w_d, plan, layer):
    P, D = xb.shape
    DB = DISPATCH_BLOCK
    nblk = P // DB
    blk = lambda b, be, nu, *_: (jnp.minimum(b, nu[0] - 1), 0)
    vec = pl.BlockSpec((1, D), lambda b, *_: (0, 0))
    hbm = pl.BlockSpec(memory_space=pl.ANY)
    return pl.pallas_call(
        functools.partial(_expert_body, layer=layer),
        grid_spec=pltpu.PrefetchScalarGridSpec(
            num_scalar_prefetch=5,
            grid=(nblk,),
            in_specs=[pl.BlockSpec((DB, D), blk), vec, vec, hbm, hbm, hbm],
            out_specs=pl.BlockSpec((DB, D), lambda b, *_: (b, 0)),
            scratch_shapes=[pltpu.VMEM((2, D, D_EXPERT), F32), pltpu.VMEM((2, D, D_EXPERT), F32),
                            pltpu.VMEM((2, D_EXPERT, D), F32),
                            pltpu.VMEM((D, D_EXPERT), BF16), pltpu.VMEM((D, D_EXPERT), BF16),
                            pltpu.VMEM((D_EXPERT, D), BF16),
                            pltpu.SemaphoreType.DMA((2, 3))]),
        out_shape=jax.ShapeDtypeStruct((P, D), F32),
        compiler_params=_params(("arbitrary",)),
        name="moe_experts",
    )(*plan, xb, sc2, sh2, w_g, w_u, w_d)


def _combine_body(choice_ref, pstart_ref, x1_ref, wt_ref, g2_ref, lng_ref, lnb_ref, yb_hbm, o_ref,
                  buf0, buf1, sem, *, tm, alpha):
    i = pl.program_id(0)
    n_tokens = pl.num_programs(0) * tm
    bufs = (buf0, buf1)

    def row_copy(d, k, r):
        return pltpu.make_async_copy(yb_hbm.at[pl.ds(d, 1)], bufs[k].at[pl.ds(r, 1)], sem)

    def issue(g, carry):
        for u in range(ROW_UNROLL):
            r = g * ROW_UNROLL + u
            t = i * tm + r
            for k in range(2):
                row_copy(_slot(choice_ref, pstart_ref, k * n_tokens + t), k, r).start()
        return carry

    def wait(g, carry):
        for _ in range(ROW_UNROLL):
            for k in range(2):
                row_copy(0, k, 0).wait()
        return carry

    lax.fori_loop(0, tm // ROW_UNROLL, issue, 0)
    lax.fori_loop(0, tm // ROW_UNROLL, wait, 0)
    wt = wt_ref[...]
    y = buf0[...] * wt[:, 0:1] + buf1[...] * wt[:, 1:2]
    o_ref[...] = _layer_norm(alpha * x1_ref[...] + g2_ref[...] * y, lng_ref[...], lnb_ref[...])


def _combine_ln(x1, yb, slots, w_top, g2, ln_g, ln_b, alpha):
    T, D = x1.shape
    tm = min(256, T)
    row = pl.BlockSpec((tm, D), lambda i, *_: (i, 0))
    vec = pl.BlockSpec((1, D), lambda i, *_: (0, 0))
    return pl.pallas_call(
        functools.partial(_combine_body, tm=tm, alpha=alpha),
        grid_spec=pltpu.PrefetchScalarGridSpec(
            num_scalar_prefetch=2,
            grid=(T // tm,),
            in_specs=[row, pl.BlockSpec((tm, 2), lambda i, *_: (i, 0)), vec, vec, vec,
                      pl.BlockSpec(memory_space=pl.ANY)],
            out_specs=row,
            scratch_shapes=[pltpu.VMEM((tm, D), F32), pltpu.VMEM((tm, D), F32),
                            pltpu.SemaphoreType.DMA(())]),
        out_shape=jax.ShapeDtypeStruct((T, D), F32),
        compiler_params=_params(("arbitrary",)),
        name="moe_combine_ln",
    )(*slots, x1, w_top, g2, ln_g, ln_b, yb)


def _route(route, counts):
    T = route.shape[0]
    DB = DISPATCH_BLOCK
    w_top = route[:, ROUTE_WEIGHT:ROUTE_WEIGHT + 2]
    by_choice = route[:, :8].T
    expert_id = by_choice[ROUTE_EXPERT:ROUTE_EXPERT + 2].astype(jnp.int32)
    rank = by_choice[ROUTE_RANK:ROUTE_RANK + 2].astype(jnp.int32)
    counts = counts[0, N_GROUPS:N_GROUPS + N_EXPERTS].astype(jnp.int32)
    padded = (counts + DB - 1) // DB * DB
    pend = jnp.cumsum(padded)
    pstart = pend - padded
    choice = ((expert_id << RANK_BITS) | rank).astype(jnp.int32).reshape(-1)
    P = ((2 * T + DB - 1) // DB) * DB + N_EXPERTS * DB
    nblk = P // DB
    block_e = jnp.minimum(jnp.searchsorted(pend, jnp.arange(nblk, dtype=jnp.int32) * DB, side='right'),
                          N_EXPERTS - 1).astype(jnp.int32)
    n_used = (pend[-1:] // DB).astype(jnp.int32)
    b_idx = jnp.arange(nblk, dtype=jnp.int32)
    first = (b_idx < n_used) & ((b_idx == 0) | (block_e != jnp.roll(block_e, 1)))
    slot = (jnp.cumsum(first.astype(jnp.int32)) - 1) % 2
    after = (pend // DB).astype(jnp.int32)[block_e]
    next_e = jnp.where(after < n_used, block_e[jnp.minimum(after, nblk - 1)], -1)
    plan = (block_e, n_used, first.astype(jnp.int32), slot.astype(jnp.int32), next_e.astype(jnp.int32))
    return (choice, pstart.astype(jnp.int32)), pend.astype(jnp.int32), w_top, plan, P


def kernel(x, c, w_ada, b_ada, w_in, w_branch_gate, b_branch_gate, attn_sinks, lambda_q1, lambda_k1,
           lambda_q2, lambda_k2, subln_g, w_branch, w_out, ln1_g, ln1_b, w_router_group, b_router_group,
           w_router_expert, b_router_expert, w_exp_gate, w_exp_up, w_exp_down, ln2_g, ln2_b):
    B, S, D = x.shape
    assert B == 1 and D == D_MODEL
    depth = w_in.shape[0]
    alpha = ALPHA
    xs = x.reshape(S, D)
    mod = _ada_mod(c, w_ada, b_ada)
    diff_slopes = jnp.exp2(-8.0 * jnp.arange(1, DIFF_HEADS + 1, dtype=F32) / DIFF_HEADS)
    diff_slopes = jnp.stack([diff_slopes, 1.0 / diff_slopes])
    zero_bias = jnp.zeros((1, w_in.shape[2]), F32)
    pad = LANES - N_GROUPS - N_EXPERTS
    w_eg = w_exp_gate.reshape(depth * N_EXPERTS, D, D_EXPERT)
    w_eu = w_exp_up.reshape(depth * N_EXPERTS, D, D_EXPERT)
    w_ed = w_exp_down.reshape(depth * N_EXPERTS, D_EXPERT, D)
    for l in range(depth):
        lambda_init = 0.8 - 0.6 * math.exp(-0.3 * l)
        sh1, sc1, g1, sh2, sc2, g2 = [mod[l, :, n * D:(n + 1) * D] for n in range(6)]
        proj = _mod_matmul(xs, sc1, sh1, w_in, l, zero_bias, sigmoid=False, name="in_proj")
        gates = _mod_matmul(xs, sc1, sh1, w_branch_gate, l, b_branch_gate[l].reshape(1, -1),
                            sigmoid=True, name="branch_gates")
        o_a = _swa_attention(proj, attn_sinks[l])
        o_b = _sb_attention(proj)
        o_c = _diff_attention(proj, diff_slopes, lambda_q1[l], lambda_k1[l], lambda_q2[l], lambda_k2[l],
                              subln_g[l], lambda_init)
        z = _branch_merge(o_a, o_b, o_c, w_branch, l, gates)
        w_router = jnp.pad(jnp.concatenate([w_router_group[l], w_router_expert[l]], axis=1), ((0, 0), (0, pad)))
        b_router = jnp.pad(jnp.concatenate([b_router_group[l], b_router_expert[l]]), (0, pad)).reshape(1, LANES)
        x1, route, counts = _mixer_out(z, w_out[l].astype(BF16), xs, g1, ln1_g[l].reshape(1, D),
                                       ln1_b[l].reshape(1, D), sc2, sh2, w_router, b_router, alpha)
        slots, pend, w_top, plan, P = _route(route, counts)
        xb = _dispatch(x1, slots, pend, plan[1], P)
        yb = _experts(xb, sc2, sh2, w_eg, w_eu, w_ed, plan, l)
        xs = _combine_ln(x1, yb, slots, w_top, g2, ln2_g[l].reshape(1, D), ln2_b[l].reshape(1, D), alpha)
    return xs.reshape(B, S, D)
```

```python
import functools
import math

import jax
import jax.numpy as jnp
from jax import lax
from jax.experimental import pallas as pl
from jax.experimental.pallas import tpu as pltpu

F32 = jnp.float32
BF16 = jnp.bfloat16

D_MODEL = 2048
HEAD_DIM = 64
SWA_Q_HEADS = 16
SWA_KV_HEADS = 4
WINDOW = 128
SB_HEADS = 16
DIFF_HEADS = 8
BLOCK_Q = 128
BRANCH_WIDTH = 1024
A_KV = 256
N_BRANCHES = 3
N_GROUPS = 8
EXPERTS_PER_GROUP = 8
N_EXPERTS = 64
D_EXPERT = 384
DISPATCH_BLOCK = 128
DEPTH = 4
ALPHA = (2.0 * DEPTH) ** 0.25
LN_EPS = 1e-5
RMS_EPS = 1e-5
QK_SCALE = 1.0 / math.sqrt(HEAD_DIM)

COL_QA, COL_KA, COL_VA = 0, 1024, 1280
COL_QB, COL_KB, COL_VB = 1536, 2560, 3584
COL_QC, COL_KC, COL_VC = 4608, 5632, 6656

LANES = 128
VMEM_LIMIT = 56 * 1024 * 1024
NEG = -1e30
SB_EXIT = 88.0
ALIBI_CUT = 100.0
POS_SPLIT = 64
SUM_ROWS = 16
RANK_BITS = 16
ROW_UNROLL = 16


def _params(sem):
    return pltpu.CompilerParams(dimension_semantics=sem, vmem_limit_bytes=VMEM_LIMIT)


def _ada_body(c_ref, w_ref, b_ref, o_ref):
    o_ref[0] = jnp.sum(c_ref[...] * w_ref[0], axis=0, keepdims=True) + b_ref[0]


def _ada_mod(c, w_ada, b_ada):
    L, D, N = w_ada.shape
    tn = 512
    return pl.pallas_call(
        _ada_body,
        grid=(L, N // tn),
        in_specs=[pl.BlockSpec((D, 1), lambda l, j: (0, 0)),
                  pl.BlockSpec((1, D, tn), lambda l, j: (l, 0, j)),
                  pl.BlockSpec((1, 1, tn), lambda l, j: (l, 0, j))],
        out_specs=pl.BlockSpec((1, 1, tn), lambda l, j: (l, 0, j)),
        out_shape=jax.ShapeDtypeStruct((L, 1, N), F32),
        compiler_params=_params(("arbitrary", "arbitrary")),
        name="ada_mod",
    )(c.reshape(D, 1), w_ada, b_ada.reshape(L, 1, N))


def _modmm_body(x_ref, sc_ref, sh_ref, w_ref, b_ref, o_ref, h_ref, *, sigmoid):
    @pl.when(pl.program_id(1) == 0)
    def _():
        h_ref[...] = (x_ref[...] * (1.0 + sc_ref[...]) + sh_ref[...]).astype(BF16)

    acc = jnp.dot(h_ref[...], w_ref[0].astype(BF16), preferred_element_type=F32) + b_ref[...]
    if sigmoid:
        acc = 1.0 / (1.0 + jnp.exp(-acc))
    o_ref[...] = acc.astype(o_ref.dtype)


def _mod_matmul(x, sc, sh, w, layer, b, *, sigmoid, name):
    M, K = x.shape
    N = w.shape[2]
    tm = min(1024, M)
    tn = 768
    return pl.pallas_call(
        functools.partial(_modmm_body, sigmoid=sigmoid),
        grid=(M // tm, N // tn),
        in_specs=[pl.BlockSpec((tm, K), lambda i, j: (i, 0)),
                  pl.BlockSpec((1, K), lambda i, j: (0, 0)),
                  pl.BlockSpec((1, K), lambda i, j: (0, 0)),
                  pl.BlockSpec((1, K, tn), lambda i, j: (layer, 0, j)),
                  pl.BlockSpec((1, tn), lambda i, j: (0, j))],
        out_specs=pl.BlockSpec((tm, tn), lambda i, j: (i, j)),
        out_shape=jax.ShapeDtypeStruct((M, N), BF16),
        scratch_shapes=[pltpu.VMEM((tm, K), BF16)],
        compiler_params=_params(("arbitrary", "arbitrary")),
        name=name,
    )(x, sc, sh, w, b)


def _swa_body(sinks_ref, q_ref, kp_ref, kc_ref, vp_ref, vc_ref, o_ref):
    i = pl.program_id(0)
    k = jnp.concatenate([kp_ref[...], kc_ref[...]], axis=0)
    v = jnp.concatenate([vp_ref[...], vc_ref[...]], axis=0)
    qi = lax.broadcasted_iota(jnp.int32, (BLOCK_Q, 2 * BLOCK_Q), 0)
    ki = lax.broadcasted_iota(jnp.int32, (BLOCK_Q, 2 * BLOCK_Q), 1)
    dist = qi + BLOCK_Q - ki
    valid = (dist >= 0) & (dist < WINDOW) & ((ki >= BLOCK_Q) | (i > 0))
    distf = dist.astype(F32)
    group = SWA_Q_HEADS // SWA_KV_HEADS
    outs = []
    for h in range(SWA_Q_HEADS):
        g = h // group
        slope = 2.0 ** (-8.0 * (h + 1) / SWA_Q_HEADS)
        qh = q_ref[:, h * HEAD_DIM:(h + 1) * HEAD_DIM]
        kg = k[:, g * HEAD_DIM:(g + 1) * HEAD_DIM]
        vg = v[:, g * HEAD_DIM:(g + 1) * HEAD_DIM]
        s = lax.dot_general(qh, kg, (((1,), (1,)), ((), ())), preferred_element_type=F32) * QK_SCALE
        s = jnp.where(valid, s - slope * distf, NEG)
        sink = sinks_ref[h]
        m = jnp.maximum(jnp.max(s, axis=1, keepdims=True), sink)
        p = jnp.where(valid, jnp.exp(s - m), 0.0)
        denom = jnp.sum(p, axis=1, keepdims=True) + jnp.exp(sink - m)
        o = jnp.dot(p.astype(BF16), vg, preferred_element_type=F32) / denom
        outs.append(o)
    o_ref[...] = jnp.concatenate(outs, axis=1).astype(o_ref.dtype)


def _swa_attention(proj, sinks):
    S = proj.shape[0]
    nb = S // BLOCK_Q
    kv_blk = lambda col: pl.BlockSpec((BLOCK_Q, A_KV), lambda i: (i, col // A_KV))
    kv_prev = lambda col: pl.BlockSpec((BLOCK_Q, A_KV), lambda i: (jnp.maximum(i - 1, 0), col // A_KV))
    return pl.pallas_call(
        _swa_body,
        grid=(nb,),
        in_specs=[pl.BlockSpec(memory_space=pltpu.SMEM),
                  pl.BlockSpec((BLOCK_Q, BRANCH_WIDTH), lambda i: (i, COL_QA // BRANCH_WIDTH)),
                  kv_prev(COL_KA), kv_blk(COL_KA), kv_prev(COL_VA), kv_blk(COL_VA)],
        out_specs=pl.BlockSpec((BLOCK_Q, BRANCH_WIDTH), lambda i: (i, 0)),
        out_shape=jax.ShapeDtypeStruct((S, BRANCH_WIDTH), BF16),
        compiler_params=_params(("arbitrary",)),
        name="swa_attention",
    )(sinks, proj, proj, proj, proj, proj)


def _sb_body(q_ref, k_ref, v_ref, o_ref, vt_ref, later_ref, carry_ref, acc_ref, *, tq):
    g = pl.program_id(1)
    nkb = vt_ref.shape[0]
    lane = lax.broadcasted_iota(jnp.int32, (1, LANES), 1)

    @pl.when(g == 0)
    def _():
        def setup(c, carry):
            off = pl.multiple_of(c * tq, tq)
            vt_ref[c] = v_ref[pl.ds(off, tq), :].astype(F32).T.astype(BF16)
            return carry

        lax.fori_loop(0, nkb, setup, 0)
        this_key = lax.broadcasted_iota(jnp.int32, later_ref.shape, 0)
        other_key = lax.broadcasted_iota(jnp.int32, later_ref.shape, 1)
        later_ref[...] = (other_key > this_key).astype(BF16)

    qs = q_ref[...] * QK_SCALE
    acc_ref[...] = jnp.zeros_like(acc_ref)

    def stream(tile, half, first, carry):
        in_half = (lane >= HEAD_DIM) if half else (lane < HEAD_DIM)
        q_tile = qs[tile * tq:(tile + 1) * tq, :]
        return dict(qm=jnp.where(in_half, -q_tile, jnp.zeros_like(q_tile)), tile=tile, half=half,
                    first=first, carry=carry)

    def process(streams, nb, masked):
        n = nb * tq
        later = later_ref[...]
        if masked:
            k_row = lax.broadcasted_iota(jnp.int32, (n, tq), 0)
            q_col = lax.broadcasted_iota(jnp.int32, (n, tq), 1)
            before = k_row < q_col + (nb - 1) * tq
        zs = []
        for st in streams:
            off = pl.multiple_of(st["first"] * tq, tq)
            zs.append(lax.dot_general(k_ref[pl.ds(off, n), :], st["qm"], (((1,), (1,)), ((), ())),
                                      preferred_element_type=F32))
        if masked:
            zs = [jnp.where(before, neg_z, -NEG) for neg_z in zs]
        log_rems, splits = [], []
        for neg_z in zs:
            soft = jnp.log(1.0 + jnp.exp(-jnp.abs(neg_z)))
            log_rem = jnp.minimum(neg_z, 0.0) - soft
            hi = log_rem.astype(BF16)
            lo = (log_rem - hi.astype(F32)).astype(BF16)
            log_rems.append(log_rem)
            splits.append((hi, lo))
        afters, totals = [], []
        for (hi, lo), log_rem in zip(splits, log_rems):
            parts, total = [None] * nb, jnp.zeros((1, tq), F32)
            for b in reversed(range(nb)):
                blk = slice(b * tq, (b + 1) * tq)
                inside = (jnp.dot(later, hi[blk, :], preferred_element_type=F32)
                          + jnp.dot(later, lo[blk, :], preferred_element_type=F32))
                parts[b] = inside + total
                total = total + inside[0:1, :] + log_rem[b * tq:b * tq + 1, :]
            afters.append(parts[0] if nb == 1 else jnp.concatenate(parts, axis=0))
            totals.append(total)
        new_carries = []
        for st, neg_z, log_rem, after, total in zip(streams, zs, log_rems, afters, totals):
            w = jnp.exp((log_rem - neg_z) + (after + st["carry"])).astype(BF16)
            rows = slice(st["half"] * HEAD_DIM, (st["half"] + 1) * HEAD_DIM)
            for b in range(nb):
                acc_ref[st["tile"], st["half"]] += jnp.dot(vt_ref[st["first"] + b][rows, :],
                                                           w[b * tq:(b + 1) * tq, :], preferred_element_type=F32)
            new_carries.append(st["carry"] + total)
        return new_carries

    zero = jnp.zeros((1, tq), F32)
    tiles_heads = [(tile, half) for tile in range(2) for half in range(2)]

    def keep(carries, pairs):
        for (tile, half), carry in zip(pairs, carries):
            carry_ref[tile, half] = carry

    @pl.when(g == 0)
    def _():
        keep(process([stream(0, half, 0, zero) for half in range(2)], 1, True), tiles_heads[:2])
        keep(process([stream(1, half, 0, zero) for half in range(2)], 2, True), tiles_heads[2:])

    @pl.when(g > 0)
    def _():
        keep(process([stream(tile, half, 2 * g + tile - 1, zero) for tile, half in tiles_heads], 2, True),
             tiles_heads)

    def top(carries):
        return jnp.maximum(jnp.max(carries[0]), jnp.max(carries[1]))

    for tile in range(2):
        carries = (carry_ref[tile, 0], carry_ref[tile, 1])

        def cond(state):
            j, _, best = state
            return (j >= 0) & (best > -SB_EXIT)

        def body(state, tile=tile):
            j, carries, _ = state
            carries = tuple(process([stream(tile, half, j, carries[half]) for half in range(2)], 1, False))
            return j - 1, carries, top(carries)

        lax.while_loop(cond, body, (2 * g + tile - 2, carries, top(carries)))
        o_ref[tile * tq:(tile + 1) * tq, :] = jnp.concatenate(
            [acc_ref[tile, 0], acc_ref[tile, 1]], axis=0).T.astype(o_ref.dtype)


def _sb_attention(proj):
    S = proj.shape[0]
    tq = min(256, S // 2)
    npair = BRANCH_WIDTH // LANES
    return pl.pallas_call(
        functools.partial(_sb_body, tq=tq),
        grid=(npair, S // (2 * tq)),
        in_specs=[pl.BlockSpec((2 * tq, LANES), lambda p, g: (g, COL_QB // LANES + p)),
                  pl.BlockSpec((S, LANES), lambda p, g: (0, COL_KB // LANES + p)),
                  pl.BlockSpec((S, LANES), lambda p, g: (0, COL_VB // LANES + p))],
        out_specs=pl.BlockSpec((2 * tq, LANES), lambda p, g: (g, p)),
        out_shape=jax.ShapeDtypeStruct((S, BRANCH_WIDTH), BF16),
        scratch_shapes=[pltpu.VMEM((S // tq, LANES, tq), BF16),
                        pltpu.VMEM((tq, tq), BF16),
                        pltpu.VMEM((2, 2, 1, tq), F32),
                        pltpu.VMEM((2, 2, HEAD_DIM, tq), F32)],
        compiler_params=_params(("arbitrary", "arbitrary")),
        name="stickbreak_attention",
    )(proj, proj, proj)


def _diff_body(slopes_ref, q_ref, k_ref, v_ref, lq1_ref, lk1_ref, lq2_ref, lk2_ref, g_ref, o_ref,
               kaug_ref, vt_ref, knorm_ref, qa_ref, sta_ref, stb_ref, bmaxa_ref, bmaxb_ref, m_ref, acc_ref, *, tq,
               lambda_init):
    h = pl.program_id(0)
    i = pl.program_id(1)
    slope = slopes_ref[0, h]
    inv_slope = slopes_ref[1, h]
    nkb = kaug_ref.shape[0] // tq
    lane = lax.broadcasted_iota(jnp.int32, (1, LANES), 1)

    def max_sq_norms(x):
        sq = x.astype(F32)
        sq = sq * sq
        first = jnp.sum(jnp.where(lane < HEAD_DIM, sq, 0.0), axis=1, keepdims=True)
        second = jnp.sum(jnp.where(lane >= HEAD_DIM, sq, 0.0), axis=1, keepdims=True)
        return jnp.max(first, axis=0, keepdims=True), jnp.max(second, axis=0, keepdims=True)

    @pl.when(i == 0)
    def _():
        def setup(c, carry):
            off = pl.multiple_of(c * tq, tq)
            n1, n2 = max_sq_norms(k_ref[pl.ds(off, tq), :])
            carry = (jnp.maximum(carry[0], n1), jnp.maximum(carry[1], n2))
            pos = off + lax.broadcasted_iota(jnp.int32, (tq, LANES), 0)
            lanes = lax.broadcasted_iota(jnp.int32, (tq, LANES), 1)
            coarse = (pos // POS_SPLIT) * POS_SPLIT
            posm = jnp.where(lanes == 0, coarse, jnp.where(lanes == 1, pos - coarse, 0))
            kaug_ref[pl.ds(off, tq), 0:LANES] = k_ref[pl.ds(off, tq), :]
            kaug_ref[pl.ds(off, tq), LANES:2 * LANES] = posm.astype(F32).astype(BF16)
            vt_ref[c, 0:LANES, :] = v_ref[pl.ds(off, tq), :].astype(F32).T.astype(BF16)
            ones_row = lax.broadcasted_iota(jnp.int32, (SUM_ROWS, tq), 0) == 0
            vt_ref[c, LANES:LANES + SUM_ROWS, :] = jnp.where(ones_row, 1.0, 0.0).astype(BF16)
            return carry

        zero = jnp.zeros((1, 1), F32)
        knorm_ref[0], knorm_ref[1] = lax.fori_loop(0, nkb, setup, (zero, zero))

    q1n, q2n = max_sq_norms(q_ref[...])
    qk = jnp.sqrt(jnp.maximum(q1n * knorm_ref[0], q2n * knorm_ref[1]))
    reach = (ALIBI_CUT + 2.0 * QK_SCALE * qk) * inv_slope
    keep = jnp.minimum(jnp.floor((reach - 1.0) * (1.0 / tq)) + 1.0, 1e6).astype(jnp.int32)
    n_off = jnp.clip(jnp.max(keep), 0, i)
    j0 = i - n_off

    qs = q_ref[...] * QK_SCALE
    bias_cols = jnp.broadcast_to(jnp.where(lane < 2, slope, 0.0).astype(BF16), (tq, LANES))
    for half in range(2):
        in_half = (lane >= HEAD_DIM) if half else (lane < HEAD_DIM)
        qa_ref[half] = jnp.concatenate([jnp.where(in_half, qs, jnp.zeros_like(qs)), bias_cols], axis=1)
    acc_ref[...] = jnp.zeros_like(acc_ref)
    m_ref[...] = jnp.full_like(m_ref, NEG)
    on_or_below_diagonal = (lax.broadcasted_iota(jnp.int32, (tq, tq), 0)
                            <= lax.broadcasted_iota(jnp.int32, (tq, tq), 1))

    def scores(jb, buf):
        st_ref, bmax_ref = buf
        off = pl.multiple_of(jb * tq, tq)
        kb = kaug_ref[pl.ds(off, tq), :]
        for half in range(2):
            st = lax.dot_general(kb, qa_ref[half], (((1,), (1,)), ((), ())),
                                 preferred_element_type=F32)
            st_ref[half] = st
            bmax_ref[half] = jnp.max(st, axis=0, keepdims=True)

    def softmax_pv(jb, buf, diagonal):
        st_ref, bmax_ref = buf
        vtb = vt_ref[jb]
        alphas, ps = [], []
        for half in range(2):
            st = st_ref[half]
            if diagonal:
                st = jnp.where(on_or_below_diagonal, st, NEG)
                block_max = jnp.max(st, axis=0, keepdims=True)
            else:
                block_max = bmax_ref[half]
            m_old = m_ref[half]
            m_new = jnp.maximum(m_old, block_max)
            alpha = jnp.exp(m_old - m_new)
            p = jnp.exp(st - m_new)
            m_ref[half] = m_new
            alphas.append(alpha)
            ps.append(p.astype(BF16))
        for half in range(2):
            acc_ref[half] = alphas[half] * acc_ref[half] + jnp.dot(vtb, ps[half], preferred_element_type=F32)

    buf_a, buf_b = (sta_ref, bmaxa_ref), (stb_ref, bmaxb_ref)
    scores(j0, buf_a)

    def pair(p, carry):
        scores(j0 + 2 * p + 1, buf_b)
        softmax_pv(j0 + 2 * p, buf_a, False)
        scores(j0 + 2 * p + 2, buf_a)
        softmax_pv(j0 + 2 * p + 1, buf_b, False)
        return carry

    lax.fori_loop(0, n_off // 2, pair, 0)

    @pl.when(n_off % 2 == 0)
    def _():
        softmax_pv(i, buf_a, True)

    @pl.when(n_off % 2 == 1)
    def _():
        scores(i, buf_b)
        softmax_pv(i - 1, buf_a, False)
        softmax_pv(i, buf_b, True)

    lam = (jnp.exp(jnp.sum(lq1_ref[...] * lk1_ref[...], axis=1, keepdims=True))
           - jnp.exp(jnp.sum(lq2_ref[...] * lk2_ref[...], axis=1, keepdims=True)) + lambda_init)
    o = (acc_ref[0, 0:LANES, :] * (1.0 / acc_ref[0, LANES:LANES + 1, :])
         - acc_ref[1, 0:LANES, :] * (lam / acc_ref[1, LANES:LANES + 1, :]))
    y = o * lax.rsqrt(jnp.mean(o * o, axis=0, keepdims=True) + RMS_EPS) * g_ref[...]
    o_ref[...] = (y * (1.0 - lambda_init)).T.astype(o_ref.dtype)


def _diff_attention(proj, slopes, lq1, lk1, lq2, lk2, subln_g, lambda_init):
    S = proj.shape[0]
    tq = min(512, S)
    vec = lambda n: pl.BlockSpec((1, n), lambda h, i: (0, 0))
    return pl.pallas_call(
        functools.partial(_diff_body, tq=tq, lambda_init=lambda_init),
        grid=(DIFF_HEADS, S // tq),
        in_specs=[pl.BlockSpec(memory_space=pltpu.SMEM),
                  pl.BlockSpec((tq, LANES), lambda h, i: (i, COL_QC // LANES + h)),
                  pl.BlockSpec((S, LANES), lambda h, i: (0, COL_KC // LANES + h)),
                  pl.BlockSpec((S, LANES), lambda h, i: (0, COL_VC // LANES + h)),
                  vec(HEAD_DIM), vec(HEAD_DIM), vec(HEAD_DIM), vec(HEAD_DIM),
                  pl.BlockSpec((2 * HEAD_DIM, 1), lambda h, i: (0, 0))],
        out_specs=pl.BlockSpec((tq, LANES), lambda h, i: (i, h)),
        out_shape=jax.ShapeDtypeStruct((S, BRANCH_WIDTH), BF16),
        scratch_shapes=[pltpu.VMEM((S, 2 * LANES), BF16),
                        pltpu.VMEM((S // tq, LANES + SUM_ROWS, tq), BF16),
                        pltpu.VMEM((2, 1, 1), F32),
                        pltpu.VMEM((2, tq, 2 * LANES), BF16),
                        pltpu.VMEM((2, tq, tq), F32), pltpu.VMEM((2, tq, tq), F32),
                        pltpu.VMEM((2, 1, tq), F32), pltpu.VMEM((2, 1, tq), F32),
                        pltpu.VMEM((2, 1, tq), F32),
                        pltpu.VMEM((2, LANES + SUM_ROWS, tq), F32)],
        compiler_params=_params(("arbitrary", "arbitrary")),
        name="diff_attention",
    )(slopes, proj, proj, proj, lq1.reshape(1, -1), lk1.reshape(1, -1), lq2.reshape(1, -1),
      lk2.reshape(1, -1), subln_g.reshape(-1, 1))


def _branch_body(oa_ref, ob_ref, oc_ref, w_ref, ga_ref, gb_ref, gc_ref, z_ref):
    z = None
    for n, (o_ref, g_ref) in enumerate(((oa_ref, ga_ref), (ob_ref, gb_ref), (oc_ref, gc_ref))):
        y = jnp.dot(o_ref[...], w_ref[0, n].astype(BF16), preferred_element_type=F32)
        t = g_ref[...].astype(F32) * y
        z = t if z is None else z + t
    z_ref[...] = z.astype(z_ref.dtype)


def _branch_merge(o_a, o_b, o_c, w_branch, layer, gates):
    S = o_a.shape[0]
    D = w_branch.shape[3]
    tm = min(1024, S)
    tn = 512
    nj = D // tn
    o_spec = pl.BlockSpec((tm, BRANCH_WIDTH), lambda i, j: (i, 0))
    gate_spec = lambda n: pl.BlockSpec((tm, tn), lambda i, j: (i, n * nj + j))
    return pl.pallas_call(
        _branch_body,
        grid=(S // tm, nj),
        in_specs=[o_spec, o_spec, o_spec,
                  pl.BlockSpec((1, N_BRANCHES, BRANCH_WIDTH, tn), lambda i, j: (layer, 0, 0, j)),
                  gate_spec(0), gate_spec(1), gate_spec(2)],
        out_specs=pl.BlockSpec((tm, tn), lambda i, j: (i, j)),
        out_shape=jax.ShapeDtypeStruct((S, D), BF16),
        compiler_params=_params(("arbitrary", "arbitrary")),
        name="branch_merge",
    )(o_a, o_b, o_c, w_branch, gates, gates, gates)


def _layer_norm(r, g, b):
    mu = jnp.mean(r, axis=1, keepdims=True)
    d = r - mu
    var = jnp.mean(d * d, axis=1, keepdims=True)
    return d * lax.rsqrt(var + LN_EPS) * g + b


def _split_bf16(a):
    hi = a.astype(BF16)
    return hi, (a - hi.astype(F32)).astype(BF16)


ROUTE_EXPERT, ROUTE_WEIGHT, ROUTE_RANK = 0, 2, 4


def _route_tile(logits, counts):
    rows = logits.shape[0]
    lane = lax.broadcasted_iota(jnp.int32, logits.shape, 1)
    row_min = lambda cond: jnp.min(jnp.where(cond, lane, LANES), axis=1, keepdims=True)
    row_max = lambda cond: jnp.max(jnp.where(cond, logits, NEG), axis=1, keepdims=True)
    is_group = lane < N_GROUPS
    g_max = row_max(is_group)
    g_top = row_min(is_group & (logits == g_max))
    p_group = 1.0 / jnp.sum(jnp.where(is_group, jnp.exp(logits - g_max), 0.0), axis=1, keepdims=True)
    e_lane = lane - N_GROUPS
    in_group = ((e_lane >= 0) & (e_lane < N_EXPERTS)
                & (jnp.right_shift(e_lane, EXPERTS_PER_GROUP.bit_length() - 1) == g_top))
    v1 = row_max(in_group)
    i1 = row_min(in_group & (logits == v1))
    rest = in_group & (lane != i1)
    v2 = row_max(rest)
    i2 = row_min(rest & (logits == v2))
    ratio = jnp.exp(v2 - v1)
    w1 = p_group / (1.0 + ratio)
    w2 = w1 * ratio
    member = (lane == i1) | (lane == i2)
    earlier = (lax.broadcasted_iota(jnp.int32, (rows, rows), 1)
               < lax.broadcasted_iota(jnp.int32, (rows, rows), 0)).astype(BF16)
    before = jnp.dot(earlier, member.astype(BF16), preferred_element_type=F32) + counts
    pick = lambda idx: jnp.sum(jnp.where(lane == idx, before, 0.0), axis=1, keepdims=True)
    record = jnp.zeros(logits.shape, F32)
    for at, value in ((ROUTE_EXPERT, (i1 - N_GROUPS).astype(F32)), (ROUTE_EXPERT + 1, (i2 - N_GROUPS).astype(F32)),
                      (ROUTE_WEIGHT, w1), (ROUTE_WEIGHT + 1, w2),
                      (ROUTE_RANK, pick(i1)), (ROUTE_RANK + 1, pick(i2))):
        record = jnp.where(lane == at, value, record)
    return record, counts + jnp.sum(member.astype(F32), axis=0, keepdims=True)


def _out_body(z_ref, w_ref, x_ref, g1_ref, lng_ref, lnb_ref, sc2_ref, sh2_ref, wr_hi_ref, wr_lo_ref, br_ref,
              x1_ref, route_ref, counts_ref, running_ref, *, alpha):
    @pl.when(pl.program_id(0) == 0)
    def _():
        running_ref[...] = jnp.zeros_like(running_ref)

    y = jnp.dot(z_ref[...], w_ref[...], preferred_element_type=F32)
    x1 = _layer_norm(alpha * x_ref[...] + g1_ref[...] * y, lng_ref[...], lnb_ref[...])
    x1_ref[...] = x1
    h_hi, h_lo = _split_bf16(x1 * (1.0 + sc2_ref[...]) + sh2_ref[...])
    w_hi = wr_hi_ref[...]
    logits = (jnp.dot(h_hi, w_hi, preferred_element_type=F32)
              + (jnp.dot(h_hi, wr_lo_ref[...], preferred_element_type=F32)
                 + jnp.dot(h_lo, w_hi, preferred_element_type=F32))) + br_ref[...]
    route_ref[...], counts = _route_tile(logits, running_ref[...])
    running_ref[...] = counts
    counts_ref[...] = jnp.broadcast_to(counts, counts_ref.shape)


def _mixer_out(z, w_out, x, g1, ln_g, ln_b, sc2, sh2, w_router, b_router, alpha):
    S, D = x.shape
    tm = min(256, S)
    row = pl.BlockSpec((tm, D), lambda i: (i, 0))
    vec = pl.BlockSpec((1, D), lambda i: (0, 0))
    router = pl.BlockSpec((D, LANES), lambda i: (0, 0))
    wr_hi, wr_lo = _split_bf16(w_router)
    return pl.pallas_call(
        functools.partial(_out_body, alpha=alpha),
        grid=(S // tm,),
        in_specs=[row, pl.BlockSpec((D, D), lambda i: (0, 0)), row, vec, vec, vec, vec, vec,
                  router, router, pl.BlockSpec((1, LANES), lambda i: (0, 0))],
        out_specs=[row, pl.BlockSpec((tm, LANES), lambda i: (i, 0)), pl.BlockSpec((8, LANES), lambda i: (0, 0))],
        out_shape=[jax.ShapeDtypeStruct((S, D), F32), jax.ShapeDtypeStruct((S, LANES), F32),
                   jax.ShapeDtypeStruct((8, LANES), F32)],
        scratch_shapes=[pltpu.VMEM((1, LANES), F32)],
        compiler_params=_params(("arbitrary",)),
        name="mixer_out_ln",
    )(z, w_out, x, g1, ln_g, ln_b, sc2, sh2, wr_hi, wr_lo, b_router)


def _slot(choice_ref, pstart_ref, at):
    code = choice_ref[at]
    return pstart_ref[code >> RANK_BITS] + (code & ((1 << RANK_BITS) - 1))


def _dispatch_body(choice_ref, pstart_ref, pend_ref, nu_ref, x_ref, xb_hbm, zero_ref, sem, zero_sem, *,
                   rows_per_step, nblk):
    i = pl.program_id(0)
    n_tokens = pl.num_programs(0) * rows_per_step
    DB = DISPATCH_BLOCK

    @pl.when(i == 0)
    def _():
        zero_ref[...] = jnp.zeros_like(zero_ref)

        def zero_copy(row):
            return pltpu.make_async_copy(zero_ref, xb_hbm.at[pl.ds(pl.multiple_of(row, DB), DB)], zero_sem)

        def has_rows(e):
            return pend_ref[e] > pstart_ref[e]

        def tail_start(b, carry):
            zero_copy(b * DB).start()
            return carry

        def tail_wait(b, carry):
            zero_copy(0).wait()
            return carry

        for e in range(N_EXPERTS):
            @pl.when(has_rows(e))
            def _():
                zero_copy(pend_ref[e] - DB).start()
        lax.fori_loop(nu_ref[0], nblk, tail_start, 0)
        for e in range(N_EXPERTS):
            @pl.when(has_rows(e))
            def _():
                zero_copy(0).wait()
        lax.fori_loop(nu_ref[0], nblk, tail_wait, 0)

    def row_copy(r, d):
        return pltpu.make_async_copy(x_ref.at[pl.ds(r, 1)], xb_hbm.at[pl.ds(d, 1)], sem)

    def issue(g, carry):
        for u in range(ROW_UNROLL):
            r = g * ROW_UNROLL + u
            t = i * rows_per_step + r
            for k in range(2):
                row_copy(r, _slot(choice_ref, pstart_ref, k * n_tokens + t)).start(priority=k)
        return carry

    def wait(g, carry):
        for _ in range(2 * ROW_UNROLL):
            row_copy(0, 0).wait()
        return carry

    lax.fori_loop(0, rows_per_step // ROW_UNROLL, issue, 0)
    lax.fori_loop(0, rows_per_step // ROW_UNROLL, wait, 0)


def _dispatch(x1, slots, pend, n_used, padded_rows):
    T, D = x1.shape
    rows_per_step = min(512, T)
    nblk = padded_rows // DISPATCH_BLOCK
    return pl.pallas_call(
        functools.partial(_dispatch_body, rows_per_step=rows_per_step, nblk=nblk),
        grid_spec=pltpu.PrefetchScalarGridSpec(
            num_scalar_prefetch=4,
            grid=(T // rows_per_step,),
            in_specs=[pl.BlockSpec((rows_per_step, D), lambda i, *_: (i, 0))],
            out_specs=pl.BlockSpec(memory_space=pl.ANY),
            scratch_shapes=[pltpu.VMEM((DISPATCH_BLOCK, D), F32), pltpu.SemaphoreType.DMA(()),
                            pltpu.SemaphoreType.DMA(())]),
        out_shape=jax.ShapeDtypeStruct((padded_rows, D), F32),
        compiler_params=_params(("arbitrary",)),
        name="moe_dispatch",
    )(*slots, pend, n_used, x1)


def _expert_body(be_ref, nu_ref, first_ref, slot_ref, next_ref, xb_ref, sc_ref, sh_ref,
                 wg_hbm, wu_hbm, wd_hbm, yb_ref, wg_f, wu_f, wd_f, wg_s, wu_s, wd_s, sem, *, layer):
    b = pl.program_id(0)
    nu = nu_ref[0]
    base = layer * N_EXPERTS

    def weight_copies(e, slot):
        return (pltpu.make_async_copy(wg_hbm.at[base + e], wg_f.at[slot], sem.at[slot, 0]),
                pltpu.make_async_copy(wu_hbm.at[base + e], wu_f.at[slot], sem.at[slot, 1]),
                pltpu.make_async_copy(wd_hbm.at[base + e], wd_f.at[slot], sem.at[slot, 2]))

    @pl.when(b == 0)
    def _():
        for copy in weight_copies(be_ref[0], 0):
            copy.start()

    @pl.when((b < nu) & (first_ref[b] == 1))
    def _():
        e = be_ref[b]
        nxt = next_ref[b]
        for slot in range(2):
            @pl.when(slot_ref[b] == slot)
            def _():
                for copy in weight_copies(e, slot):
                    copy.wait()

                @pl.when(nxt >= 0)
                def _():
                    for copy in weight_copies(nxt, 1 - slot):
                        copy.start()

                wg_s[...] = wg_f[slot].astype(BF16)
                wu_s[...] = wu_f[slot].astype(BF16)
                wd_s[...] = wd_f[slot].astype(BF16)

    @pl.when(b < nu)
    def _():
        h = (xb_ref[...] * (1.0 + sc_ref[...]) + sh_ref[...]).astype(BF16)
        a = jnp.dot(h, wg_s[...], preferred_element_type=F32)
        u = jnp.dot(h, wu_s[...], preferred_element_type=F32)
        act = (a / (1.0 + jnp.exp(-a))) * u
        yb_ref[...] = jnp.dot(act.astype(BF16), wd_s[...], preferred_element_type=F32)

    @pl.when(b >= nu)
    def _():
        yb_ref[...] = jnp.zeros_like(yb_ref)


def _experts(xb, sc2, sh2, w_g, w_u, w_d, plan, layer):
    P, D = xb.shape
    DB = DISPATCH_BLOCK
    nblk = P // DB
    blk = lambda b, be, nu, *_: (jnp.minimum(b, nu[0] - 1), 0)
    vec = pl.BlockSpec((1, D), lambda b, *_: (0, 0))
    hbm = pl.BlockSpec(memory_space=pl.ANY)
    return pl.pallas_call(
        functools.partial(_expert_body, layer=layer),
        grid_spec=pltpu.PrefetchScalarGridSpec(
            num_scalar_prefetch=5,
            grid=(nblk,),
            in_specs=[pl.BlockSpec((DB, D), blk), vec, vec, hbm, hbm, hbm],
            out_specs=pl.BlockSpec((DB, D), lambda b, *_: (b, 0)),
            scratch_shapes=[pltpu.VMEM((2, D, D_EXPERT), F32), pltpu.VMEM((2, D, D_EXPERT), F32),
                            pltpu.VMEM((2, D_EXPERT, D), F32),
                            pltpu.VMEM((D, D_EXPERT), BF16), pltpu.VMEM((D, D_EXPERT), BF16),
                            pltpu.VMEM((D_EXPERT, D), BF16),
                            pltpu.SemaphoreType.DMA((2, 3))]),
        out_shape=jax.ShapeDtypeStruct((P, D), F32),
        compiler_params=_params(("arbitrary",)),
        name="moe_experts",
    )(*plan, xb, sc2, sh2, w_g, w_u, w_d)


def _combine_body(choice_ref, pstart_ref, x1_ref, wt_ref, g2_ref, lng_ref, lnb_ref, yb_hbm, o_ref,
                  buf0, buf1, sem, *, tm, alpha):
    i = pl.program_id(0)
    n_tokens = pl.num_programs(0) * tm
    bufs = (buf0, buf1)

    def row_copy(d, k, r):
        return pltpu.make_async_copy(yb_hbm.at[pl.ds(d, 1)], bufs[k].at[pl.ds(r, 1)], sem)

    def issue(g, carry):
        for u in range(ROW_UNROLL):
            r = g * ROW_UNROLL + u
            t = i * tm + r
            for k in range(2):
                row_copy(_slot(choice_ref, pstart_ref, k * n_tokens + t), k, r).start(priority=k)
        return carry

    def wait(g, carry):
        for _ in range(ROW_UNROLL):
            for k in range(2):
                row_copy(0, k, 0).wait()
        return carry

    lax.fori_loop(0, tm // ROW_UNROLL, issue, 0)
    lax.fori_loop(0, tm // ROW_UNROLL, wait, 0)
    wt = wt_ref[...]
    y = buf0[...] * wt[:, 0:1] + buf1[...] * wt[:, 1:2]
    o_ref[...] = _layer_norm(alpha * x1_ref[...] + g2_ref[...] * y, lng_ref[...], lnb_ref[...])


def _combine_ln(x1, yb, slots, w_top, g2, ln_g, ln_b, alpha):
    T, D = x1.shape
    tm = min(256, T)
    row = pl.BlockSpec((tm, D), lambda i, *_: (i, 0))
    vec = pl.BlockSpec((1, D), lambda i, *_: (0, 0))
    return pl.pallas_call(
        functools.partial(_combine_body, tm=tm, alpha=alpha),
        grid_spec=pltpu.PrefetchScalarGridSpec(
            num_scalar_prefetch=2,
            grid=(T // tm,),
            in_specs=[row, pl.BlockSpec((tm, 2), lambda i, *_: (i, 0)), vec, vec, vec,
                      pl.BlockSpec(memory_space=pl.ANY)],
            out_specs=row,
            scratch_shapes=[pltpu.VMEM((tm, D), F32), pltpu.VMEM((tm, D), F32),
                            pltpu.SemaphoreType.DMA(())]),
        out_shape=jax.ShapeDtypeStruct((T, D), F32),
        compiler_params=_params(("arbitrary",)),
        name="moe_combine_ln",
    )(*slots, x1, w_top, g2, ln_g, ln_b, yb)


def _route(route, counts):
    T = route.shape[0]
    DB = DISPATCH_BLOCK
    w_top = route[:, ROUTE_WEIGHT:ROUTE_WEIGHT + 2]
    by_choice = route[:, :8].T
    expert_id = by_choice[ROUTE_EXPERT:ROUTE_EXPERT + 2].astype(jnp.int32)
    rank = by_choice[ROUTE_RANK:ROUTE_RANK + 2].astype(jnp.int32)
    counts = counts[0, N_GROUPS:N_GROUPS + N_EXPERTS].astype(jnp.int32)
    padded = (counts + DB - 1) // DB * DB
    pend = jnp.cumsum(padded)
    pstart = pend - padded
    choice = ((expert_id << RANK_BITS) | rank).astype(jnp.int32).reshape(-1)
    P = ((2 * T + DB - 1) // DB) * DB + N_EXPERTS * DB
    nblk = P // DB
    block_e = jnp.minimum(jnp.searchsorted(pend, jnp.arange(nblk, dtype=jnp.int32) * DB, side='right'),
                          N_EXPERTS - 1).astype(jnp.int32)
    n_used = (pend[-1:] // DB).astype(jnp.int32)
    b_idx = jnp.arange(nblk, dtype=jnp.int32)
    first = (b_idx < n_used) & ((b_idx == 0) | (block_e != jnp.roll(block_e, 1)))
    slot = (jnp.cumsum(first.astype(jnp.int32)) - 1) % 2
    after = (pend // DB).astype(jnp.int32)[block_e]
    next_e = jnp.where(after < n_used, block_e[jnp.minimum(after, nblk - 1)], -1)
    plan = (block_e, n_used, first.astype(jnp.int32), slot.astype(jnp.int32), next_e.astype(jnp.int32))
    return (choice, pstart.astype(jnp.int32)), pend.astype(jnp.int32), w_top, plan, P


def kernel(x, c, w_ada, b_ada, w_in, w_branch_gate, b_branch_gate, attn_sinks, lambda_q1, lambda_k1,
           lambda_q2, lambda_k2, subln_g, w_branch, w_out, ln1_g, ln1_b, w_router_group, b_router_group,
           w_router_expert, b_router_expert, w_exp_gate, w_exp_up, w_exp_down, ln2_g, ln2_b):
    B, S, D = x.shape
    assert B == 1 and D == D_MODEL
    depth = w_in.shape[0]
    alpha = ALPHA
    xs = x.reshape(S, D)
    mod = _ada_mod(c, w_ada, b_ada)
    diff_slopes = jnp.exp2(-8.0 * jnp.arange(1, DIFF_HEADS + 1, dtype=F32) / DIFF_HEADS)
    diff_slopes = jnp.stack([diff_slopes, 1.0 / diff_slopes])
    zero_bias = jnp.zeros((1, w_in.shape[2]), F32)
    pad = LANES - N_GROUPS - N_EXPERTS
    w_eg = w_exp_gate.reshape(depth * N_EXPERTS, D, D_EXPERT)
    w_eu = w_exp_up.reshape(depth * N_EXPERTS, D, D_EXPERT)
    w_ed = w_exp_down.reshape(depth * N_EXPERTS, D_EXPERT, D)
    for l in range(depth):
        lambda_init = 0.8 - 0.6 * math.exp(-0.3 * l)
        sh1, sc1, g1, sh2, sc2, g2 = [mod[l, :, n * D:(n + 1) * D] for n in range(6)]
        proj = _mod_matmul(xs, sc1, sh1, w_in, l, zero_bias, sigmoid=False, name="in_proj")
        gates = _mod_matmul(xs, sc1, sh1, w_branch_gate, l, b_branch_gate[l].reshape(1, -1),
                            sigmoid=True, name="branch_gates")
        o_a = _swa_attention(proj, attn_sinks[l])
        o_b = _sb_attention(proj)
        o_c = _diff_attention(proj, diff_slopes, lambda_q1[l], lambda_k1[l], lambda_q2[l], lambda_k2[l],
                              subln_g[l], lambda_init)
        z = _branch_merge(o_a, o_b, o_c, w_branch, l, gates)
        w_router = jnp.pad(jnp.concatenate([w_router_group[l], w_router_expert[l]], axis=1), ((0, 0), (0, pad)))
        b_router = jnp.pad(jnp.concatenate([b_router_group[l], b_router_expert[l]]), (0, pad)).reshape(1, LANES)
        x1, route, counts = _mixer_out(z, w_out[l].astype(BF16), xs, g1, ln1_g[l].reshape(1, D),
                                       ln1_b[l].reshape(1, D), sc2, sh2, w_router, b_router, alpha)
        slots, pend, w_top, plan, P = _route(route, counts)
        xb = _dispatch(x1, slots, pend, plan[1], P)
        yb = _experts(xb, sc2, sh2, w_eg, w_eu, w_ed, plan, l)
        xs = _combine_ln(x1, yb, slots, w_top, g2, ln2_g[l].reshape(1, D), ln2_b[l].reshape(1, D), alpha)
    return xs.reshape(B, S, D)
```
